```python
import math
import jax, jax.numpy as jnp
from jax import lax
import numpy as np

D_MODEL = 2048
BATCH = 4
SEQ = 4096
DEPTH = 1

N_META = 16
D_CONV = 2048
CONV_K = 3
D_POOL = 1024
POOL_WINDOWS = (2, 4, 8, 16)
N_POOL_GROUPS = len(POOL_WINDOWS)
POOL_GROUP_IN = D_POOL // N_POOL_GROUPS
POOL_GROUP_OUT = D_MODEL // N_POOL_GROUPS
N_BRANCH = 2
D_IN_PROJ = 3 * D_CONV + D_POOL + N_BRANCH * D_MODEL
N_EXPERTS = 32
TOP_K = 4
D_EXPERT = 2048
SWIGLU_LIMIT = 7.0
SWIGLU_ALPHA = 1.702
LN_EPS = 1e-5
DEEPNORM_ALPHA = (2.0 * DEPTH) ** 0.25
DEEPNORM_BETA = (8.0 * DEPTH) ** -0.25

kernel_name = "hybrid_conv_pool_moe_deepnorm_layer"


def layer_norm(x, g, b):
    xf = x.astype(jnp.float32)
    mu = jnp.mean(xf, axis=-1, keepdims=True)
    var = jnp.mean(jnp.square(xf - mu), axis=-1, keepdims=True)
    y = (xf - mu) * lax.rsqrt(var + LN_EPS) * g.astype(jnp.float32) + b.astype(jnp.float32)
    return y.astype(x.dtype)


def causal_short_conv(z, w):
    L = z.shape[1]
    zp = jnp.pad(z, ((0, 0), (CONV_K - 1, 0), (0, 0)))
    y = w[0] * zp[:, 0:L]
    for k in range(1, CONV_K):
        y = y + w[k] * zp[:, k:k + L]
    return y


def multiscale_pool(v, pool_w, pool_scale):
    B, L, _ = v.shape
    vf = v.astype(jnp.float32).reshape(B, L, N_POOL_GROUPS, POOL_GROUP_IN)
    cs = jnp.concatenate([jnp.zeros_like(vf[:, :1]), jnp.cumsum(vf, axis=1)], axis=1)
    t = jnp.arange(L, dtype=jnp.int32)[:, None]
    win = jnp.asarray(POOL_WINDOWS, dtype=jnp.int32)[None, :]
    lo = jnp.maximum(t + 1 - win, 0)
    cs_lo = cs[:, lo, jnp.arange(N_POOL_GROUPS)[None, :], :]
    count = (t + 1 - lo).astype(jnp.float32)
    pooled = (cs[:, 1:] - cs_lo) / count[None, :, :, None] - vf
    pooled = pooled.astype(v.dtype)
    y = jnp.einsum('blgc,gcd->blgd', pooled, pool_w)
    return y.reshape(B, L, D_MODEL) * pool_scale


def token_mixer(h, w_in, conv_w, w_a_out, pool_w, pool_scale, w_o):
    proj = jnp.einsum('bld,de->ble', h, w_in)
    b_gate, c_gate, u, v, gates = jnp.split(
        proj, [D_CONV, 2 * D_CONV, 3 * D_CONV, 3 * D_CONV + D_POOL], axis=-1)
    y_a = jnp.einsum('blc,cd->bld', b_gate * causal_short_conv(c_gate * u, conv_w), w_a_out)
    y_b = multiscale_pool(v, pool_w, pool_scale)
    g = jax.nn.sigmoid(gates)
    merged = g[..., :D_MODEL] * y_a + g[..., D_MODEL:] * y_b
    return jnp.einsum('bld,de->ble', merged, w_o)


def moe(h, router_w, router_b, w_gate_up, b_gate_up, w_down, b_down):
    B, L, D = h.shape
    tok = h.reshape(B * L, D)
    logits = (tok @ router_w + router_b).astype(jnp.float32)
    top_v, top_i = lax.top_k(logits, TOP_K)
    wts = jax.nn.softmax(top_v, axis=-1)
    flat_e = top_i.reshape(-1)
    order = jnp.argsort(flat_e)
    e_sorted = flat_e[order]
    tok_idx = order // TOP_K
    xs = tok[tok_idx]
    group_sizes = jnp.bincount(flat_e, length=N_EXPERTS).astype(jnp.int32)
    gu = lax.ragged_dot(xs, w_gate_up, group_sizes) + b_gate_up[e_sorted]
    gate = jnp.minimum(gu[:, :D_EXPERT], SWIGLU_LIMIT)
    lin = jnp.clip(gu[:, D_EXPERT:], -SWIGLU_LIMIT, SWIGLU_LIMIT)
    act = (lin + 1.0) * gate * jax.nn.sigmoid(SWIGLU_ALPHA * gate)
    out = lax.ragged_dot(act, w_down, group_sizes) + b_down[e_sorted]
    out = out * wts.reshape(-1)[order][:, None].astype(out.dtype)
    y = jnp.zeros_like(tok).at[tok_idx].add(out)
    return y.reshape(B, L, D)


def setup_inputs(seed: int = 0) -> dict:
    key = jax.random.key(seed)
    ks = jax.random.split(key, 20)
    nrm = lambda k, shape, s: jax.random.normal(k, shape, jnp.float32) * s
    Lyr = DEPTH
    return {
        "x": nrm(ks[0], (BATCH, SEQ, D_MODEL), 1.0),
        "meta_tokens": nrm(ks[1], (N_META, D_MODEL), 1.0),
        "ln_in_g": 1.0 + nrm(ks[2], (D_MODEL,), 0.02),
        "ln_in_b": nrm(ks[3], (D_MODEL,), 0.02),
        "w_in": nrm(ks[4], (Lyr, D_MODEL, D_IN_PROJ), D_MODEL ** -0.5),
        "conv_w": nrm(ks[5], (Lyr, CONV_K, D_CONV), CONV_K ** -0.5),
        "w_a_out": nrm(ks[6], (Lyr, D_CONV, D_MODEL), D_CONV ** -0.5 * DEEPNORM_BETA),
        "pool_w": nrm(ks[7], (Lyr, N_POOL_GROUPS, POOL_GROUP_IN, POOL_GROUP_OUT), POOL_GROUP_IN ** -0.5 * DEEPNORM_BETA),
        "pool_scale": 1.0 + nrm(ks[8], (Lyr, D_MODEL), 0.02),
        "w_o": nrm(ks[9], (Lyr, D_MODEL, D_MODEL), D_MODEL ** -0.5 * DEEPNORM_BETA),
        "ln1_g": 1.0 + nrm(ks[10], (Lyr, D_MODEL), 0.02),
        "ln1_b": nrm(ks[11], (Lyr, D_MODEL), 0.02),
        "router_w": nrm(ks[12], (Lyr, D_MODEL, N_EXPERTS), D_MODEL ** -0.5),
        "router_b": nrm(ks[13], (Lyr, N_EXPERTS), 0.01),
        "w_gate_up": nrm(ks[14], (Lyr, N_EXPERTS, D_MODEL, 2 * D_EXPERT), D_MODEL ** -0.5),
        "b_gate_up": nrm(ks[15], (Lyr, N_EXPERTS, 2 * D_EXPERT), 0.02),
        "w_down": nrm(ks[16], (Lyr, N_EXPERTS, D_EXPERT, D_MODEL), D_EXPERT ** -0.5 * DEEPNORM_BETA),
        "b_down": nrm(ks[17], (Lyr, N_EXPERTS, D_MODEL), 0.02),
        "ln2_g": 1.0 + nrm(ks[18], (Lyr, D_MODEL), 0.02),
        "ln2_b": nrm(ks[19], (Lyr, D_MODEL), 0.02),
    }


def reference(x, meta_tokens, ln_in_g, ln_in_b, w_in, conv_w, w_a_out, pool_w, pool_scale, w_o,
              ln1_g, ln1_b, router_w, router_b, w_gate_up, b_gate_up, w_down, b_down, ln2_g, ln2_b):
    B = x.shape[0]
    meta = jnp.broadcast_to(meta_tokens[None].astype(x.dtype), (B, N_META, D_MODEL))
    h = jnp.concatenate([meta, x], axis=1)
    h = layer_norm(h, ln_in_g, ln_in_b)
    for i in range(DEPTH):
        mix = token_mixer(h, w_in[i], conv_w[i], w_a_out[i], pool_w[i], pool_scale[i], w_o[i])
        h = layer_norm(DEEPNORM_ALPHA * h + mix, ln1_g[i], ln1_b[i])
        ffn = moe(h, router_w[i], router_b[i], w_gate_up[i], b_gate_up[i], w_down[i], b_down[i])
        h = layer_norm(DEEPNORM_ALPHA * h + ffn, ln2_g[i], ln2_b[i])
    return h[:, N_META:]
```

```python
import functools

import jax
import jax.numpy as jnp
from jax import lax
from jax.experimental import pallas as pl
from jax.experimental.pallas import tpu as pltpu

F32 = jnp.float32
BF16 = jnp.bfloat16
I32 = jnp.int32

N_META = 16
CONV_K = 3
POOL_WINDOWS = (2, 4, 8, 16)
TOP_K = 4
SWIGLU_LIMIT = 7.0
SWIGLU_ALPHA = 1.702
LN_EPS = 1e-5
DEPTH = 1
DEEPNORM_ALPHA = (2.0 * DEPTH) ** 0.25

HALO = 16
POOL_PAD = 128
V7X_VMEM_LIMIT = 56 * 1024 * 1024

_ARB = "arbitrary"


def _params(n_axes):
    return pltpu.CompilerParams(dimension_semantics=(_ARB,) * n_axes, vmem_limit_bytes=V7X_VMEM_LIMIT)


def _layer_norm(x, g, b):
    mu = jnp.mean(x, axis=-1, keepdims=True)
    xc = x - mu
    var = jnp.mean(xc * xc, axis=-1, keepdims=True)
    return xc * lax.rsqrt(var + LN_EPS) * g + b


def _dot(a, b):
    return jnp.dot(a, b, preferred_element_type=F32)


def _ln_cast_kernel(x_ref, g_ref, b_ref, o_ref):
    o_ref[...] = _layer_norm(x_ref[...], g_ref[...], b_ref[...]).astype(o_ref.dtype)


def _ln_cast(x2d, g, b, tr):
    n, d = x2d.shape
    return pl.pallas_call(
        _ln_cast_kernel,
        grid=(n // tr,),
        in_specs=[pl.BlockSpec((tr, d), lambda i: (i, 0)),
                  pl.BlockSpec((1, d), lambda i: (0, 0)),
                  pl.BlockSpec((1, d), lambda i: (0, 0))],
        out_specs=pl.BlockSpec((tr, d), lambda i: (i, 0)),
        out_shape=jax.ShapeDtypeStruct((n, d), BF16),
        compiler_params=_params(1),
        name="ln_cast",
    )(x2d, g, b)


def _conv_kernel(h_ref, halo_ref, hm_ref, wb_ref, wc_ref, wu_ref, cw_ref, o_ref, he_ref, cu_ref, *, tm):
    i = pl.program_id(1)
    j = pl.program_id(2)

    @pl.when(j == 0)
    def _():
        he_ref[HALO:, :] = h_ref[0]

    @pl.when((j == 0) & (i == 0))
    def _():
        he_ref[0:HALO, :] = hm_ref[...]

    @pl.when((j == 0) & (i > 0))
    def _():
        he_ref[0:HALO, :] = halo_ref[0]

    he = he_ref[...]
    cu_ref[...] = _dot(he, wc_ref[...]) * _dot(he, wu_ref[...])
    bg = _dot(he_ref[HALO:, :], wb_ref[...])
    cw = cw_ref[...]
    y = cw[CONV_K - 1:CONV_K, :] * cu_ref[HALO:HALO + tm, :]
    for k in range(CONV_K - 1):
        s = HALO - (CONV_K - 1) + k
        y = y + cw[k:k + 1, :] * cu_ref[s:s + tm, :]
    o_ref[0] = (bg * y).astype(o_ref.dtype)


def _conv_branch(hx, hm, w_in_b, conv_w, d_conv, tm, tc):
    bsz, seq, d = hx.shape
    nc = d_conv // tc
    hb = tm // HALO
    return pl.pallas_call(
        functools.partial(_conv_kernel, tm=tm),
        grid=(bsz, seq // tm, nc),
        in_specs=[pl.BlockSpec((1, tm, d), lambda b, i, j: (b, i, 0)),
                  pl.BlockSpec((1, HALO, d), lambda b, i, j: (b, jnp.maximum(i * hb - 1, 0), 0)),
                  pl.BlockSpec((HALO, d), lambda b, i, j: (0, 0)),
                  pl.BlockSpec((d, tc), lambda b, i, j: (0, j)),
                  pl.BlockSpec((d, tc), lambda b, i, j: (0, nc + j)),
                  pl.BlockSpec((d, tc), lambda b, i, j: (0, 2 * nc + j)),
                  pl.BlockSpec((CONV_K, tc), lambda b, i, j: (0, j))],
        out_specs=pl.BlockSpec((1, tm, tc), lambda b, i, j: (b, i, j)),
        out_shape=jax.ShapeDtypeStruct((bsz, seq, d_conv), BF16),
        scratch_shapes=[pltpu.VMEM((tm + HALO, d), BF16), pltpu.VMEM((tm + HALO, tc), F32)],
        compiler_params=_params(3),
        name="conv_branch",
    )(hx, hx, hm, w_in_b, w_in_b, w_in_b, conv_w)


def _pool_kernel(h_ref, halo_ref, hm_ref, wv_ref, band_ref, pw_ref, ps_ref, o_ref, he_ref):
    i = pl.program_id(1)
    g = pl.program_id(2)
    lead = POOL_PAD - HALO

    @pl.when(g == 0)
    def _():
        he_ref[0:lead, :] = jnp.zeros((lead, he_ref.shape[1]), he_ref.dtype)
        he_ref[POOL_PAD:, :] = h_ref[0]

    @pl.when((g == 0) & (i == 0))
    def _():
        he_ref[lead:POOL_PAD, :] = hm_ref[...]

    @pl.when((g == 0) & (i > 0))
    def _():
        he_ref[lead:POOL_PAD, :] = halo_ref[0]

    v = _dot(he_ref[...], wv_ref[...])
    v_hi = v.astype(BF16)
    v_lo = (v - v_hi.astype(F32)).astype(BF16)
    band = band_ref[0]
    pooled = _dot(band, v_hi) + _dot(band, v_lo)
    o_ref[0] = (_dot(pooled.astype(BF16), pw_ref[0]) * ps_ref[...]).astype(o_ref.dtype)


def _pool_band(tm):
    t = jnp.arange(tm, dtype=I32)[:, None] + POOL_PAD
    s = jnp.arange(tm + POOL_PAD, dtype=I32)[None, :]
    bands = []
    for w in POOL_WINDOWS:
        inside = ((s <= t) & (s > t - w)).astype(F32) / w
        bands.append(inside - (s == t).astype(F32))
    return jnp.stack(bands).astype(BF16)


def _pool_branch(hx, hm, w_in_b, pool_w_b, pool_scale, col0, tm):
    bsz, seq, d = hx.shape
    ng, cin, cout = pool_w_b.shape
    hb = tm // HALO
    band = _pool_band(tm)
    return pl.pallas_call(
        _pool_kernel,
        grid=(bsz, seq // tm, ng),
        in_specs=[pl.BlockSpec((1, tm, d), lambda b, i, g: (b, i, 0)),
                  pl.BlockSpec((1, HALO, d), lambda b, i, g: (b, jnp.maximum(i * hb - 1, 0), 0)),
                  pl.BlockSpec((HALO, d), lambda b, i, g: (0, 0)),
                  pl.BlockSpec((d, cin), lambda b, i, g: (0, col0 // cin + g)),
                  pl.BlockSpec((1, tm, tm + POOL_PAD), lambda b, i, g: (g, 0, 0)),
                  pl.BlockSpec((1, cin, cout), lambda b, i, g: (g, 0, 0)),
                  pl.BlockSpec((1, cout), lambda b, i, g: (0, g))],
        out_specs=pl.BlockSpec((1, tm, cout), lambda b, i, g: (b, i, g)),
        out_shape=jax.ShapeDtypeStruct((bsz, seq, ng * cout), BF16),
        scratch_shapes=[pltpu.VMEM((tm + POOL_PAD, d), BF16)],
        compiler_params=_params(3),
        name="pool_branch",
    )(hx, hx, hm, w_in_b, band, pool_w_b, pool_scale)


def _merge_kernel(h_ref, a_ref, yb_ref, wga_ref, wgb_ref, wao_ref, o_ref):
    h = h_ref[...]
    ga = jax.nn.sigmoid(_dot(h, wga_ref[...]))
    gb = jax.nn.sigmoid(_dot(h, wgb_ref[...]))
    ya = _dot(a_ref[...], wao_ref[...])
    o_ref[...] = (ga * ya + gb * yb_ref[...].astype(F32)).astype(o_ref.dtype)


def _merge(hx2, a2, yb2, w_in_b, w_a_out_b, col0, tm, tj):
    n, d = hx2.shape
    dc = a2.shape[1]
    nj = d // tj
    return pl.pallas_call(
        _merge_kernel,
        grid=(n // tm, nj),
        in_specs=[pl.BlockSpec((tm, d), lambda i, j: (i, 0)),
                  pl.BlockSpec((tm, dc), lambda i, j: (i, 0)),
                  pl.BlockSpec((tm, tj), lambda i, j: (i, j)),
                  pl.BlockSpec((d, tj), lambda i, j: (0, col0 // tj + j)),
                  pl.BlockSpec((d, tj), lambda i, j: (0, col0 // tj + nj + j)),
                  pl.BlockSpec((dc, tj), lambda i, j: (0, j))],
        out_specs=pl.BlockSpec((tm, tj), lambda i, j: (i, j)),
        out_shape=jax.ShapeDtypeStruct((n, d), BF16),
        compiler_params=_params(2),
        name="merge",
    )(hx2, a2, yb2, w_in_b, w_in_b, w_a_out_b)


def _mix_kernel(m_ref, x_ref, lig_ref, lib_ref, wo_ref, g1_ref, b1_ref, rwh_ref, rwl_ref, rb_ref, tri_ref,
                h1_ref, eid_ref, wts_ref, rank_ref, cnt_ref, carry_ref):
    step = pl.program_id(0)

    @pl.when(step == 0)
    def _():
        carry_ref[...] = jnp.zeros(carry_ref.shape, F32)

    h0 = _layer_norm(x_ref[...], lig_ref[...], lib_ref[...])
    z = DEEPNORM_ALPHA * h0 + _dot(m_ref[...], wo_ref[...])
    h1 = _layer_norm(z, g1_ref[...], b1_ref[...])
    h1_ref[...] = h1

    h_hi = h1.astype(BF16)
    h_lo = (h1 - h_hi.astype(F32)).astype(BF16)
    nt = (((1,), (1,)), ((), ()))
    rwh = rwh_ref[...]
    logits = (lax.dot_general(rwh, h_hi, nt, preferred_element_type=F32)
              + lax.dot_general(rwh, h_lo, nt, preferred_element_type=F32)
              + lax.dot_general(rwl_ref[...], h_hi, nt, preferred_element_type=F32)
              + rb_ref[...])
    n_exp, tm = logits.shape
    e_iota = lax.broadcasted_iota(I32, (n_exp, tm), 0)
    work = logits
    vals, hots = [], []
    for k in range(TOP_K):
        m = jnp.max(work, axis=0, keepdims=True)
        idx = jnp.min(jnp.where(work == m, e_iota, n_exp), axis=0, keepdims=True)
        hot = e_iota == idx
        vals.append(m)
        hots.append(hot)
        eid_ref[k:k + 1, :] = idx
        work = jnp.where(hot, -jnp.inf, work)
    exps = [jnp.exp(v - vals[0]) for v in vals]
    denom = exps[0]
    for e in exps[1:]:
        denom = denom + e
    for k in range(TOP_K):
        wts_ref[k:k + 1, :] = exps[k] / denom

    multi = hots[0]
    for hot in hots[1:]:
        multi = multi | hot
    multi_f = jnp.where(multi, 1.0, 0.0).astype(F32)
    prefix = _dot(multi_f.astype(BF16), tri_ref[...]) + carry_ref[:, 0:1]
    for k in range(TOP_K):
        rank_ref[k:k + 1, :] = jnp.sum(jnp.where(hots[k], prefix, 0.0), axis=0, keepdims=True).astype(I32)
    carry = carry_ref[...] + jnp.sum(multi_f, axis=1, keepdims=True)
    carry_ref[...] = carry
    cnt_ref[...] = carry


def _mix_ln1(m2, x2, ln_in_g, ln_in_b, w_o_b, ln1_g, ln1_b, rw_hi, rw_lo, rb, tm):
    n, d = x2.shape
    n_exp = rw_hi.shape[0]
    tri = (jnp.arange(tm, dtype=I32)[:, None] < jnp.arange(tm, dtype=I32)[None, :]).astype(BF16)
    row = lambda i: (i, 0)
    fixed = lambda i: (0, 0)
    col = lambda i: (0, i)
    return pl.pallas_call(
        _mix_kernel,
        grid=(n // tm,),
        in_specs=[pl.BlockSpec((tm, d), row), pl.BlockSpec((tm, d), row),
                  pl.BlockSpec((1, d), fixed), pl.BlockSpec((1, d), fixed),
                  pl.BlockSpec((d, d), fixed),
                  pl.BlockSpec((1, d), fixed), pl.BlockSpec((1, d), fixed),
                  pl.BlockSpec((n_exp, d), fixed), pl.BlockSpec((n_exp, d), fixed),
                  pl.BlockSpec((n_exp, 1), fixed),
                  pl.BlockSpec((tm, tm), fixed)],
        out_specs=[pl.BlockSpec((tm, d), row),
                   pl.BlockSpec((TOP_K, tm), col), pl.BlockSpec((TOP_K, tm), col), pl.BlockSpec((TOP_K, tm), col),
                   pl.BlockSpec((n_exp, 128), fixed)],
        out_shape=[jax.ShapeDtypeStruct((n, d), F32),
                   jax.ShapeDtypeStruct((TOP_K, n), I32),
                   jax.ShapeDtypeStruct((TOP_K, n), F32),
                   jax.ShapeDtypeStruct((TOP_K, n), I32),
                   jax.ShapeDtypeStruct((n_exp, 128), F32)],
        scratch_shapes=[pltpu.VMEM((n_exp, 128), F32)],
        compiler_params=_params(1),
        name="mix_ln1",
    )(m2, x2, ln_in_g, ln_in_b, w_o_b, ln1_g, ln1_b, rw_hi, rw_lo, rb, tri)


def _row_copy(src_ref, src_row, dst_ref, dst_row, sem):
    return pltpu.make_async_copy(src_ref.at[pl.ds(src_row, 1)], dst_ref.at[pl.ds(dst_row, 1)], sem)


def _dispatch_kernel(pos_ref, h_ref, xs_in_ref, xs_ref, sem, *, tt):
    del xs_in_ref
    base = pl.program_id(0) * (TOP_K * tt)

    def issue(t, c):
        for k in range(TOP_K):
            _row_copy(h_ref, t, xs_ref, pos_ref[base + k * tt + t], sem).start()
        return c

    lax.fori_loop(0, tt, issue, 0)
    for k in range(TOP_K):
        pltpu.make_async_copy(h_ref, xs_ref.at[pl.ds(0, tt)], sem).wait()


def _dispatch(h1, pos_tiles, n_slots, tt):
    n, d = h1.shape
    zeros = jnp.zeros((n_slots, d), h1.dtype)
    grid_spec = pltpu.PrefetchScalarGridSpec(
        num_scalar_prefetch=1,
        grid=(n // tt,),
        in_specs=[pl.BlockSpec((tt, d), lambda i, pos: (i, 0)),
                  pl.BlockSpec(memory_space=pl.ANY)],
        out_specs=pl.BlockSpec(memory_space=pl.ANY),
        scratch_shapes=[pltpu.SemaphoreType.DMA(())],
    )
    return pl.pallas_call(
        functools.partial(_dispatch_kernel, tt=tt),
        grid_spec=grid_spec,
        out_shape=jax.ShapeDtypeStruct((n_slots, d), h1.dtype),
        input_output_aliases={2: 0},
        compiler_params=_params(1),
        name="dispatch",
    )(pos_tiles, h1, zeros)


def _expert_kernel(te_ref, nu_ref, xs_ref, wg_ref, wl_ref, wd_ref, bg_ref, bl_ref, bd_ref, o_ref, xb_ref):
    r = pl.program_id(0)
    j = pl.program_id(1)

    @pl.when(r < nu_ref[0])
    def _():
        @pl.when(j == 0)
        def _():
            xb_ref[...] = xs_ref[...].astype(BF16)

        x = xb_ref[...]
        gate = _dot(x, wg_ref[0].astype(BF16)) + bg_ref[0]
        lin = _dot(x, wl_ref[0].astype(BF16)) + bl_ref[0]
        gate = jnp.minimum(gate, SWIGLU_LIMIT)
        lin = jnp.clip(lin, -SWIGLU_LIMIT, SWIGLU_LIMIT)
        act = (lin + 1.0) * gate * jax.nn.sigmoid(SWIGLU_ALPHA * gate)
        contrib = _dot(act.astype(BF16), wd_ref[0].astype(BF16))

        @pl.when(j == 0)
        def _():
            o_ref[...] = contrib + bd_ref[0]

        @pl.when(j > 0)
        def _():
            o_ref[...] += contrib

    @pl.when((r >= nu_ref[0]) & (j == 0))
    def _():
        o_ref[...] = jnp.zeros(o_ref.shape, o_ref.dtype)


def _experts(xs, tile_expert, n_used, w_gate_up, b_gate_up, w_down, b_down, tm, tn):
    n_slots, d = xs.shape
    n_exp, _, de2 = w_gate_up.shape
    de = de2 // 2
    nj = de // tn
    n_tiles = n_slots // tm

    def rr(r, nu):
        return jnp.minimum(r, nu[0] - 1)

    def jj(r, j, nu):
        return jnp.where(r < nu[0], j, nj - 1)

    grid_spec = pltpu.PrefetchScalarGridSpec(
        num_scalar_prefetch=2,
        grid=(n_tiles, nj),
        in_specs=[pl.BlockSpec((tm, d), lambda r, j, te, nu: (rr(r, nu), 0)),
                  pl.BlockSpec((1, d, tn), lambda r, j, te, nu: (te[rr(r, nu)], 0, jj(r, j, nu))),
                  pl.BlockSpec((1, d, tn), lambda r, j, te, nu: (te[rr(r, nu)], 0, nj + jj(r, j, nu))),
                  pl.BlockSpec((1, tn, d), lambda r, j, te, nu: (te[rr(r, nu)], jj(r, j, nu), 0)),
                  pl.BlockSpec((1, 1, tn), lambda r, j, te, nu: (te[rr(r, nu)], 0, jj(r, j, nu))),
                  pl.BlockSpec((1, 1, tn), lambda r, j, te, nu: (te[rr(r, nu)], 0, nj + jj(r, j, nu))),
                  pl.BlockSpec((1, 1, d), lambda r, j, te, nu: (te[rr(r, nu)], 0, 0))],
        out_specs=pl.BlockSpec((tm, d), lambda r, j, te, nu: (r, 0)),
        scratch_shapes=[pltpu.VMEM((tm, d), BF16)],
    )
    return pl.pallas_call(
        _expert_kernel,
        grid_spec=grid_spec,
        out_shape=jax.ShapeDtypeStruct((n_slots, d), F32),
        compiler_params=_params(2),
        name="experts",
    )(tile_expert, n_used, xs, w_gate_up, w_gate_up, w_down,
      b_gate_up.reshape(n_exp, 1, de2), b_gate_up.reshape(n_exp, 1, de2), b_down.reshape(n_exp, 1, d))


def _combine_kernel(pos_ref, h_ref, w_ref, g_ref, b_ref, ys_ref, o_ref, buf_ref, sem, *, tt):
    base = pl.program_id(0) * (TOP_K * tt)

    def issue(t, c):
        for k in range(TOP_K):
            _row_copy(ys_ref, pos_ref[base + k * tt + t], buf_ref.at[k], t, sem).start()
        return c

    lax.fori_loop(0, tt, issue, 0)
    for k in range(TOP_K):
        pltpu.make_async_copy(ys_ref.at[pl.ds(0, tt)], buf_ref.at[k], sem).wait()
    w = w_ref[...]
    y = w[:, 0:1] * buf_ref[0]
    for k in range(1, TOP_K):
        y = y + w[:, k:k + 1] * buf_ref[k]
    o_ref[...] = _layer_norm(DEEPNORM_ALPHA * h_ref[...] + y, g_ref[...], b_ref[...])


def _combine(h1, pos_tiles, wts_t, ys, ln2_g, ln2_b, tt):
    n, d = h1.shape
    grid_spec = pltpu.PrefetchScalarGridSpec(
        num_scalar_prefetch=1,
        grid=(n // tt,),
        in_specs=[pl.BlockSpec((tt, d), lambda i, pos: (i, 0)),
                  pl.BlockSpec((tt, TOP_K), lambda i, pos: (i, 0)),
                  pl.BlockSpec((1, d), lambda i, pos: (0, 0)),
                  pl.BlockSpec((1, d), lambda i, pos: (0, 0)),
                  pl.BlockSpec(memory_space=pl.ANY)],
        out_specs=pl.BlockSpec((tt, d), lambda i, pos: (i, 0)),
        scratch_shapes=[pltpu.VMEM((TOP_K, tt, d), F32), pltpu.SemaphoreType.DMA(())],
    )
    return pl.pallas_call(
        functools.partial(_combine_kernel, tt=tt),
        grid_spec=grid_spec,
        out_shape=jax.ShapeDtypeStruct((n, d), F32),
        compiler_params=_params(1),
        name="combine",
    )(pos_tiles, h1, wts_t, ln2_g, ln2_b, ys)


def _tiles(seq, n_tok):
    return dict(
        ln_rows=min(512, n_tok),
        conv_rows=min(512, seq), conv_cols=512,
        pool_rows=min(512, seq),
        merge_rows=min(512, n_tok), merge_cols=512,
        mix_rows=min(256, n_tok),
        route_rows=min(128, n_tok),
        expert_rows=min(512, n_tok), expert_cols=256,
    )


def kernel(x, meta_tokens, ln_in_g, ln_in_b, w_in, conv_w, w_a_out, pool_w, pool_scale, w_o, ln1_g, ln1_b,
           router_w, router_b, w_gate_up, b_gate_up, w_down, b_down, ln2_g, ln2_b):
    bsz, seq, d = x.shape
    assert w_in.shape[0] == DEPTH and meta_tokens.shape[0] == N_META == HALO
    d_conv = conv_w.shape[-1]
    n_groups, pool_cin, pool_cout = pool_w.shape[1:]
    d_pool = n_groups * pool_cin
    assert n_groups == len(POOL_WINDOWS) and n_groups * pool_cout == d
    n_exp = router_w.shape[-1]
    n_tok = bsz * seq
    t = _tiles(seq, n_tok)
    row = lambda v: v.reshape(1, -1).astype(F32)

    w_in_b = w_in[0].astype(BF16)
    x2 = x.reshape(n_tok, d)
    lig, lib = row(ln_in_g), row(ln_in_b)

    hx2 = _ln_cast(x2, lig, lib, t["ln_rows"])
    hm = _ln_cast(meta_tokens.astype(F32), lig, lib, N_META)
    hx = hx2.reshape(bsz, seq, d)

    a = _conv_branch(hx, hm, w_in_b, conv_w[0], d_conv, t["conv_rows"], min(t["conv_cols"], d_conv))
    yb = _pool_branch(hx, hm, w_in_b, pool_w[0].astype(BF16), row(pool_scale[0]), 3 * d_conv, t["pool_rows"])
    m2 = _merge(hx2, a.reshape(n_tok, d_conv), yb.reshape(n_tok, d), w_in_b, w_a_out[0].astype(BF16),
                3 * d_conv + d_pool, t["merge_rows"], min(t["merge_cols"], d))

    rw_t = router_w[0].T
    rw_hi = rw_t.astype(BF16)
    rw_lo = (rw_t - rw_hi.astype(F32)).astype(BF16)
    h1, eid, wts, rank, cnt = _mix_ln1(m2, x2, lig, lib, w_o[0].astype(BF16), row(ln1_g[0]), row(ln1_b[0]),
                                       rw_hi, rw_lo, router_b[0].reshape(n_exp, 1), t["mix_rows"])

    tm_e = t["expert_rows"]
    n_tiles = (n_tok * TOP_K) // tm_e + n_exp
    counts = cnt[:, 0].astype(I32)
    padded = ((counts + tm_e - 1) // tm_e) * tm_e
    ends = jnp.cumsum(padded)
    pos = (ends - padded)[eid] + rank
    n_used = (ends[-1:] // tm_e).astype(I32)
    tile_expert = jnp.minimum(
        jnp.searchsorted(ends, jnp.arange(n_tiles, dtype=I32) * tm_e, side="right"), n_exp - 1).astype(I32)
    tt = t["route_rows"]
    pos_tiles = pos.reshape(TOP_K, n_tok // tt, tt).transpose(1, 0, 2).reshape(-1)

    xs = _dispatch(h1, pos_tiles, n_tiles * tm_e, tt)
    ys = _experts(xs, tile_expert, n_used, w_gate_up[0], b_gate_up[0], w_down[0], b_down[0],
                  tm_e, t["expert_cols"])
    out = _combine(h1, pos_tiles, wts.T, ys, row(ln2_g[0]), row(ln2_b[0]), tt)
    return out.reshape(bsz, seq, d)
```

```python
import functools

import jax
import jax.numpy as jnp
from jax import lax
from jax.experimental import pallas as pl
from jax.experimental.pallas import tpu as pltpu

F32 = jnp.float32
BF16 = jnp.bfloat16
I32 = jnp.int32
U32 = jnp.uint32

N_META = 16
CONV_K = 3
POOL_WINDOWS = (2, 4, 8, 16)
TOP_K = 4
SWIGLU_LIMIT = 7.0
SWIGLU_ALPHA = 1.702
LN_EPS = 1e-5
DEPTH = 1
DEEPNORM_ALPHA = (2.0 * DEPTH) ** 0.25

HALO = 16
POOL_PAD = 128
SLOT_ROWS = 256
GROUP_SUBTILES = 8
V7X_VMEM_LIMIT = 56 * 1024 * 1024

_ARB = "arbitrary"


def _params(n_axes):
    return pltpu.CompilerParams(dimension_semantics=(_ARB,) * n_axes, vmem_limit_bytes=V7X_VMEM_LIMIT)


def _layer_norm(x, g, b):
    mu = jnp.mean(x, axis=-1, keepdims=True)
    xc = x - mu
    var = jnp.mean(xc * xc, axis=-1, keepdims=True)
    return xc * lax.rsqrt(var + LN_EPS) * g + b


def _dot(a, b):
    return jnp.dot(a, b, preferred_element_type=F32)


def _pack_pairs(lo_bf16, hi_bf16):
    lo = lax.bitcast_convert_type(lo_bf16.astype(F32), U32) >> 16
    hi = lax.bitcast_convert_type(hi_bf16.astype(F32), U32) & jnp.uint32(0xFFFF0000)
    return hi | lo


def _unpack_pairs(words):
    lo = lax.bitcast_convert_type(words << 16, F32).astype(BF16)
    hi = lax.bitcast_convert_type(words & jnp.uint32(0xFFFF0000), F32).astype(BF16)
    return lo, hi


def _ln_cast_kernel(x_ref, g_ref, b_ref, o_ref):
    o_ref[...] = _layer_norm(x_ref[...], g_ref[...], b_ref[...]).astype(o_ref.dtype)


def _ln_cast(x2d, g, b, tr):
    n, d = x2d.shape
    return pl.pallas_call(
        _ln_cast_kernel,
        grid=(n // tr,),
        in_specs=[pl.BlockSpec((tr, d), lambda i: (i, 0)),
                  pl.BlockSpec((1, d), lambda i: (0, 0)),
                  pl.BlockSpec((1, d), lambda i: (0, 0))],
        out_specs=pl.BlockSpec((tr, d), lambda i: (i, 0)),
        out_shape=jax.ShapeDtypeStruct((n, d), BF16),
        compiler_params=_params(1),
        name="ln_cast",
    )(x2d, g, b)


def _conv_kernel(h_ref, halo_ref, hm_ref, wb_ref, wc_ref, wu_ref, cw_ref, o_ref, he_ref, cu_ref, *, tm):
    i = pl.program_id(1)
    j = pl.program_id(2)

    @pl.when(j == 0)
    def _():
        he_ref[HALO:, :] = h_ref[0]

    @pl.when((j == 0) & (i == 0))
    def _():
        he_ref[0:HALO, :] = hm_ref[...]

    @pl.when((j == 0) & (i > 0))
    def _():
        he_ref[0:HALO, :] = halo_ref[0]

    he = he_ref[...]
    cu_ref[...] = _dot(he, wc_ref[...]) * _dot(he, wu_ref[...])
    bg = _dot(he_ref[HALO:, :], wb_ref[...])
    cw = cw_ref[...]
    y = cw[CONV_K - 1:CONV_K, :] * cu_ref[HALO:HALO + tm, :]
    for k in range(CONV_K - 1):
        s = HALO - (CONV_K - 1) + k
        y = y + cw[k:k + 1, :] * cu_ref[s:s + tm, :]
    o_ref[0] = (bg * y).astype(o_ref.dtype)


def _conv_branch(hx, hm, w_in_b, conv_w, d_conv, tm, tc):
    bsz, seq, d = hx.shape
    nc = d_conv // tc
    hb = tm // HALO
    return pl.pallas_call(
        functools.partial(_conv_kernel, tm=tm),
        grid=(bsz, seq // tm, nc),
        in_specs=[pl.BlockSpec((1, tm, d), lambda b, i, j: (b, i, 0)),
                  pl.BlockSpec((1, HALO, d), lambda b, i, j: (b, jnp.maximum(i * hb - 1, 0), 0)),
                  pl.BlockSpec((HALO, d), lambda b, i, j: (0, 0)),
                  pl.BlockSpec((d, tc), lambda b, i, j: (0, j)),
                  pl.BlockSpec((d, tc), lambda b, i, j: (0, nc + j)),
                  pl.BlockSpec((d, tc), lambda b, i, j: (0, 2 * nc + j)),
                  pl.BlockSpec((CONV_K, tc), lambda b, i, j: (0, j))],
        out_specs=pl.BlockSpec((1, tm, tc), lambda b, i, j: (b, i, j)),
        out_shape=jax.ShapeDtypeStruct((bsz, seq, d_conv), BF16),
        scratch_shapes=[pltpu.VMEM((tm + HALO, d), BF16), pltpu.VMEM((tm + HALO, tc), F32)],
        compiler_params=_params(3),
        name="conv_branch",
    )(hx, hx, hm, w_in_b, w_in_b, w_in_b, conv_w)


def _pool_kernel(h_ref, halo_ref, hm_ref, wv_ref, band_ref, pw_ref, ps_ref, o_ref, he_ref):
    i = pl.program_id(1)
    g = pl.program_id(2)
    lead = POOL_PAD - HALO

    @pl.when(g == 0)
    def _():
        he_ref[0:lead, :] = jnp.zeros((lead, he_ref.shape[1]), he_ref.dtype)
        he_ref[POOL_PAD:, :] = h_ref[0]

    @pl.when((g == 0) & (i == 0))
    def _():
        he_ref[lead:POOL_PAD, :] = hm_ref[...]

    @pl.when((g == 0) & (i > 0))
    def _():
        he_ref[lead:POOL_PAD, :] = halo_ref[0]

    v = _dot(he_ref[...], wv_ref[...])
    v_hi = v.astype(BF16)
    v_lo = (v - v_hi.astype(F32)).astype(BF16)
    band = band_ref[0]
    pooled = _dot(band, v_hi) + _dot(band, v_lo)
    o_ref[0] = (_dot(pooled.astype(BF16), pw_ref[0]) * ps_ref[...]).astype(o_ref.dtype)


def _pool_band(tm):
    t = jnp.arange(tm, dtype=I32)[:, None] + POOL_PAD
    s = jnp.arange(tm + POOL_PAD, dtype=I32)[None, :]
    bands = []
    for w in POOL_WINDOWS:
        inside = ((s <= t) & (s > t - w)).astype(F32) / w
        bands.append(inside - (s == t).astype(F32))
    return jnp.stack(bands).astype(BF16)


def _pool_branch(hx, hm, w_in_b, pool_w_b, pool_scale, col0, tm):
    bsz, seq, d = hx.shape
    ng, cin, cout = pool_w_b.shape
    hb = tm // HALO
    band = _pool_band(tm)
    return pl.pallas_call(
        _pool_kernel,
        grid=(bsz, seq // tm, ng),
        in_specs=[pl.BlockSpec((1, tm, d), lambda b, i, g: (b, i, 0)),
                  pl.BlockSpec((1, HALO, d), lambda b, i, g: (b, jnp.maximum(i * hb - 1, 0), 0)),
                  pl.BlockSpec((HALO, d), lambda b, i, g: (0, 0)),
                  pl.BlockSpec((d, cin), lambda b, i, g: (0, col0 // cin + g)),
                  pl.BlockSpec((1, tm, tm + POOL_PAD), lambda b, i, g: (g, 0, 0)),
                  pl.BlockSpec((1, cin, cout), lambda b, i, g: (g, 0, 0)),
                  pl.BlockSpec((1, cout), lambda b, i, g: (0, g))],
        out_specs=pl.BlockSpec((1, tm, cout), lambda b, i, g: (b, i, g)),
        out_shape=jax.ShapeDtypeStruct((bsz, seq, ng * cout), BF16),
        scratch_shapes=[pltpu.VMEM((tm + POOL_PAD, d), BF16)],
        compiler_params=_params(3),
        name="pool_branch",
    )(hx, hx, hm, w_in_b, band, pool_w_b, pool_scale)


def _merge_kernel(h_ref, a_ref, yb_ref, wga_ref, wgb_ref, wao_ref, o_ref):
    h = h_ref[...]
    ga = jax.nn.sigmoid(_dot(h, wga_ref[...]))
    gb = jax.nn.sigmoid(_dot(h, wgb_ref[...]))
    ya = _dot(a_ref[...], wao_ref[...])
    o_ref[...] = (ga * ya + gb * yb_ref[...].astype(F32)).astype(o_ref.dtype)


def _merge(hx2, a2, yb2, w_in_b, w_a_out_b, col0, tm, tj):
    n, d = hx2.shape
    dc = a2.shape[1]
    nj = d // tj
    return pl.pallas_call(
        _merge_kernel,
        grid=(n // tm, nj),
        in_specs=[pl.BlockSpec((tm, d), lambda i, j: (i, 0)),
                  pl.BlockSpec((tm, dc), lambda i, j: (i, 0)),
                  pl.BlockSpec((tm, tj), lambda i, j: (i, j)),
                  pl.BlockSpec((d, tj), lambda i, j: (0, col0 // tj + j)),
                  pl.BlockSpec((d, tj), lambda i, j: (0, col0 // tj + nj + j)),
                  pl.BlockSpec((dc, tj), lambda i, j: (0, j))],
        out_specs=pl.BlockSpec((tm, tj), lambda i, j: (i, j)),
        out_shape=jax.ShapeDtypeStruct((n, d), BF16),
        compiler_params=_params(2),
        name="merge",
    )(hx2, a2, yb2, w_in_b, w_in_b, w_a_out_b)


def _mix_kernel(m_ref, x_ref, lig_ref, lib_ref, wo_ref, g1_ref, b1_ref, rwh_ref, rwl_ref, rb_ref, tri_ref,
                h1_ref, h1p_ref, eid_ref, wts_ref, rank_ref, cnt_ref, carry_ref):
    step = pl.program_id(0)

    @pl.when(step == 0)
    def _():
        carry_ref[...] = jnp.zeros(carry_ref.shape, F32)

    h0 = _layer_norm(x_ref[...], lig_ref[...], lib_ref[...])
    z = DEEPNORM_ALPHA * h0 + _dot(m_ref[...], wo_ref[...])
    h1 = _layer_norm(z, g1_ref[...], b1_ref[...])
    h1_ref[...] = h1
    h_hi = h1.astype(BF16)
    half = h1.shape[1] // 2
    h1p_ref[...] = _pack_pairs(h_hi[:, :half], h_hi[:, half:])

    h_lo = (h1 - h_hi.astype(F32)).astype(BF16)
    nt = (((1,), (1,)), ((), ()))
    rwh = rwh_ref[...]
    logits = (lax.dot_general(rwh, h_hi, nt, preferred_element_type=F32)
              + lax.dot_general(rwh, h_lo, nt, preferred_element_type=F32)
              + lax.dot_general(rwl_ref[...], h_hi, nt, preferred_element_type=F32)
              + rb_ref[...])
    n_exp, tm = logits.shape
    e_iota = lax.broadcasted_iota(I32, (n_exp, tm), 0)
    work = logits
    vals, hots = [], []
    for k in range(TOP_K):
        m = jnp.max(work, axis=0, keepdims=True)
        idx = jnp.min(jnp.where(work == m, e_iota, n_exp), axis=0, keepdims=True)
        hot = e_iota == idx
        vals.append(m)
        hots.append(hot)
        eid_ref[k:k + 1, :] = idx
        work = jnp.where(hot, -jnp.inf, work)
    exps = [jnp.exp(v - vals[0]) for v in vals]
    denom = exps[0]
    for e in exps[1:]:
        denom = denom + e
    for k in range(TOP_K):
        wts_ref[k:k + 1, :] = exps[k] / denom

    multi = hots[0]
    for hot in hots[1:]:
        multi = multi | hot
    multi_f = jnp.where(multi, 1.0, 0.0).astype(F32)
    prefix = _dot(multi_f.astype(BF16), tri_ref[...]) + carry_ref[:, 0:1]
    for k in range(TOP_K):
        rank_ref[k:k + 1, :] = jnp.sum(jnp.where(hots[k], prefix, 0.0), axis=0, keepdims=True).astype(I32)
    carry = carry_ref[...] + jnp.sum(multi_f, axis=1, keepdims=True)
    carry_ref[...] = carry
    cnt_ref[...] = carry


def _mix_ln1(m2, x2, ln_in_g, ln_in_b, w_o_b, ln1_g, ln1_b, rw_hi, rw_lo, rb, tm):
    n, d = x2.shape
    n_exp = rw_hi.shape[0]
    tri = (jnp.arange(tm, dtype=I32)[:, None] < jnp.arange(tm, dtype=I32)[None, :]).astype(BF16)
    row = lambda i: (i, 0)
    fixed = lambda i: (0, 0)
    col = lambda i: (0, i)
    return pl.pallas_call(
        _mix_kernel,
        grid=(n // tm,),
        in_specs=[pl.BlockSpec((tm, d), row), pl.BlockSpec((tm, d), row),
                  pl.BlockSpec((1, d), fixed), pl.BlockSpec((1, d), fixed),
                  pl.BlockSpec((d, d), fixed),
                  pl.BlockSpec((1, d), fixed), pl.BlockSpec((1, d), fixed),
                  pl.BlockSpec((n_exp, d), fixed), pl.BlockSpec((n_exp, d), fixed),
                  pl.BlockSpec((n_exp, 1), fixed),
                  pl.BlockSpec((tm, tm), fixed)],
        out_specs=[pl.BlockSpec((tm, d), row), pl.BlockSpec((tm, d // 2), row),
                   pl.BlockSpec((TOP_K, tm), col), pl.BlockSpec((TOP_K, tm), col), pl.BlockSpec((TOP_K, tm), col),
                   pl.BlockSpec((n_exp, 128), fixed)],
        out_shape=[jax.ShapeDtypeStruct((n, d), F32),
                   jax.ShapeDtypeStruct((n, d // 2), U32),
                   jax.ShapeDtypeStruct((TOP_K, n), I32),
                   jax.ShapeDtypeStruct((TOP_K, n), F32),
                   jax.ShapeDtypeStruct((TOP_K, n), I32),
                   jax.ShapeDtypeStruct((n_exp, 128), F32)],
        scratch_shapes=[pltpu.VMEM((n_exp, 128), F32)],
        compiler_params=_params(1),
        name="mix_ln1",
    )(m2, x2, ln_in_g, ln_in_b, w_o_b, ln1_g, ln1_b, rw_hi, rw_lo, rb, tri)


def _row_copy(src_ref, src_row, dst_ref, dst_row, sem):
    return pltpu.make_async_copy(src_ref.at[pl.ds(src_row, 1)], dst_ref.at[pl.ds(dst_row, 1)], sem)


def _dispatch_kernel(pos_ref, h_ref, xs_in_ref, xs_ref, sem, *, tt):
    del xs_in_ref
    base = pl.program_id(0) * (TOP_K * tt)

    def issue(t, c):
        for k in range(TOP_K):
            _row_copy(h_ref, t, xs_ref, pos_ref[base + k * tt + t], sem).start()
        return c

    lax.fori_loop(0, tt, issue, 0)
    for k in range(TOP_K):
        pltpu.make_async_copy(h_ref, xs_ref.at[pl.ds(0, tt)], sem).wait()


def _dispatch(h1p, pos_tiles, n_slots, tt):
    n, dw = h1p.shape
    zeros = jnp.zeros((n_slots, dw), h1p.dtype)
    grid_spec = pltpu.PrefetchScalarGridSpec(
        num_scalar_prefetch=1,
        grid=(n // tt,),
        in_specs=[pl.BlockSpec((tt, dw), lambda i, pos: (i, 0)),
                  pl.BlockSpec(memory_space=pl.ANY)],
        out_specs=pl.BlockSpec(memory_space=pl.ANY),
        scratch_shapes=[pltpu.SemaphoreType.DMA(())],
    )
    return pl.pallas_call(
        functools.partial(_dispatch_kernel, tt=tt),
        grid_spec=grid_spec,
        out_shape=jax.ShapeDtypeStruct((n_slots, dw), h1p.dtype),
        input_output_aliases={2: 0},
        compiler_params=_params(1),
        name="dispatch",
    )(pos_tiles, h1p, zeros)


def _expert_kernel(ge_ref, gs_ref, gn_ref, nu_ref, xs_ref, wg_ref, wl_ref, wd_ref, bg_ref, bl_ref, bd_ref,
                   ys_ref, xb_ref, acc_ref, stage_ref, sem_in, sem_out, *, nj):
    s = pl.program_id(0)
    j = pl.program_id(1)
    half = xb_ref.shape[1] // 2

    def rows(q):
        return pl.ds(pl.multiple_of(q * SLOT_ROWS, SLOT_ROWS), SLOT_ROWS)

    def out_copy(first_sub, q):
        return pltpu.make_async_copy(acc_ref.at[rows(q)], ys_ref.at[rows(first_sub + q)], sem_out.at[q])

    def in_copy(first_sub, q, slot):
        return pltpu.make_async_copy(xs_ref.at[rows(first_sub + q)], stage_ref.at[slot], sem_in.at[slot])

    @pl.when(s < nu_ref[0])
    def _():
        first_sub = gs_ref[s]
        nsub = gn_ref[s]

        @pl.when(j == 0)
        def _():
            @pl.when(s > 0)
            def _():
                prev_first = gs_ref[s - 1]

                def wait_out(q, c):
                    out_copy(prev_first, q).wait()
                    return c

                lax.fori_loop(0, gn_ref[s - 1], wait_out, 0)

            in_copy(first_sub, 0, 0).start()

            def load(q, c):
                slot = q % 2
                in_copy(first_sub, q, slot).wait()

                @pl.when(q + 1 < nsub)
                def _():
                    in_copy(first_sub, q + 1, 1 - slot).start()

                lo, hi = _unpack_pairs(stage_ref[slot])
                xb_ref[rows(q), 0:half] = lo
                xb_ref[rows(q), half:] = hi
                return c

            lax.fori_loop(0, nsub, load, 0)

        def make_body(first, last):
            def body(q, c):
                x = xb_ref[rows(q), :]
                gate = _dot(x, wg_ref[0].astype(BF16)) + bg_ref[0]
                lin = _dot(x, wl_ref[0].astype(BF16)) + bl_ref[0]
                gate = jnp.minimum(gate, SWIGLU_LIMIT)
                lin = jnp.clip(lin, -SWIGLU_LIMIT, SWIGLU_LIMIT)
                act = (lin + 1.0) * gate * jax.nn.sigmoid(SWIGLU_ALPHA * gate)
                contrib = _dot(act.astype(BF16), wd_ref[0].astype(BF16))
                if first:
                    acc_ref[rows(q), :] = contrib + bd_ref[0]
                else:
                    acc_ref[rows(q), :] += contrib
                if last:
                    out_copy(first_sub, q).start()
                return c
            return body

        def run(first, last):
            body = make_body(first, last)

            def pair(p, c):
                body(2 * p, c)
                return body(2 * p + 1, c)

            lax.fori_loop(0, nsub // 2, pair, 0)

            @pl.when(nsub % 2 == 1)
            def _():
                body(nsub - 1, 0)

        @pl.when(j == 0)
        def _():
            run(True, nj == 1)

        if nj > 2:
            @pl.when((j > 0) & (j < nj - 1))
            def _():
                run(False, False)

        if nj > 1:
            @pl.when(j == nj - 1)
            def _():
                run(False, True)

        @pl.when((s == nu_ref[0] - 1) & (j == nj - 1))
        def _():
            def wait_out(q, c):
                out_copy(first_sub, q).wait()
                return c

            lax.fori_loop(0, nsub, wait_out, 0)

    @pl.when((s >= nu_ref[0]) & (j == 0))
    def _():
        @pl.when(s == nu_ref[0])
        def _():
            def zero(q, c):
                acc_ref[rows(q), :] = jnp.zeros((SLOT_ROWS, acc_ref.shape[1]), F32)
                return c

            lax.fori_loop(0, GROUP_SUBTILES, zero, 0)

        tail_first = nu_ref[1] + (s - nu_ref[0]) * GROUP_SUBTILES
        n_tail = jnp.clip(ys_ref.shape[0] // SLOT_ROWS - tail_first, 0, GROUP_SUBTILES)

        def start_zero(q, c):
            out_copy(tail_first, q).start()
            return c

        def wait_zero(q, c):
            out_copy(tail_first, q).wait()
            return c

        lax.fori_loop(0, n_tail, start_zero, 0)
        lax.fori_loop(0, n_tail, wait_zero, 0)


def _experts(xs, group_expert, group_first, group_nsub, n_groups, w_gate_up, b_gate_up, w_down, b_down, tn):
    n_slots, dw = xs.shape
    d = 2 * dw
    n_exp, _, de2 = w_gate_up.shape
    de = de2 // 2
    nj = de // tn
    max_groups = group_expert.shape[0]
    rmax = GROUP_SUBTILES * SLOT_ROWS

    def gi(s, nu):
        return jnp.minimum(s, nu[0] - 1)

    def jj(s, j, nu):
        return jnp.where(s < nu[0], j, nj - 1)

    grid_spec = pltpu.PrefetchScalarGridSpec(
        num_scalar_prefetch=4,
        grid=(max_groups, nj),
        in_specs=[pl.BlockSpec(memory_space=pl.ANY),
                  pl.BlockSpec((1, d, tn), lambda s, j, ge, gs, gn, nu: (ge[gi(s, nu)], 0, jj(s, j, nu))),
                  pl.BlockSpec((1, d, tn), lambda s, j, ge, gs, gn, nu: (ge[gi(s, nu)], 0, nj + jj(s, j, nu))),
                  pl.BlockSpec((1, tn, d), lambda s, j, ge, gs, gn, nu: (ge[gi(s, nu)], jj(s, j, nu), 0)),
                  pl.BlockSpec((1, 1, tn), lambda s, j, ge, gs, gn, nu: (ge[gi(s, nu)], 0, jj(s, j, nu))),
                  pl.BlockSpec((1, 1, tn), lambda s, j, ge, gs, gn, nu: (ge[gi(s, nu)], 0, nj + jj(s, j, nu))),
                  pl.BlockSpec((1, 1, d), lambda s, j, ge, gs, gn, nu: (ge[gi(s, nu)], 0, 0))],
        out_specs=pl.BlockSpec(memory_space=pl.ANY),
        scratch_shapes=[pltpu.VMEM((rmax, d), BF16),
                        pltpu.VMEM((rmax, d), F32),
                        pltpu.VMEM((2, SLOT_ROWS, dw), U32),
                        pltpu.SemaphoreType.DMA((2,)),
                        pltpu.SemaphoreType.DMA((GROUP_SUBTILES,))],
    )
    return pl.pallas_call(
        functools.partial(_expert_kernel, nj=nj),
        grid_spec=grid_spec,
        out_shape=jax.ShapeDtypeStruct((n_slots, d), F32),
        compiler_params=_params(2),
        name="experts",
    )(group_expert, group_first, group_nsub, n_groups, xs, w_gate_up, w_gate_up, w_down,
      b_gate_up.reshape(n_exp, 1, de2), b_gate_up.reshape(n_exp, 1, de2), b_down.reshape(n_exp, 1, d))


def _combine_kernel(pos_ref, h_ref, w_ref, g_ref, b_ref, ys_ref, o_ref, buf_ref, sem, *, tt):
    base = pl.program_id(0) * (TOP_K * tt)

    def issue(t, c):
        for k in range(TOP_K):
            _row_copy(ys_ref, pos_ref[base + k * tt + t], buf_ref.at[k], t, sem).start()
        return c

    lax.fori_loop(0, tt, issue, 0)
    for k in range(TOP_K):
        pltpu.make_async_copy(ys_ref.at[pl.ds(0, tt)], buf_ref.at[k], sem).wait()
    w = w_ref[...]
    y = w[:, 0:1] * buf_ref[0]
    for k in range(1, TOP_K):
        y = y + w[:, k:k + 1] * buf_ref[k]
    o_ref[...] = _layer_norm(DEEPNORM_ALPHA * h_ref[...] + y, g_ref[...], b_ref[...])


def _combine(h1, pos_tiles, wts_t, ys, ln2_g, ln2_b, tt):
    n, d = h1.shape
    grid_spec = pltpu.PrefetchScalarGridSpec(
        num_scalar_prefetch=1,
        grid=(n // tt,),
        in_specs=[pl.BlockSpec((tt, d), lambda i, pos: (i, 0)),
                  pl.BlockSpec((tt, TOP_K), lambda i, pos: (i, 0)),
                  pl.BlockSpec((1, d), lambda i, pos: (0, 0)),
                  pl.BlockSpec((1, d), lambda i, pos: (0, 0)),
                  pl.BlockSpec(memory_space=pl.ANY)],
        out_specs=pl.BlockSpec((tt, d), lambda i, pos: (i, 0)),
        scratch_shapes=[pltpu.VMEM((TOP_K, tt, d), F32), pltpu.SemaphoreType.DMA(())],
    )
    return pl.pallas_call(
        functools.partial(_combine_kernel, tt=tt),
        grid_spec=grid_spec,
        out_shape=jax.ShapeDtypeStruct((n, d), F32),
        compiler_params=_params(1),
        name="combine",
    )(pos_tiles, h1, wts_t, ln2_g, ln2_b, ys)


def _routing_tables(counts, eid, rank, n_tok):
    n_exp = counts.shape[0]
    e_ids = jnp.arange(n_exp, dtype=I32)
    nsub = (counts + SLOT_ROWS - 1) // SLOT_ROWS
    ngrp = (nsub + GROUP_SUBTILES - 1) // GROUP_SUBTILES
    base = nsub // jnp.maximum(ngrp, 1)
    rem = nsub - base * ngrp
    first_sub = jnp.cumsum(nsub) - nsub
    pos = jnp.sum(jnp.where(eid[..., None] == e_ids, first_sub * SLOT_ROWS, 0), axis=-1) + rank

    n_subtiles = (n_tok * TOP_K) // SLOT_ROWS + n_exp
    max_groups = (n_subtiles + GROUP_SUBTILES - 1) // GROUP_SUBTILES + n_exp
    grp_end = jnp.cumsum(ngrp)
    g_ids = jnp.arange(max_groups, dtype=I32)
    g_exp = jnp.minimum(jnp.sum((grp_end[None, :] <= g_ids[:, None]).astype(I32), axis=1), n_exp - 1)
    pick = lambda tab: jnp.sum(jnp.where(g_exp[:, None] == e_ids[None, :], tab[None, :], 0), axis=1)
    local = g_ids - pick(grp_end - ngrp)
    g_base, g_rem = pick(base), pick(rem)
    g_nsub = g_base + (local < g_rem).astype(I32)
    g_first = pick(first_sub) + local * g_base + jnp.minimum(local, g_rem)
    used = jnp.stack([grp_end[-1], jnp.sum(nsub)]).astype(I32)
    return pos, n_subtiles * SLOT_ROWS, g_exp.astype(I32), g_first.astype(I32), g_nsub.astype(I32), used


def _tiles(seq, n_tok):
    return dict(
        ln_rows=min(512, n_tok),
        conv_rows=min(512, seq), conv_cols=512,
        pool_rows=min(512, seq),
        merge_rows=min(512, n_tok), merge_cols=512,
        mix_rows=min(256, n_tok),
        route_rows=min(128, n_tok),
        expert_cols=256,
    )


def kernel(x, meta_tokens, ln_in_g, ln_in_b, w_in, conv_w, w_a_out, pool_w, pool_scale, w_o, ln1_g, ln1_b,
           router_w, router_b, w_gate_up, b_gate_up, w_down, b_down, ln2_g, ln2_b):
    bsz, seq, d = x.shape
    assert w_in.shape[0] == DEPTH and meta_tokens.shape[0] == N_META == HALO
    d_conv = conv_w.shape[-1]
    n_groups, pool_cin, pool_cout = pool_w.shape[1:]
    d_pool = n_groups * pool_cin
    assert n_groups == len(POOL_WINDOWS) and n_groups * pool_cout == d
    n_exp = router_w.shape[-1]
    n_tok = bsz * seq
    t = _tiles(seq, n_tok)
    row = lambda v: v.reshape(1, -1).astype(F32)

    w_in_b = w_in[0].astype(BF16)
    x2 = x.reshape(n_tok, d)
    lig, lib = row(ln_in_g), row(ln_in_b)

    hx2 = _ln_cast(x2, lig, lib, t["ln_rows"])
    hm = _ln_cast(meta_tokens.astype(F32), lig, lib, N_META)
    hx = hx2.reshape(bsz, seq, d)

    a = _conv_branch(hx, hm, w_in_b, conv_w[0], d_conv, t["conv_rows"], min(t["conv_cols"], d_conv))
    yb = _pool_branch(hx, hm, w_in_b, pool_w[0].astype(BF16), row(pool_scale[0]), 3 * d_conv, t["pool_rows"])
    m2 = _merge(hx2, a.reshape(n_tok, d_conv), yb.reshape(n_tok, d), w_in_b, w_a_out[0].astype(BF16),
                3 * d_conv + d_pool, t["merge_rows"], min(t["merge_cols"], d))

    rw_t = router_w[0].T
    rw_hi = rw_t.astype(BF16)
    rw_lo = (rw_t - rw_hi.astype(F32)).astype(BF16)
    h1, h1p, eid, wts, rank, cnt = _mix_ln1(m2, x2, lig, lib, w_o[0].astype(BF16), row(ln1_g[0]), row(ln1_b[0]),
                                            rw_hi, rw_lo, router_b[0].reshape(n_exp, 1), t["mix_rows"])

    pos, n_slots, g_exp, g_first, g_nsub, n_grp = _routing_tables(cnt[:, 0].astype(I32), eid, rank, n_tok)
    tt = t["route_rows"]
    pos_tiles = pos.reshape(TOP_K, n_tok // tt, tt).transpose(1, 0, 2).reshape(-1)

    xs = _dispatch(h1p, pos_tiles, n_slots, tt)
    ys = _experts(xs, g_exp, g_first, g_nsub, n_grp, w_gate_up[0], b_gate_up[0], w_down[0], b_down[0],
                  min(t["expert_cols"], w_down.shape[-2]))
    out = _combine(h1, pos_tiles, wts.T, ys, row(ln2_g[0]), row(ln2_b[0]), tt)
    return out.reshape(bsz, seq, d)
```

```python
import functools

import jax
import jax.numpy as jnp
from jax import lax
from jax.experimental import pallas as pl
from jax.experimental.pallas import tpu as pltpu

F32 = jnp.float32
BF16 = jnp.bfloat16
I32 = jnp.int32
U32 = jnp.uint32

N_META = 16
CONV_K = 3
POOL_WINDOWS = (2, 4, 8, 16)
TOP_K = 4
SWIGLU_LIMIT = 7.0
SWIGLU_ALPHA = 1.702
LN_EPS = 1e-5
DEPTH = 1
DEEPNORM_ALPHA = (2.0 * DEPTH) ** 0.25

HALO = 16
POOL_PAD = 128
POOL_CHUNK = 256
SLOT_ROWS = 256
GROUP_SUBTILES = 9
MIX_HALF = 256
COMBINE_CHUNK = 32
V7X_VMEM_LIMIT = 58 * 1024 * 1024

_ARB = "arbitrary"


def _params(n_axes):
    return pltpu.CompilerParams(dimension_semantics=(_ARB,) * n_axes, vmem_limit_bytes=V7X_VMEM_LIMIT)


def _layer_norm(x, g, b):
    mu = jnp.mean(x, axis=-1, keepdims=True)
    xc = x - mu
    var = jnp.mean(xc * xc, axis=-1, keepdims=True)
    return xc * lax.rsqrt(var + LN_EPS) * g + b


def _dot(a, b):
    return jnp.dot(a, b, preferred_element_type=F32)


def _pack_pairs(lo_bf16, hi_bf16):
    lo = lax.bitcast_convert_type(lo_bf16.astype(F32), U32) >> 16
    hi = lax.bitcast_convert_type(hi_bf16.astype(F32), U32) & jnp.uint32(0xFFFF0000)
    return hi | lo


def _unpack_pairs(words):
    lo = lax.bitcast_convert_type(words << 16, F32).astype(BF16)
    hi = lax.bitcast_convert_type(words & jnp.uint32(0xFFFF0000), F32).astype(BF16)
    return lo, hi


def _ln_cast_kernel(x_ref, g_ref, b_ref, o_ref):
    o_ref[...] = _layer_norm(x_ref[...], g_ref[...], b_ref[...]).astype(o_ref.dtype)


def _ln_cast(x2d, g, b, tr):
    n, d = x2d.shape
    return pl.pallas_call(
        _ln_cast_kernel,
        grid=(n // tr,),
        in_specs=[pl.BlockSpec((tr, d), lambda i: (i, 0)),
                  pl.BlockSpec((1, d), lambda i: (0, 0)),
                  pl.BlockSpec((1, d), lambda i: (0, 0))],
        out_specs=pl.BlockSpec((tr, d), lambda i: (i, 0)),
        out_shape=jax.ShapeDtypeStruct((n, d), BF16),
        compiler_params=_params(1),
        name="ln_cast",
    )(x2d, g, b)


def _conv_kernel(h_ref, halo_ref, hm_ref, wb_ref, wc_ref, wu_ref, cw_ref, o_ref, he_ref, cu_ref, *, tm):
    i = pl.program_id(1)
    j = pl.program_id(2)

    @pl.when(j == 0)
    def _():
        he_ref[HALO:, :] = h_ref[0]

    @pl.when((j == 0) & (i == 0))
    def _():
        he_ref[0:HALO, :] = hm_ref[...]

    @pl.when((j == 0) & (i > 0))
    def _():
        he_ref[0:HALO, :] = halo_ref[0]

    he = he_ref[...]
    cu_ref[...] = _dot(he, wc_ref[...]) * _dot(he, wu_ref[...])
    bg = _dot(he_ref[HALO:, :], wb_ref[...])
    cw = cw_ref[...]
    y = cw[CONV_K - 1:CONV_K, :] * cu_ref[HALO:HALO + tm, :]
    for k in range(CONV_K - 1):
        s = HALO - (CONV_K - 1) + k
        y = y + cw[k:k + 1, :] * cu_ref[s:s + tm, :]
    o_ref[0] = (bg * y).astype(o_ref.dtype)


def _conv_branch(hx, hm, w_in_b, conv_w, d_conv, tm, tc):
    bsz, seq, d = hx.shape
    nc = d_conv // tc
    hb = tm // HALO
    return pl.pallas_call(
        functools.partial(_conv_kernel, tm=tm),
        grid=(bsz, seq // tm, nc),
        in_specs=[pl.BlockSpec((1, tm, d), lambda b, i, j: (b, i, 0)),
                  pl.BlockSpec((1, HALO, d), lambda b, i, j: (b, jnp.maximum(i * hb - 1, 0), 0)),
                  pl.BlockSpec((HALO, d), lambda b, i, j: (0, 0)),
                  pl.BlockSpec((d, tc), lambda b, i, j: (0, j)),
                  pl.BlockSpec((d, tc), lambda b, i, j: (0, nc + j)),
                  pl.BlockSpec((d, tc), lambda b, i, j: (0, 2 * nc + j)),
                  pl.BlockSpec((CONV_K, tc), lambda b, i, j: (0, j))],
        out_specs=pl.BlockSpec((1, tm, tc), lambda b, i, j: (b, i, j)),
        out_shape=jax.ShapeDtypeStruct((bsz, seq, d_conv), BF16),
        scratch_shapes=[pltpu.VMEM((tm + HALO, d), BF16), pltpu.VMEM((tm + HALO, tc), F32)],
        compiler_params=_params(3),
        name="conv_branch",
    )(hx, hx, hm, w_in_b, w_in_b, w_in_b, conv_w)


def _pool_kernel(h_ref, halo_ref, hm_ref, wv_ref, band_ref, pw_ref, ps_ref, o_ref, he_ref, vh_ref, vl_ref, *, tm):
    i = pl.program_id(1)
    lead = POOL_PAD - HALO
    he_ref[0:lead, :] = jnp.zeros((lead, he_ref.shape[1]), he_ref.dtype)
    he_ref[POOL_PAD:, :] = h_ref[0]

    @pl.when(i == 0)
    def _():
        he_ref[lead:POOL_PAD, :] = hm_ref[...]

    @pl.when(i > 0)
    def _():
        he_ref[lead:POOL_PAD, :] = halo_ref[0]

    v = _dot(he_ref[...], wv_ref[...])
    v_hi = v.astype(BF16)
    vh_ref[...] = v_hi
    vl_ref[...] = (v - v_hi.astype(F32)).astype(BF16)
    ng, cin, cout = pw_ref.shape
    rows = min(POOL_CHUNK, tm)
    for c in range(tm // rows):
        r0 = c * rows
        for g in range(ng):
            band = band_ref[g]
            win = (slice(r0, r0 + rows + POOL_PAD), slice(g * cin, (g + 1) * cin))
            pooled = _dot(band, vh_ref[win]) + _dot(band, vl_ref[win])
            y = _dot(pooled.astype(BF16), pw_ref[g]) * ps_ref[:, g * cout:(g + 1) * cout]
            o_ref[0, r0:r0 + rows, g * cout:(g + 1) * cout] = y.astype(o_ref.dtype)


def _pool_band(rows):
    t = jnp.arange(rows, dtype=I32)[:, None] + POOL_PAD
    s = jnp.arange(rows + POOL_PAD, dtype=I32)[None, :]
    bands = []
    for w in POOL_WINDOWS:
        inside = ((s <= t) & (s > t - w)).astype(F32) / w
        bands.append(inside - (s == t).astype(F32))
    return jnp.stack(bands).astype(BF16)


def _pool_branch(hx, hm, w_in_b, pool_w_b, pool_scale, col0, tm):
    bsz, seq, d = hx.shape
    ng, cin, cout = pool_w_b.shape
    d_pool = ng * cin
    hb = tm // HALO
    band = _pool_band(min(POOL_CHUNK, tm))
    return pl.pallas_call(
        functools.partial(_pool_kernel, tm=tm),
        grid=(bsz, seq // tm),
        in_specs=[pl.BlockSpec((1, tm, d), lambda b, i: (b, i, 0)),
                  pl.BlockSpec((1, HALO, d), lambda b, i: (b, jnp.maximum(i * hb - 1, 0), 0)),
                  pl.BlockSpec((HALO, d), lambda b, i: (0, 0)),
                  pl.BlockSpec((d, d_pool), lambda b, i: (0, col0 // d_pool)),
                  pl.BlockSpec(band.shape, lambda b, i: (0, 0, 0)),
                  pl.BlockSpec((ng, cin, cout), lambda b, i: (0, 0, 0)),
                  pl.BlockSpec((1, ng * cout), lambda b, i: (0, 0))],
        out_specs=pl.BlockSpec((1, tm, ng * cout), lambda b, i: (b, i, 0)),
        out_shape=jax.ShapeDtypeStruct((bsz, seq, ng * cout), BF16),
        scratch_shapes=[pltpu.VMEM((tm + POOL_PAD, d), BF16),
                        pltpu.VMEM((tm + POOL_PAD, d_pool), BF16),
                        pltpu.VMEM((tm + POOL_PAD, d_pool), BF16)],
        compiler_params=_params(2),
        name="pool_branch",
    )(hx, hx, hm, w_in_b, band, pool_w_b, pool_scale)


def _merge_kernel(h_ref, a_ref, yb_ref, wga_ref, wgb_ref, wao_ref, o_ref):
    h = h_ref[...]
    ga = jax.nn.sigmoid(_dot(h, wga_ref[...]))
    gb = jax.nn.sigmoid(_dot(h, wgb_ref[...]))
    ya = _dot(a_ref[...], wao_ref[...])
    o_ref[...] = (ga * ya + gb * yb_ref[...].astype(F32)).astype(o_ref.dtype)


def _merge(hx2, a2, yb2, w_in_b, w_a_out_b, col0, tm, tj):
    n, d = hx2.shape
    dc = a2.shape[1]
    nj = d // tj
    return pl.pallas_call(
        _merge_kernel,
        grid=(n // tm, nj),
        in_specs=[pl.BlockSpec((tm, d), lambda i, j: (i, 0)),
                  pl.BlockSpec((tm, dc), lambda i, j: (i, 0)),
                  pl.BlockSpec((tm, tj), lambda i, j: (i, j)),
                  pl.BlockSpec((d, tj), lambda i, j: (0, col0 // tj + j)),
                  pl.BlockSpec((d, tj), lambda i, j: (0, col0 // tj + nj + j)),
                  pl.BlockSpec((dc, tj), lambda i, j: (0, j))],
        out_specs=pl.BlockSpec((tm, tj), lambda i, j: (i, j)),
        out_shape=jax.ShapeDtypeStruct((n, d), BF16),
        compiler_params=_params(2),
        name="merge",
    )(hx2, a2, yb2, w_in_b, w_in_b, w_a_out_b)


def _mix_kernel(m_ref, x_ref, lig_ref, lib_ref, wo_ref, g1_ref, b1_ref, rwh_ref, rwl_ref, rb_ref, tri_ref,
                h1_ref, h1p_ref, eid_ref, wts_ref, rank_ref, cnt_ref, carry_ref, *, rows):
    step = pl.program_id(0)

    @pl.when(step == 0)
    def _():
        carry_ref[...] = jnp.zeros(carry_ref.shape, F32)

    for r0 in range(0, m_ref.shape[0], rows):
        rs = slice(r0, r0 + rows)
        h0 = _layer_norm(x_ref[rs, :], lig_ref[...], lib_ref[...])
        z = DEEPNORM_ALPHA * h0 + _dot(m_ref[rs, :], wo_ref[...])
        h1 = _layer_norm(z, g1_ref[...], b1_ref[...])
        h1_ref[rs, :] = h1
        h_hi = h1.astype(BF16)
        half = h1.shape[1] // 2
        h1p_ref[rs, :] = _pack_pairs(h_hi[:, :half], h_hi[:, half:])

        h_lo = (h1 - h_hi.astype(F32)).astype(BF16)
        nt = (((1,), (1,)), ((), ()))
        rwh = rwh_ref[...]
        logits = (lax.dot_general(rwh, h_hi, nt, preferred_element_type=F32)
                  + lax.dot_general(rwh, h_lo, nt, preferred_element_type=F32)
                  + lax.dot_general(rwl_ref[...], h_hi, nt, preferred_element_type=F32)
                  + rb_ref[...])
        n_exp = logits.shape[0]
        e_iota = lax.broadcasted_iota(I32, logits.shape, 0)
        work = logits
        vals, hots = [], []
        for k in range(TOP_K):
            m = jnp.max(work, axis=0, keepdims=True)
            idx = jnp.min(jnp.where(work == m, e_iota, n_exp), axis=0, keepdims=True)
            hot = e_iota == idx
            vals.append(m)
            hots.append(hot)
            eid_ref[k:k + 1, rs] = idx
            work = jnp.where(hot, -jnp.inf, work)
        exps = [jnp.exp(v - vals[0]) for v in vals]
        denom = exps[0]
        for e in exps[1:]:
            denom = denom + e
        for k in range(TOP_K):
            wts_ref[k:k + 1, rs] = exps[k] / denom

        multi = hots[0]
        for hot in hots[1:]:
            multi = multi | hot
        multi_f = jnp.where(multi, 1.0, 0.0).astype(F32)
        prefix = _dot(multi_f.astype(BF16), tri_ref[...]) + carry_ref[:, 0:1]
        for k in range(TOP_K):
            rank_ref[k:k + 1, rs] = jnp.sum(jnp.where(hots[k], prefix, 0.0), axis=0, keepdims=True).astype(I32)
        carry_ref[...] = carry_ref[...] + jnp.sum(multi_f, axis=1, keepdims=True)
    cnt_ref[...] = carry_ref[...]


def _mix_ln1(m2, x2, ln_in_g, ln_in_b, w_o_b, ln1_g, ln1_b, rw_hi, rw_lo, rb, tm):
    n, d = x2.shape
    n_exp = rw_hi.shape[0]
    rows = min(MIX_HALF, tm)
    tri = (jnp.arange(rows, dtype=I32)[:, None] < jnp.arange(rows, dtype=I32)[None, :]).astype(BF16)
    row = lambda i: (i, 0)
    fixed = lambda i: (0, 0)
    col = lambda i: (0, i)
    return pl.pallas_call(
        functools.partial(_mix_kernel, rows=rows),
        grid=(n // tm,),
        in_specs=[pl.BlockSpec((tm, d), row), pl.BlockSpec((tm, d), row),
                  pl.BlockSpec((1, d), fixed), pl.BlockSpec((1, d), fixed),
                  pl.BlockSpec((d, d), fixed),
                  pl.BlockSpec((1, d), fixed), pl.BlockSpec((1, d), fixed),
                  pl.BlockSpec((n_exp, d), fixed), pl.BlockSpec((n_exp, d), fixed),
                  pl.BlockSpec((n_exp, 1), fixed),
                  pl.BlockSpec((rows, rows), fixed)],
        out_specs=[pl.BlockSpec((tm, d), row), pl.BlockSpec((tm, d // 2), row),
                   pl.BlockSpec((TOP_K, tm), col), pl.BlockSpec((TOP_K, tm), col), pl.BlockSpec((TOP_K, tm), col),
                   pl.BlockSpec((n_exp, 128), fixed)],
        out_shape=[jax.ShapeDtypeStruct((n, d), F32),
                   jax.ShapeDtypeStruct((n, d // 2), U32),
                   jax.ShapeDtypeStruct((TOP_K, n), I32),
                   jax.ShapeDtypeStruct((TOP_K, n), F32),
                   jax.ShapeDtypeStruct((TOP_K, n), I32),
                   jax.ShapeDtypeStruct((n_exp, 128), F32)],
        scratch_shapes=[pltpu.VMEM((n_exp, 128), F32)],
        compiler_params=_params(1),
        name="mix_ln1",
    )(m2, x2, ln_in_g, ln_in_b, w_o_b, ln1_g, ln1_b, rw_hi, rw_lo, rb, tri)


def _row_copy(src_ref, src_row, dst_ref, dst_row, sem):
    return pltpu.make_async_copy(src_ref.at[pl.ds(src_row, 1)], dst_ref.at[pl.ds(dst_row, 1)], sem)


def _subtile(q):
    return pl.ds(pl.multiple_of(q * SLOT_ROWS, SLOT_ROWS), SLOT_ROWS)


def _dispatch_kernel(pos_ref, gl_ref, gn_ref, used_ref, h_ref, xs_ref, zero_ref, sem, zsem, *, tt):
    base = pl.program_id(0) * (TOP_K * tt)

    @pl.when(pl.program_id(0) == 0)
    def _():
        zero_ref[...] = jnp.zeros(zero_ref.shape, zero_ref.dtype)
        n_exp = gl_ref.shape[0]
        n_sub = xs_ref.shape[0] // SLOT_ROWS

        def zero_copy(q):
            return pltpu.make_async_copy(zero_ref, xs_ref.at[_subtile(q)], zsem)

        def group_tail(e, start):
            @pl.when(gn_ref[e] > 0)
            def _():
                c = zero_copy(gl_ref[e])
                c.start() if start else c.wait()

        def unused(q, start):
            c = zero_copy(q)
            c.start() if start else c.wait()

        for start in (True, False):
            lax.fori_loop(0, n_exp, lambda e, c: (group_tail(e, start), c)[1], 0)
            lax.fori_loop(used_ref[1], n_sub, lambda q, c: (unused(q, start), c)[1], 0)

    def issue(t, c):
        for k in range(TOP_K):
            _row_copy(h_ref, t, xs_ref, pos_ref[base + k * tt + t], sem).start(priority=k % 2)
        return c

    lax.fori_loop(0, tt, issue, 0)
    for k in range(TOP_K):
        pltpu.make_async_copy(h_ref, xs_ref.at[pl.ds(0, tt)], sem).wait()


def _dispatch(h1p, pos_tiles, group_last, group_nsub, used, n_slots, tt):
    n, dw = h1p.shape
    grid_spec = pltpu.PrefetchScalarGridSpec(
        num_scalar_prefetch=4,
        grid=(n // tt,),
        in_specs=[pl.BlockSpec((tt, dw), lambda i, *_: (i, 0))],
        out_specs=pl.BlockSpec(memory_space=pl.ANY),
        scratch_shapes=[pltpu.VMEM((SLOT_ROWS, dw), h1p.dtype),
                        pltpu.SemaphoreType.DMA(()), pltpu.SemaphoreType.DMA(())],
    )
    return pl.pallas_call(
        functools.partial(_dispatch_kernel, tt=tt),
        grid_spec=grid_spec,
        out_shape=jax.ShapeDtypeStruct((n_slots, dw), h1p.dtype),
        compiler_params=_params(1),
        name="dispatch",
    )(pos_tiles, group_last, group_nsub, used, h1p)


def _expert_kernel(ge_ref, gs_ref, gn_ref, nu_ref, xs_ref, wg_ref, wl_ref, wd_ref, bg_ref, bl_ref, bd_ref,
                   ys_ref, xb_ref, acc_ref, stage_ref, wgb_ref, wlb_ref, wdb_ref, sem_in, sem_out, *, nj):
    s = pl.program_id(0)
    j = pl.program_id(1)
    half = xb_ref.shape[1] // 2
    rows = _subtile

    def out_copy(first_sub, q):
        return pltpu.make_async_copy(acc_ref.at[rows(q)], ys_ref.at[rows(first_sub + q)], sem_out.at[q])

    def in_copy(first_sub, q, slot):
        return pltpu.make_async_copy(xs_ref.at[rows(first_sub + q)], stage_ref.at[slot], sem_in.at[slot])

    @pl.when(s < nu_ref[0])
    def _():
        first_sub = gs_ref[s]
        nsub = gn_ref[s]

        wgb_ref[...] = wg_ref[0].astype(BF16)
        wlb_ref[...] = wl_ref[0].astype(BF16)
        wdb_ref[...] = wd_ref[0].astype(BF16)

        @pl.when(j == 0)
        def _():
            @pl.when(s > 0)
            def _():
                prev_first = gs_ref[s - 1]

                def wait_out(q, c):
                    out_copy(prev_first, q).wait()
                    return c

                lax.fori_loop(0, gn_ref[s - 1], wait_out, 0)

            in_copy(first_sub, 0, 0).start()

            def load(q, c):
                slot = q % 2
                in_copy(first_sub, q, slot).wait()

                @pl.when(q + 1 < nsub)
                def _():
                    in_copy(first_sub, q + 1, 1 - slot).start()

                lo, hi = _unpack_pairs(stage_ref[slot])
                xb_ref[rows(q), 0:half] = lo
                xb_ref[rows(q), half:] = hi
                return c

            lax.fori_loop(0, nsub, load, 0)

        def make_body(first, last):
            def body(q, c):
                x = xb_ref[rows(q), :]
                gate = _dot(x, wgb_ref[...]) + bg_ref[0]
                lin = _dot(x, wlb_ref[...]) + bl_ref[0]
                gate = jnp.minimum(gate, SWIGLU_LIMIT)
                lin = jnp.clip(lin, -SWIGLU_LIMIT, SWIGLU_LIMIT)
                act = (lin + 1.0) * gate * jax.nn.sigmoid(SWIGLU_ALPHA * gate)
                contrib = _dot(act.astype(BF16), wdb_ref[...])
                if first:
                    acc_ref[rows(q), :] = contrib + bd_ref[0]
                else:
                    acc_ref[rows(q), :] += contrib
                if last:
                    out_copy(first_sub, q).start()
                return c
            return body

        def run(first, last):
            body = make_body(first, last)

            def pair(p, c):
                body(2 * p, c)
                return body(2 * p + 1, c)

            lax.fori_loop(0, nsub // 2, pair, 0)

            @pl.when(nsub % 2 == 1)
            def _():
                body(nsub - 1, 0)

        @pl.when(j == 0)
        def _():
            run(True, nj == 1)

        if nj > 2:
            @pl.when((j > 0) & (j < nj - 1))
            def _():
                run(False, False)

        if nj > 1:
            @pl.when(j == nj - 1)
            def _():
                run(False, True)

        @pl.when((s == nu_ref[0] - 1) & (j == nj - 1))
        def _():
            def wait_out(q, c):
                out_copy(first_sub, q).wait()
                return c

            lax.fori_loop(0, nsub, wait_out, 0)

    @pl.when((s >= nu_ref[0]) & (j == 0))
    def _():
        @pl.when(s == nu_ref[0])
        def _():
            def zero(q, c):
                acc_ref[rows(q), :] = jnp.zeros((SLOT_ROWS, acc_ref.shape[1]), F32)
                return c

            lax.fori_loop(0, GROUP_SUBTILES, zero, 0)

        tail_first = nu_ref[1] + (s - nu_ref[0]) * GROUP_SUBTILES
        n_tail = jnp.clip(ys_ref.shape[0] // SLOT_ROWS - tail_first, 0, GROUP_SUBTILES)

        def start_zero(q, c):
            out_copy(tail_first, q).start()
            return c

        def wait_zero(q, c):
            out_copy(tail_first, q).wait()
            return c

        lax.fori_loop(0, n_tail, start_zero, 0)
        lax.fori_loop(0, n_tail, wait_zero, 0)


def _experts(xs, group_expert, group_first, group_nsub, n_groups, w_gate_up, b_gate_up, w_down, b_down, tn):
    n_slots, dw = xs.shape
    d = 2 * dw
    n_exp, _, de2 = w_gate_up.shape
    de = de2 // 2
    nj = de // tn
    max_groups = group_expert.shape[0]
    rmax = GROUP_SUBTILES * SLOT_ROWS

    def gi(s, nu):
        return jnp.minimum(s, nu[0] - 1)

    def jj(s, j, nu):
        return jnp.where(s < nu[0], j, nj - 1)

    grid_spec = pltpu.PrefetchScalarGridSpec(
        num_scalar_prefetch=4,
        grid=(max_groups, nj),
        in_specs=[pl.BlockSpec(memory_space=pl.ANY),
                  pl.BlockSpec((1, d, tn), lambda s, j, ge, gs, gn, nu: (ge[gi(s, nu)], 0, jj(s, j, nu))),
                  pl.BlockSpec((1, d, tn), lambda s, j, ge, gs, gn, nu: (ge[gi(s, nu)], 0, nj + jj(s, j, nu))),
                  pl.BlockSpec((1, tn, d), lambda s, j, ge, gs, gn, nu: (ge[gi(s, nu)], jj(s, j, nu), 0)),
                  pl.BlockSpec((1, 1, tn), lambda s, j, ge, gs, gn, nu: (ge[gi(s, nu)], 0, jj(s, j, nu))),
                  pl.BlockSpec((1, 1, tn), lambda s, j, ge, gs, gn, nu: (ge[gi(s, nu)], 0, nj + jj(s, j, nu))),
                  pl.BlockSpec((1, 1, d), lambda s, j, ge, gs, gn, nu: (ge[gi(s, nu)], 0, 0))],
        out_specs=pl.BlockSpec(memory_space=pl.ANY),
        scratch_shapes=[pltpu.VMEM((rmax, d), BF16),
                        pltpu.VMEM((rmax, d), F32),
                        pltpu.VMEM((2, SLOT_ROWS, dw), U32),
                        pltpu.VMEM((d, tn), BF16), pltpu.VMEM((d, tn), BF16), pltpu.VMEM((tn, d), BF16),
                        pltpu.SemaphoreType.DMA((2,)),
                        pltpu.SemaphoreType.DMA((GROUP_SUBTILES,))],
    )
    return pl.pallas_call(
        functools.partial(_expert_kernel, nj=nj),
        grid_spec=grid_spec,
        out_shape=jax.ShapeDtypeStruct((n_slots, d), F32),
        compiler_params=_params(2),
        name="experts",
    )(group_expert, group_first, group_nsub, n_groups, xs, w_gate_up, w_gate_up, w_down,
      b_gate_up.reshape(n_exp, 1, de2), b_gate_up.reshape(n_exp, 1, de2), b_down.reshape(n_exp, 1, d))


def _combine_kernel(pos_ref, h_ref, w_ref, g_ref, b_ref, ys_ref, o_ref, buf_ref, sem, *, tt, n_steps):
    i = pl.program_id(0)
    slot = i % 2
    ch = min(COMBINE_CHUNK, tt)

    def issue_rows(step, dst_slot, t0):
        base = step * (TOP_K * tt)
        for r in range(ch):
            for k in range(TOP_K):
                _row_copy(ys_ref, pos_ref[base + k * tt + t0 + r], buf_ref.at[dst_slot, k], t0 + r,
                          sem.at[dst_slot]).start(priority=k % 2)

    def wait_rows(src_slot):
        for k in range(TOP_K):
            pltpu.make_async_copy(ys_ref.at[pl.ds(0, tt)], buf_ref.at[src_slot, k], sem.at[src_slot]).wait()

    @pl.when(i == 0)
    def _():
        lax.fori_loop(0, tt // ch, lambda c, z: (issue_rows(0, 0, c * ch), z)[1], 0)

    wait_rows(slot)
    nxt = lax.rem(i + 1, n_steps)

    def chunk(c, z):
        r0 = pl.multiple_of(c * ch, ch)
        issue_rows(nxt, 1 - slot, r0)
        rs = pl.ds(r0, ch)
        w = w_ref[rs, :]
        y = w[:, 0:1] * buf_ref[slot, 0, rs, :]
        for k in range(1, TOP_K):
            y = y + w[:, k:k + 1] * buf_ref[slot, k, rs, :]
        o_ref[rs, :] = _layer_norm(DEEPNORM_ALPHA * h_ref[rs, :] + y, g_ref[...], b_ref[...])
        return z

    lax.fori_loop(0, tt // ch, chunk, 0)

    @pl.when(i == n_steps - 1)
    def _():
        wait_rows(1 - slot)


def _combine(h1, pos_tiles, wts_t, ys, ln2_g, ln2_b, tt):
    n, d = h1.shape
    n_steps = n // tt
    grid_spec = pltpu.PrefetchScalarGridSpec(
        num_scalar_prefetch=1,
        grid=(n_steps,),
        in_specs=[pl.BlockSpec((tt, d), lambda i, pos: (i, 0)),
                  pl.BlockSpec((tt, TOP_K), lambda i, pos: (i, 0)),
                  pl.BlockSpec((1, d), lambda i, pos: (0, 0)),
                  pl.BlockSpec((1, d), lambda i, pos: (0, 0)),
                  pl.BlockSpec(memory_space=pl.ANY)],
        out_specs=pl.BlockSpec((tt, d), lambda i, pos: (i, 0)),
        scratch_shapes=[pltpu.VMEM((2, TOP_K, tt, d), F32), pltpu.SemaphoreType.DMA((2,))],
    )
    return pl.pallas_call(
        functools.partial(_combine_kernel, tt=tt, n_steps=n_steps),
        grid_spec=grid_spec,
        out_shape=jax.ShapeDtypeStruct((n, d), F32),
        compiler_params=_params(1),
        name="combine",
    )(pos_tiles, h1, wts_t, ln2_g, ln2_b, ys)


def _routing_tables(counts, eid, rank, n_tok):
    n_exp = counts.shape[0]
    e_ids = jnp.arange(n_exp, dtype=I32)
    nsub = (counts + SLOT_ROWS - 1) // SLOT_ROWS
    ngrp = (nsub + GROUP_SUBTILES - 1) // GROUP_SUBTILES
    base = nsub // jnp.maximum(ngrp, 1)
    rem = nsub - base * ngrp
    first_sub = jnp.cumsum(nsub) - nsub
    pos = jnp.sum(jnp.where(eid[..., None] == e_ids, first_sub * SLOT_ROWS, 0), axis=-1) + rank

    n_subtiles = (n_tok * TOP_K) // SLOT_ROWS + n_exp
    max_groups = (n_subtiles + GROUP_SUBTILES - 1) // GROUP_SUBTILES + n_exp
    grp_end = jnp.cumsum(ngrp)
    g_ids = jnp.arange(max_groups, dtype=I32)
    g_exp = jnp.minimum(jnp.sum((grp_end[None, :] <= g_ids[:, None]).astype(I32), axis=1), n_exp - 1)
    pick = lambda tab: jnp.sum(jnp.where(g_exp[:, None] == e_ids[None, :], tab[None, :], 0), axis=1)
    local = g_ids - pick(grp_end - ngrp)
    g_base, g_rem = pick(base), pick(rem)
    g_nsub = g_base + (local < g_rem).astype(I32)
    g_first = pick(first_sub) + local * g_base + jnp.minimum(local, g_rem)
    used = jnp.stack([grp_end[-1], jnp.sum(nsub)]).astype(I32)
    return dict(pos=pos, n_slots=n_subtiles * SLOT_ROWS, used=used,
                group_expert=g_exp.astype(I32), group_first=g_first.astype(I32), group_nsub=g_nsub.astype(I32),
                expert_last=(first_sub + nsub - 1).astype(I32), expert_nsub=nsub.astype(I32))


def _tiles(seq, n_tok):
    return dict(
        ln_rows=min(512, n_tok),
        conv_rows=min(512, seq), conv_cols=512,
        pool_rows=min(1024, seq),
        merge_rows=min(512, n_tok), merge_cols=512,
        mix_rows=min(512, n_tok),
        route_rows=min(128, n_tok),
        expert_cols=256,
    )


def kernel(x, meta_tokens, ln_in_g, ln_in_b, w_in, conv_w, w_a_out, pool_w, pool_scale, w_o, ln1_g, ln1_b,
           router_w, router_b, w_gate_up, b_gate_up, w_down, b_down, ln2_g, ln2_b):
    bsz, seq, d = x.shape
    assert w_in.shape[0] == DEPTH and meta_tokens.shape[0] == N_META == HALO
    d_conv = conv_w.shape[-1]
    n_groups, pool_cin, pool_cout = pool_w.shape[1:]
    d_pool = n_groups * pool_cin
    assert n_groups == len(POOL_WINDOWS) and n_groups * pool_cout == d
    n_exp = router_w.shape[-1]
    n_tok = bsz * seq
    t = _tiles(seq, n_tok)
    row = lambda v: v.reshape(1, -1).astype(F32)

    w_in_b = w_in[0].astype(BF16)
    x2 = x.reshape(n_tok, d)
    lig, lib = row(ln_in_g), row(ln_in_b)

    hx2 = _ln_cast(x2, lig, lib, t["ln_rows"])
    hm = _ln_cast(meta_tokens.astype(F32), lig, lib, N_META)
    hx = hx2.reshape(bsz, seq, d)

    a = _conv_branch(hx, hm, w_in_b, conv_w[0], d_conv, t["conv_rows"], min(t["conv_cols"], d_conv))
    yb = _pool_branch(hx, hm, w_in_b, pool_w[0].astype(BF16), row(pool_scale[0]), 3 * d_conv, t["pool_rows"])
    m2 = _merge(hx2, a.reshape(n_tok, d_conv), yb.reshape(n_tok, d), w_in_b, w_a_out[0].astype(BF16),
                3 * d_conv + d_pool, t["merge_rows"], min(t["merge_cols"], d))

    rw_t = router_w[0].T
    rw_hi = rw_t.astype(BF16)
    rw_lo = (rw_t - rw_hi.astype(F32)).astype(BF16)
    h1, h1p, eid, wts, rank, cnt = _mix_ln1(m2, x2, lig, lib, w_o[0].astype(BF16), row(ln1_g[0]), row(ln1_b[0]),
                                            rw_hi, rw_lo, router_b[0].reshape(n_exp, 1), t["mix_rows"])

    rt = _routing_tables(cnt[:, 0].astype(I32), eid, rank, n_tok)
    tt = t["route_rows"]
    pos_tiles = rt["pos"].reshape(TOP_K, n_tok // tt, tt).transpose(1, 0, 2).reshape(-1)

    xs = _dispatch(h1p, pos_tiles, rt["expert_last"], rt["expert_nsub"], rt["used"], rt["n_slots"], tt)
    ys = _experts(xs, rt["group_expert"], rt["group_first"], rt["group_nsub"], rt["used"],
                  w_gate_up[0], b_gate_up[0], w_down[0], b_down[0], min(t["expert_cols"], w_down.shape[-2]))
    out = _combine(h1, pos_tiles, wts.T, ys, row(ln2_g[0]), row(ln2_b[0]), tt)
    return out.reshape(bsz, seq, d)
```

```python
import functools

import jax
import jax.numpy as jnp
from jax import lax
from jax.experimental import pallas as pl
from jax.experimental.pallas import tpu as pltpu

F32 = jnp.float32
BF16 = jnp.bfloat16
I32 = jnp.int32
U32 = jnp.uint32

N_META = 16
CONV_K = 3
POOL_WINDOWS = (2, 4, 8, 16)
TOP_K = 4
SWIGLU_LIMIT = 7.0
SWIGLU_ALPHA = 1.702
LN_EPS = 1e-5
DEPTH = 1
DEEPNORM_ALPHA = (2.0 * DEPTH) ** 0.25

HALO = 16
POOL_PAD = 128
POOL_CHUNK = 256
SLOT_ROWS = 256
GROUP_SUBTILES = 9
QUAD = 4
CAST_SLICES = 8
SLICES_PER_QUAD = 4
MIX_HALF = 256
COMBINE_CHUNK = 32
V7X_VMEM_LIMIT = 58 * 1024 * 1024

_ARB = "arbitrary"


def _params(n_axes):
    return pltpu.CompilerParams(dimension_semantics=(_ARB,) * n_axes, vmem_limit_bytes=V7X_VMEM_LIMIT)


def _layer_norm(x, g, b):
    mu = jnp.mean(x, axis=-1, keepdims=True)
    xc = x - mu
    var = jnp.mean(xc * xc, axis=-1, keepdims=True)
    return xc * lax.rsqrt(var + LN_EPS) * g + b


def _dot(a, b):
    return jnp.dot(a, b, preferred_element_type=F32)


def _pack_pairs(lo_bf16, hi_bf16):
    lo = lax.bitcast_convert_type(lo_bf16.astype(F32), U32) >> 16
    hi = lax.bitcast_convert_type(hi_bf16.astype(F32), U32) & jnp.uint32(0xFFFF0000)
    return hi | lo


def _unpack_pairs(words):
    lo = lax.bitcast_convert_type(words << 16, F32).astype(BF16)
    hi = lax.bitcast_convert_type(words & jnp.uint32(0xFFFF0000), F32).astype(BF16)
    return lo, hi


def _ln_cast_kernel(x_ref, g_ref, b_ref, o_ref):
    o_ref[...] = _layer_norm(x_ref[...], g_ref[...], b_ref[...]).astype(o_ref.dtype)


def _ln_cast(x2d, g, b, tr):
    n, d = x2d.shape
    return pl.pallas_call(
        _ln_cast_kernel,
        grid=(n // tr,),
        in_specs=[pl.BlockSpec((tr, d), lambda i: (i, 0)),
                  pl.BlockSpec((1, d), lambda i: (0, 0)),
                  pl.BlockSpec((1, d), lambda i: (0, 0))],
        out_specs=pl.BlockSpec((tr, d), lambda i: (i, 0)),
        out_shape=jax.ShapeDtypeStruct((n, d), BF16),
        compiler_params=_params(1),
        name="ln_cast",
    )(x2d, g, b)


def _conv_kernel(h_ref, halo_ref, hm_ref, wb_ref, wc_ref, wu_ref, cw_ref, o_ref, he_ref, cu_ref, *, tm):
    i = pl.program_id(1)
    j = pl.program_id(2)

    @pl.when(j == 0)
    def _():
        he_ref[HALO:, :] = h_ref[0]

    @pl.when((j == 0) & (i == 0))
    def _():
        he_ref[0:HALO, :] = hm_ref[...]

    @pl.when((j == 0) & (i > 0))
    def _():
        he_ref[0:HALO, :] = halo_ref[0]

    he = he_ref[...]
    cu_ref[...] = _dot(he, wc_ref[...]) * _dot(he, wu_ref[...])
    bg = _dot(he_ref[HALO:, :], wb_ref[...])
    cw = cw_ref[...]
    y = cw[CONV_K - 1:CONV_K, :] * cu_ref[HALO:HALO + tm, :]
    for k in range(CONV_K - 1):
        s = HALO - (CONV_K - 1) + k
        y = y + cw[k:k + 1, :] * cu_ref[s:s + tm, :]
    o_ref[0] = (bg * y).astype(o_ref.dtype)


def _conv_branch(hx, hm, w_in_b, conv_w, d_conv, tm, tc):
    bsz, seq, d = hx.shape
    nc = d_conv // tc
    hb = tm // HALO
    return pl.pallas_call(
        functools.partial(_conv_kernel, tm=tm),
        grid=(bsz, seq // tm, nc),
        in_specs=[pl.BlockSpec((1, tm, d), lambda b, i, j: (b, i, 0)),
                  pl.BlockSpec((1, HALO, d), lambda b, i, j: (b, jnp.maximum(i * hb - 1, 0), 0)),
                  pl.BlockSpec((HALO, d), lambda b, i, j: (0, 0)),
                  pl.BlockSpec((d, tc), lambda b, i, j: (0, j)),
                  pl.BlockSpec((d, tc), lambda b, i, j: (0, nc + j)),
                  pl.BlockSpec((d, tc), lambda b, i, j: (0, 2 * nc + j)),
                  pl.BlockSpec((CONV_K, tc), lambda b, i, j: (0, j))],
        out_specs=pl.BlockSpec((1, tm, tc), lambda b, i, j: (b, i, j)),
        out_shape=jax.ShapeDtypeStruct((bsz, seq, d_conv), BF16),
        scratch_shapes=[pltpu.VMEM((tm + HALO, d), BF16), pltpu.VMEM((tm + HALO, tc), F32)],
        compiler_params=_params(3),
        name="conv_branch",
    )(hx, hx, hm, w_in_b, w_in_b, w_in_b, conv_w)


def _pool_kernel(h_ref, halo_ref, hm_ref, wv_ref, band_ref, pw_ref, ps_ref, o_ref, he_ref, vh_ref, vl_ref, *, tm):
    i = pl.program_id(1)
    lead = POOL_PAD - HALO
    he_ref[0:lead, :] = jnp.zeros((lead, he_ref.shape[1]), he_ref.dtype)
    he_ref[POOL_PAD:, :] = h_ref[0]

    @pl.when(i == 0)
    def _():
        he_ref[lead:POOL_PAD, :] = hm_ref[...]

    @pl.when(i > 0)
    def _():
        he_ref[lead:POOL_PAD, :] = halo_ref[0]

    v = _dot(he_ref[...], wv_ref[...])
    v_hi = v.astype(BF16)
    vh_ref[...] = v_hi
    vl_ref[...] = (v - v_hi.astype(F32)).astype(BF16)
    ng, cin, cout = pw_ref.shape
    rows = min(POOL_CHUNK, tm)
    for c in range(tm // rows):
        r0 = c * rows
        for g in range(ng):
            band = band_ref[g]
            win = (slice(r0, r0 + rows + POOL_PAD), slice(g * cin, (g + 1) * cin))
            pooled = _dot(band, vh_ref[win]) + _dot(band, vl_ref[win])
            y = _dot(pooled.astype(BF16), pw_ref[g]) * ps_ref[:, g * cout:(g + 1) * cout]
            o_ref[0, r0:r0 + rows, g * cout:(g + 1) * cout] = y.astype(o_ref.dtype)


def _pool_band(rows):
    t = jnp.arange(rows, dtype=I32)[:, None] + POOL_PAD
    s = jnp.arange(rows + POOL_PAD, dtype=I32)[None, :]
    bands = []
    for w in POOL_WINDOWS:
        inside = ((s <= t) & (s > t - w)).astype(F32) / w
        bands.append(inside - (s == t).astype(F32))
    return jnp.stack(bands).astype(BF16)


def _pool_branch(hx, hm, w_in_b, pool_w_b, pool_scale, col0, tm):
    bsz, seq, d = hx.shape
    ng, cin, cout = pool_w_b.shape
    d_pool = ng * cin
    hb = tm // HALO
    band = _pool_band(min(POOL_CHUNK, tm))
    return pl.pallas_call(
        functools.partial(_pool_kernel, tm=tm),
        grid=(bsz, seq // tm),
        in_specs=[pl.BlockSpec((1, tm, d), lambda b, i: (b, i, 0)),
                  pl.BlockSpec((1, HALO, d), lambda b, i: (b, jnp.maximum(i * hb - 1, 0), 0)),
                  pl.BlockSpec((HALO, d), lambda b, i: (0, 0)),
                  pl.BlockSpec((d, d_pool), lambda b, i: (0, col0 // d_pool)),
                  pl.BlockSpec(band.shape, lambda b, i: (0, 0, 0)),
                  pl.BlockSpec((ng, cin, cout), lambda b, i: (0, 0, 0)),
                  pl.BlockSpec((1, ng * cout), lambda b, i: (0, 0))],
        out_specs=pl.BlockSpec((1, tm, ng * cout), lambda b, i: (b, i, 0)),
        out_shape=jax.ShapeDtypeStruct((bsz, seq, ng * cout), BF16),
        scratch_shapes=[pltpu.VMEM((tm + POOL_PAD, d), BF16),
                        pltpu.VMEM((tm + POOL_PAD, d_pool), BF16),
                        pltpu.VMEM((tm + POOL_PAD, d_pool), BF16)],
        compiler_params=_params(2),
        name="pool_branch",
    )(hx, hx, hm, w_in_b, band, pool_w_b, pool_scale)


def _merge_kernel(h_ref, a_ref, yb_ref, wga_ref, wgb_ref, wao_ref, o_ref):
    h = h_ref[...]
    ga = jax.nn.sigmoid(_dot(h, wga_ref[...]))
    gb = jax.nn.sigmoid(_dot(h, wgb_ref[...]))
    ya = _dot(a_ref[...], wao_ref[...])
    o_ref[...] = (ga * ya + gb * yb_ref[...].astype(F32)).astype(o_ref.dtype)


def _merge(hx2, a2, yb2, w_in_b, w_a_out_b, col0, tm, tj):
    n, d = hx2.shape
    dc = a2.shape[1]
    nj = d // tj
    return pl.pallas_call(
        _merge_kernel,
        grid=(n // tm, nj),
        in_specs=[pl.BlockSpec((tm, d), lambda i, j: (i, 0)),
                  pl.BlockSpec((tm, dc), lambda i, j: (i, 0)),
                  pl.BlockSpec((tm, tj), lambda i, j: (i, j)),
                  pl.BlockSpec((d, tj), lambda i, j: (0, col0 // tj + j)),
                  pl.BlockSpec((d, tj), lambda i, j: (0, col0 // tj + nj + j)),
                  pl.BlockSpec((dc, tj), lambda i, j: (0, j))],
        out_specs=pl.BlockSpec((tm, tj), lambda i, j: (i, j)),
        out_shape=jax.ShapeDtypeStruct((n, d), BF16),
        compiler_params=_params(2),
        name="merge",
    )(hx2, a2, yb2, w_in_b, w_in_b, w_a_out_b)


def _mix_kernel(m_ref, x_ref, lig_ref, lib_ref, wo_ref, g1_ref, b1_ref, rwh_ref, rwl_ref, rb_ref, tri_ref,
                h1_ref, h1p_ref, eid_ref, wts_ref, rank_ref, cnt_ref, carry_ref, *, rows):
    step = pl.program_id(0)

    @pl.when(step == 0)
    def _():
        carry_ref[...] = jnp.zeros(carry_ref.shape, F32)

    for r0 in range(0, m_ref.shape[0], rows):
        rs = slice(r0, r0 + rows)
        h0 = _layer_norm(x_ref[rs, :], lig_ref[...], lib_ref[...])
        z = DEEPNORM_ALPHA * h0 + _dot(m_ref[rs, :], wo_ref[...])
        h1 = _layer_norm(z, g1_ref[...], b1_ref[...])
        h1_ref[rs, :] = h1
        h_hi = h1.astype(BF16)
        half = h1.shape[1] // 2
        h1p_ref[rs, :] = _pack_pairs(h_hi[:, :half], h_hi[:, half:])

        h_lo = (h1 - h_hi.astype(F32)).astype(BF16)
        nt = (((1,), (1,)), ((), ()))
        rwh = rwh_ref[...]
        logits = (lax.dot_general(rwh, h_hi, nt, preferred_element_type=F32)
                  + lax.dot_general(rwh, h_lo, nt, preferred_element_type=F32)
                  + lax.dot_general(rwl_ref[...], h_hi, nt, preferred_element_type=F32)
                  + rb_ref[...])
        n_exp = logits.shape[0]
        e_iota = lax.broadcasted_iota(I32, logits.shape, 0)
        work = logits
        vals, hots = [], []
        for k in range(TOP_K):
            m = jnp.max(work, axis=0, keepdims=True)
            idx = jnp.min(jnp.where(work == m, e_iota, n_exp), axis=0, keepdims=True)
            hot = e_iota == idx
            vals.append(m)
            hots.append(hot)
            eid_ref[k:k + 1, rs] = idx
            work = jnp.where(hot, -jnp.inf, work)
        exps = [jnp.exp(v - vals[0]) for v in vals]
        denom = exps[0]
        for e in exps[1:]:
            denom = denom + e
        for k in range(TOP_K):
            wts_ref[k:k + 1, rs] = exps[k] / denom

        multi = hots[0]
        for hot in hots[1:]:
            multi = multi | hot
        multi_f = jnp.where(multi, 1.0, 0.0).astype(F32)
        prefix = _dot(multi_f.astype(BF16), tri_ref[...]) + carry_ref[:, 0:1]
        for k in range(TOP_K):
            rank_ref[k:k + 1, rs] = jnp.sum(jnp.where(hots[k], prefix, 0.0), axis=0, keepdims=True).astype(I32)
        carry_ref[...] = carry_ref[...] + jnp.sum(multi_f, axis=1, keepdims=True)
    cnt_ref[...] = carry_ref[...]


def _mix_ln1(m2, x2, ln_in_g, ln_in_b, w_o_b, ln1_g, ln1_b, rw_hi, rw_lo, rb, tm):
    n, d = x2.shape
    n_exp = rw_hi.shape[0]
    rows = min(MIX_HALF, tm)
    tri = (jnp.arange(rows, dtype=I32)[:, None] < jnp.arange(rows, dtype=I32)[None, :]).astype(BF16)
    row = lambda i: (i, 0)
    fixed = lambda i: (0, 0)
    col = lambda i: (0, i)
    return pl.pallas_call(
        functools.partial(_mix_kernel, rows=rows),
        grid=(n // tm,),
        in_specs=[pl.BlockSpec((tm, d), row), pl.BlockSpec((tm, d), row),
                  pl.BlockSpec((1, d), fixed), pl.BlockSpec((1, d), fixed),
                  pl.BlockSpec((d, d), fixed),
                  pl.BlockSpec((1, d), fixed), pl.BlockSpec((1, d), fixed),
                  pl.BlockSpec((n_exp, d), fixed), pl.BlockSpec((n_exp, d), fixed),
                  pl.BlockSpec((n_exp, 1), fixed),
                  pl.BlockSpec((rows, rows), fixed)],
        out_specs=[pl.BlockSpec((tm, d), row), pl.BlockSpec((tm, d // 2), row),
                   pl.BlockSpec((TOP_K, tm), col), pl.BlockSpec((TOP_K, tm), col), pl.BlockSpec((TOP_K, tm), col),
                   pl.BlockSpec((n_exp, 128), fixed)],
        out_shape=[jax.ShapeDtypeStruct((n, d), F32),
                   jax.ShapeDtypeStruct((n, d // 2), U32),
                   jax.ShapeDtypeStruct((TOP_K, n), I32),
                   jax.ShapeDtypeStruct((TOP_K, n), F32),
                   jax.ShapeDtypeStruct((TOP_K, n), I32),
                   jax.ShapeDtypeStruct((n_exp, 128), F32)],
        scratch_shapes=[pltpu.VMEM((n_exp, 128), F32)],
        compiler_params=_params(1),
        name="mix_ln1",
    )(m2, x2, ln_in_g, ln_in_b, w_o_b, ln1_g, ln1_b, rw_hi, rw_lo, rb, tri)


def _row_copy(src_ref, src_row, dst_ref, dst_row, sem):
    return pltpu.make_async_copy(src_ref.at[pl.ds(src_row, 1)], dst_ref.at[pl.ds(dst_row, 1)], sem)


def _subtile(q):
    return pl.ds(pl.multiple_of(q * SLOT_ROWS, SLOT_ROWS), SLOT_ROWS)


def _dispatch_kernel(pos_ref, gl_ref, gn_ref, used_ref, h_ref, xs_ref, zero_ref, sem, zsem, *, tt):
    base = pl.program_id(0) * (TOP_K * tt)

    @pl.when(pl.program_id(0) == 0)
    def _():
        zero_ref[...] = jnp.zeros(zero_ref.shape, zero_ref.dtype)
        n_exp = gl_ref.shape[0]
        n_sub = xs_ref.shape[0] // SLOT_ROWS

        def zero_copy(q):
            return pltpu.make_async_copy(zero_ref, xs_ref.at[_subtile(q)], zsem)

        def group_tail(e, start):
            @pl.when(gn_ref[e] > 0)
            def _():
                c = zero_copy(gl_ref[e])
                c.start() if start else c.wait()

        def unused(q, start):
            c = zero_copy(q)
            c.start() if start else c.wait()

        for start in (True, False):
            lax.fori_loop(0, n_exp, lambda e, c: (group_tail(e, start), c)[1], 0)
            lax.fori_loop(used_ref[1], n_sub, lambda q, c: (unused(q, start), c)[1], 0)

    def issue(t, c):
        for k in range(TOP_K):
            _row_copy(h_ref, t, xs_ref, pos_ref[base + k * tt + t], sem).start(priority=k % 2)
        return c

    lax.fori_loop(0, tt, issue, 0)
    for k in range(TOP_K):
        pltpu.make_async_copy(h_ref, xs_ref.at[pl.ds(0, tt)], sem).wait()


def _dispatch(h1p, pos_tiles, group_last, group_nsub, used, n_slots, tt):
    n, dw = h1p.shape
    grid_spec = pltpu.PrefetchScalarGridSpec(
        num_scalar_prefetch=4,
        grid=(n // tt,),
        in_specs=[pl.BlockSpec((tt, dw), lambda i, *_: (i, 0))],
        out_specs=pl.BlockSpec(memory_space=pl.ANY),
        scratch_shapes=[pltpu.VMEM((SLOT_ROWS, dw), h1p.dtype),
                        pltpu.SemaphoreType.DMA(()), pltpu.SemaphoreType.DMA(())],
    )
    return pl.pallas_call(
        functools.partial(_dispatch_kernel, tt=tt),
        grid_spec=grid_spec,
        out_shape=jax.ShapeDtypeStruct((n_slots, dw), h1p.dtype),
        compiler_params=_params(1),
        name="dispatch",
    )(pos_tiles, group_last, group_nsub, used, h1p)


def _expert_kernel(ge_ref, gs_ref, gn_ref, nu_ref, xs_ref, wg_ref, wl_ref, wd_ref, bg_ref, bl_ref, bd_ref,
                   ys_ref, xb_ref, acc_ref, stage_ref, wgb_ref, wlb_ref, wdb_ref, sem_in, sem_out, *, nj):
    t = pl.program_id(0)
    n_groups = nu_ref[0]
    n_items = n_groups * nj
    c = jnp.maximum(t - 1, 0)
    s = c // nj
    j = c - s * nj
    cast_slot = t % 2
    use_slot = 1 - cast_slot
    d, tn = wgb_ref.shape[1:]
    half = d // 2
    rows = _subtile

    def cast_slice(i):
        r = pl.ds(pl.multiple_of(i * (d // CAST_SLICES), d // CAST_SLICES), d // CAST_SLICES)
        wgb_ref[cast_slot, r, :] = wg_ref[0, r, :].astype(BF16)
        wlb_ref[cast_slot, r, :] = wl_ref[0, r, :].astype(BF16)
        r = pl.ds(pl.multiple_of(i * (tn // CAST_SLICES), tn // CAST_SLICES), tn // CAST_SLICES)
        wdb_ref[cast_slot, r, :] = wd_ref[0, r, :].astype(BF16)

    def cast_range(lo, hi):
        lax.fori_loop(lo, hi, lambda i, z: (cast_slice(i), z)[1], 0)

    def out_copy(first_sub, q):
        return pltpu.make_async_copy(acc_ref.at[rows(q)], ys_ref.at[rows(first_sub + q)], sem_out.at[q])

    def in_copy(first_sub, q, slot):
        return pltpu.make_async_copy(xs_ref.at[rows(first_sub + q)], stage_ref.at[slot], sem_in.at[slot])

    @pl.when(t == 0)
    def _():
        cast_range(0, CAST_SLICES)

    @pl.when((t >= 1) & (t <= n_items))
    def _():
        first_sub = gs_ref[s]
        nsub = gn_ref[s]

        @pl.when(j == 0)
        def _():
            @pl.when(s > 0)
            def _():
                prev_first = gs_ref[s - 1]
                lax.fori_loop(0, gn_ref[s - 1], lambda q, z: (out_copy(prev_first, q).wait(), z)[1], 0)

            in_copy(first_sub, 0, 0).start()

            def load(q, z):
                slot = q % 2
                in_copy(first_sub, q, slot).wait()

                @pl.when(q + 1 < nsub)
                def _():
                    in_copy(first_sub, q + 1, 1 - slot).start()

                lo, hi = _unpack_pairs(stage_ref[slot])
                xb_ref[rows(q), 0:half] = lo
                xb_ref[rows(q), half:] = hi
                acc_ref[rows(q), :] = jnp.broadcast_to(bd_ref[0], (SLOT_ROWS, d))
                return z

            lax.fori_loop(0, nsub, load, 0)

        def sub(q):
            x = xb_ref[rows(q), :]
            gate = _dot(x, wgb_ref[use_slot]) + bg_ref[0]
            lin = _dot(x, wlb_ref[use_slot]) + bl_ref[0]
            gate = jnp.minimum(gate, SWIGLU_LIMIT)
            lin = jnp.clip(lin, -SWIGLU_LIMIT, SWIGLU_LIMIT)
            act = (lin + 1.0) * gate * jax.nn.sigmoid(SWIGLU_ALPHA * gate)
            acc_ref[rows(q), :] += _dot(act.astype(BF16), wdb_ref[use_slot])

        n_quads = nsub // QUAD

        def quad(p, z):
            for u in range(QUAD):
                sub(QUAD * p + u)
            for v in range(SLICES_PER_QUAD):
                cast_slice(SLICES_PER_QUAD * p + v)
            return z

        lax.fori_loop(0, n_quads, quad, 0)
        rem = nsub - QUAD * n_quads

        @pl.when(rem >= 2)
        def _():
            sub(QUAD * n_quads)
            sub(QUAD * n_quads + 1)

        @pl.when(rem % 2 == 1)
        def _():
            sub(nsub - 1)

        cast_range(n_quads * SLICES_PER_QUAD, CAST_SLICES)

        @pl.when(j == nj - 1)
        def _():
            lax.fori_loop(0, nsub, lambda q, z: (out_copy(first_sub, q).start(), z)[1], 0)

        @pl.when(t == n_items)
        def _():
            lax.fori_loop(0, nsub, lambda q, z: (out_copy(first_sub, q).wait(), z)[1], 0)
            acc_ref[rows(0), :] = jnp.zeros((SLOT_ROWS, d), F32)
            n_sub = ys_ref.shape[0] // SLOT_ROWS

            def tail_copy(q):
                return pltpu.make_async_copy(acc_ref.at[rows(0)], ys_ref.at[rows(q)], sem_out.at[0])

            lax.fori_loop(nu_ref[1], n_sub, lambda q, z: (tail_copy(q).start(), z)[1], 0)
            lax.fori_loop(nu_ref[1], n_sub, lambda q, z: (tail_copy(q).wait(), z)[1], 0)


def _experts(xs, group_expert, group_first, group_nsub, used, w_gate_up, b_gate_up, w_down, b_down, tn):
    n_slots, dw = xs.shape
    d = 2 * dw
    n_exp, _, de2 = w_gate_up.shape
    de = de2 // 2
    nj = de // tn
    max_items = group_expert.shape[0] * nj
    rmax = GROUP_SUBTILES * SLOT_ROWS
    assert GROUP_SUBTILES // QUAD * SLICES_PER_QUAD <= CAST_SLICES

    def fetched(t, nu):
        w = jnp.minimum(t, nu[0] * nj - 1)
        return w // nj, w % nj

    def computed(t, nu):
        w = jnp.clip(t - 1, 0, nu[0] * nj - 1)
        return w // nj, w % nj

    def wg_map(t, ge, gs, gn, nu):
        g, j = fetched(t, nu)
        return ge[g], 0, j

    def wl_map(t, ge, gs, gn, nu):
        g, j = fetched(t, nu)
        return ge[g], 0, nj + j

    def wd_map(t, ge, gs, gn, nu):
        g, j = fetched(t, nu)
        return ge[g], j, 0

    def bg_map(t, ge, gs, gn, nu):
        g, j = computed(t, nu)
        return ge[g], 0, j

    def bl_map(t, ge, gs, gn, nu):
        g, j = computed(t, nu)
        return ge[g], 0, nj + j

    def bd_map(t, ge, gs, gn, nu):
        g, _ = computed(t, nu)
        return ge[g], 0, 0

    grid_spec = pltpu.PrefetchScalarGridSpec(
        num_scalar_prefetch=4,
        grid=(max_items + 1,),
        in_specs=[pl.BlockSpec(memory_space=pl.ANY),
                  pl.BlockSpec((1, d, tn), wg_map), pl.BlockSpec((1, d, tn), wl_map),
                  pl.BlockSpec((1, tn, d), wd_map),
                  pl.BlockSpec((1, 1, tn), bg_map), pl.BlockSpec((1, 1, tn), bl_map),
                  pl.BlockSpec((1, 1, d), bd_map)],
        out_specs=pl.BlockSpec(memory_space=pl.ANY),
        scratch_shapes=[pltpu.VMEM((rmax, d), BF16),
                        pltpu.VMEM((rmax, d), F32),
                        pltpu.VMEM((2, SLOT_ROWS, dw), U32),
                        pltpu.VMEM((2, d, tn), BF16), pltpu.VMEM((2, d, tn), BF16), pltpu.VMEM((2, tn, d), BF16),
                        pltpu.SemaphoreType.DMA((2,)),
                        pltpu.SemaphoreType.DMA((GROUP_SUBTILES,))],
    )
    return pl.pallas_call(
        functools.partial(_expert_kernel, nj=nj),
        grid_spec=grid_spec,
        out_shape=jax.ShapeDtypeStruct((n_slots, d), F32),
        compiler_params=_params(1),
        name="experts",
    )(group_expert, group_first, group_nsub, used, xs, w_gate_up, w_gate_up, w_down,
      b_gate_up.reshape(n_exp, 1, de2), b_gate_up.reshape(n_exp, 1, de2), b_down.reshape(n_exp, 1, d))


def _combine_kernel(pos_ref, h_ref, w_ref, g_ref, b_ref, ys_ref, o_ref, buf_ref, sem, *, tt, n_steps):
    i = pl.program_id(0)
    slot = i % 2
    ch = min(COMBINE_CHUNK, tt)

    def issue_rows(step, dst_slot, t0):
        base = step * (TOP_K * tt)
        for r in range(ch):
            for k in range(TOP_K):
                _row_copy(ys_ref, pos_ref[base + k * tt + t0 + r], buf_ref.at[dst_slot, k], t0 + r,
                          sem.at[dst_slot]).start(priority=k % 2)

    def wait_rows(src_slot):
        for k in range(TOP_K):
            pltpu.make_async_copy(ys_ref.at[pl.ds(0, tt)], buf_ref.at[src_slot, k], sem.at[src_slot]).wait()

    @pl.when(i == 0)
    def _():
        lax.fori_loop(0, tt // ch, lambda c, z: (issue_rows(0, 0, c * ch), z)[1], 0)

    wait_rows(slot)
    nxt = lax.rem(i + 1, n_steps)

    def chunk(c, z):
        r0 = pl.multiple_of(c * ch, ch)
        issue_rows(nxt, 1 - slot, r0)
        rs = pl.ds(r0, ch)
        w = w_ref[rs, :]
        y = w[:, 0:1] * buf_ref[slot, 0, rs, :]
        for k in range(1, TOP_K):
            y = y + w[:, k:k + 1] * buf_ref[slot, k, rs, :]
        o_ref[rs, :] = _layer_norm(DEEPNORM_ALPHA * h_ref[rs, :] + y, g_ref[...], b_ref[...])
        return z

    lax.fori_loop(0, tt // ch, chunk, 0)

    @pl.when(i == n_steps - 1)
    def _():
        wait_rows(1 - slot)


def _combine(h1, pos_tiles, wts_t, ys, ln2_g, ln2_b, tt):
    n, d = h1.shape
    n_steps = n // tt
    grid_spec = pltpu.PrefetchScalarGridSpec(
        num_scalar_prefetch=1,
        grid=(n_steps,),
        in_specs=[pl.BlockSpec((tt, d), lambda i, pos: (i, 0)),
                  pl.BlockSpec((tt, TOP_K), lambda i, pos: (i, 0)),
                  pl.BlockSpec((1, d), lambda i, pos: (0, 0)),
                  pl.BlockSpec((1, d), lambda i, pos: (0, 0)),
                  pl.BlockSpec(memory_space=pl.ANY)],
        out_specs=pl.BlockSpec((tt, d), lambda i, pos: (i, 0)),
        scratch_shapes=[pltpu.VMEM((2, TOP_K, tt, d), F32), pltpu.SemaphoreType.DMA((2,))],
    )
    return pl.pallas_call(
        functools.partial(_combine_kernel, tt=tt, n_steps=n_steps),
        grid_spec=grid_spec,
        out_shape=jax.ShapeDtypeStruct((n, d), F32),
        compiler_params=_params(1),
        name="combine",
    )(pos_tiles, h1, wts_t, ln2_g, ln2_b, ys)


def _routing_tables(counts, eid, rank, n_tok):
    n_exp = counts.shape[0]
    e_ids = jnp.arange(n_exp, dtype=I32)
    nsub = (counts + SLOT_ROWS - 1) // SLOT_ROWS
    ngrp = (nsub + GROUP_SUBTILES - 1) // GROUP_SUBTILES
    base = nsub // jnp.maximum(ngrp, 1)
    rem = nsub - base * ngrp
    first_sub = jnp.cumsum(nsub) - nsub
    pos = jnp.sum(jnp.where(eid[..., None] == e_ids, first_sub * SLOT_ROWS, 0), axis=-1) + rank

    n_subtiles = (n_tok * TOP_K) // SLOT_ROWS + n_exp
    max_groups = (n_subtiles + GROUP_SUBTILES - 1) // GROUP_SUBTILES + n_exp
    grp_end = jnp.cumsum(ngrp)
    g_ids = jnp.arange(max_groups, dtype=I32)
    g_exp = jnp.minimum(jnp.sum((grp_end[None, :] <= g_ids[:, None]).astype(I32), axis=1), n_exp - 1)
    pick = lambda tab: jnp.sum(jnp.where(g_exp[:, None] == e_ids[None, :], tab[None, :], 0), axis=1)
    local = g_ids - pick(grp_end - ngrp)
    g_base, g_rem = pick(base), pick(rem)
    g_nsub = g_base + (local < g_rem).astype(I32)
    g_first = pick(first_sub) + local * g_base + jnp.minimum(local, g_rem)
    used = jnp.stack([grp_end[-1], jnp.sum(nsub)]).astype(I32)
    return dict(pos=pos, n_slots=n_subtiles * SLOT_ROWS, used=used,
                group_expert=g_exp.astype(I32), group_first=g_first.astype(I32), group_nsub=g_nsub.astype(I32),
                expert_last=(first_sub + nsub - 1).astype(I32), expert_nsub=nsub.astype(I32))


def _tiles(seq, n_tok):
    return dict(
        ln_rows=min(512, n_tok),
        conv_rows=min(1024, seq), conv_cols=512,
        pool_rows=min(1024, seq),
        merge_rows=min(1024, n_tok), merge_cols=512,
        mix_rows=min(512, n_tok),
        route_rows=min(128, n_tok),
        expert_cols=256,
    )


def kernel(x, meta_tokens, ln_in_g, ln_in_b, w_in, conv_w, w_a_out, pool_w, pool_scale, w_o, ln1_g, ln1_b,
           router_w, router_b, w_gate_up, b_gate_up, w_down, b_down, ln2_g, ln2_b):
    bsz, seq, d = x.shape
    assert w_in.shape[0] == DEPTH and meta_tokens.shape[0] == N_META == HALO
    d_conv = conv_w.shape[-1]
    n_groups, pool_cin, pool_cout = pool_w.shape[1:]
    d_pool = n_groups * pool_cin
    assert n_groups == len(POOL_WINDOWS) and n_groups * pool_cout == d
    n_exp = router_w.shape[-1]
    n_tok = bsz * seq
    t = _tiles(seq, n_tok)
    row = lambda v: v.reshape(1, -1).astype(F32)

    w_in_b = w_in[0].astype(BF16)
    x2 = x.reshape(n_tok, d)
    lig, lib = row(ln_in_g), row(ln_in_b)

    hx2 = _ln_cast(x2, lig, lib, t["ln_rows"])
    hm = _ln_cast(meta_tokens.astype(F32), lig, lib, N_META)
    hx = hx2.reshape(bsz, seq, d)

    a = _conv_branch(hx, hm, w_in_b, conv_w[0], d_conv, t["conv_rows"], min(t["conv_cols"], d_conv))
    yb = _pool_branch(hx, hm, w_in_b, pool_w[0].astype(BF16), row(pool_scale[0]), 3 * d_conv, t["pool_rows"])
    m2 = _merge(hx2, a.reshape(n_tok, d_conv), yb.reshape(n_tok, d), w_in_b, w_a_out[0].astype(BF16),
                3 * d_conv + d_pool, t["merge_rows"], min(t["merge_cols"], d))

    rw_t = router_w[0].T
    rw_hi = rw_t.astype(BF16)
    rw_lo = (rw_t - rw_hi.astype(F32)).astype(BF16)
    h1, h1p, eid, wts, rank, cnt = _mix_ln1(m2, x2, lig, lib, w_o[0].astype(BF16), row(ln1_g[0]), row(ln1_b[0]),
                                            rw_hi, rw_lo, router_b[0].reshape(n_exp, 1), t["mix_rows"])

    rt = _routing_tables(cnt[:, 0].astype(I32), eid, rank, n_tok)
    tt = t["route_rows"]
    pos_tiles = rt["pos"].reshape(TOP_K, n_tok // tt, tt).transpose(1, 0, 2).reshape(-1)

    xs = _dispatch(h1p, pos_tiles, rt["expert_last"], rt["expert_nsub"], rt["used"], rt["n_slots"], tt)
    ys = _experts(xs, rt["group_expert"], rt["group_first"], rt["group_nsub"], rt["used"],
                  w_gate_up[0], b_gate_up[0], w_down[0], b_down[0], min(t["expert_cols"], w_down.shape[-2]))
    out = _combine(h1, pos_tiles, wts.T, ys, row(ln2_g[0]), row(ln2_b[0]), tt)
    return out.reshape(bsz, seq, d)
```

```python
import functools

import jax
import jax.numpy as jnp
from jax import lax
from jax.experimental import pallas as pl
from jax.experimental.pallas import tpu as pltpu

F32 = jnp.float32
BF16 = jnp.bfloat16
I32 = jnp.int32
U32 = jnp.uint32

N_META = 16
CONV_K = 3
POOL_WINDOWS = (2, 4, 8, 16)
TOP_K = 4
SWIGLU_LIMIT = 7.0
SWIGLU_ALPHA = 1.702
LN_EPS = 1e-5
DEPTH = 1
DEEPNORM_ALPHA = (2.0 * DEPTH) ** 0.25

HALO = 16
POOL_PAD = 128
POOL_CHUNK = 256
SLOT_ROWS = 256
GROUP_SUBTILES = 9
CAST_SLICES = 8
MIX_HALF = 256
COMBINE_CHUNK = 32
V7X_VMEM_LIMIT = 58 * 1024 * 1024

_ARB = "arbitrary"


def _params(n_axes):
    return pltpu.CompilerParams(dimension_semantics=(_ARB,) * n_axes, vmem_limit_bytes=V7X_VMEM_LIMIT)


def _layer_norm(x, g, b):
    mu = jnp.mean(x, axis=-1, keepdims=True)
    xc = x - mu
    var = jnp.mean(xc * xc, axis=-1, keepdims=True)
    return xc * lax.rsqrt(var + LN_EPS) * g + b


def _dot(a, b):
    return jnp.dot(a, b, preferred_element_type=F32)


def _pack_pairs(lo_bf16, hi_bf16):
    lo = lax.bitcast_convert_type(lo_bf16.astype(F32), U32) >> 16
    hi = lax.bitcast_convert_type(hi_bf16.astype(F32), U32) & jnp.uint32(0xFFFF0000)
    return hi | lo


def _unpack_pairs(words):
    lo = lax.bitcast_convert_type(words << 16, F32).astype(BF16)
    hi = lax.bitcast_convert_type(words & jnp.uint32(0xFFFF0000), F32).astype(BF16)
    return lo, hi


def _ln_cast_kernel(x_ref, g_ref, b_ref, o_ref):
    o_ref[...] = _layer_norm(x_ref[...], g_ref[...], b_ref[...]).astype(o_ref.dtype)


def _ln_cast(x2d, g, b, tr):
    n, d = x2d.shape
    return pl.pallas_call(
        _ln_cast_kernel,
        grid=(n // tr,),
        in_specs=[pl.BlockSpec((tr, d), lambda i: (i, 0)),
                  pl.BlockSpec((1, d), lambda i: (0, 0)),
                  pl.BlockSpec((1, d), lambda i: (0, 0))],
        out_specs=pl.BlockSpec((tr, d), lambda i: (i, 0)),
        out_shape=jax.ShapeDtypeStruct((n, d), BF16),
        compiler_params=_params(1),
        name="ln_cast",
    )(x2d, g, b)


def _conv_kernel(h_ref, halo_ref, hm_ref, wb_ref, wc_ref, wu_ref, cw_ref, o_ref, he_ref, cu_ref, *, tm):
    i = pl.program_id(1)
    j = pl.program_id(2)

    @pl.when(j == 0)
    def _():
        he_ref[HALO:, :] = h_ref[0]

    @pl.when((j == 0) & (i == 0))
    def _():
        he_ref[0:HALO, :] = hm_ref[...]

    @pl.when((j == 0) & (i > 0))
    def _():
        he_ref[0:HALO, :] = halo_ref[0]

    he = he_ref[...]
    cu_ref[...] = _dot(he, wc_ref[...]) * _dot(he, wu_ref[...])
    bg = _dot(he_ref[HALO:, :], wb_ref[...])
    cw = cw_ref[...]
    y = cw[CONV_K - 1:CONV_K, :] * cu_ref[HALO:HALO + tm, :]
    for k in range(CONV_K - 1):
        s = HALO - (CONV_K - 1) + k
        y = y + cw[k:k + 1, :] * cu_ref[s:s + tm, :]
    o_ref[0] = (bg * y).astype(o_ref.dtype)


def _conv_branch(hx, hm, w_in_b, conv_w, d_conv, tm, tc):
    bsz, seq, d = hx.shape
    nc = d_conv // tc
    hb = tm // HALO
    return pl.pallas_call(
        functools.partial(_conv_kernel, tm=tm),
        grid=(bsz, seq // tm, nc),
        in_specs=[pl.BlockSpec((1, tm, d), lambda b, i, j: (b, i, 0)),
                  pl.BlockSpec((1, HALO, d), lambda b, i, j: (b, jnp.maximum(i * hb - 1, 0), 0)),
                  pl.BlockSpec((HALO, d), lambda b, i, j: (0, 0)),
                  pl.BlockSpec((d, tc), lambda b, i, j: (0, j)),
                  pl.BlockSpec((d, tc), lambda b, i, j: (0, nc + j)),
                  pl.BlockSpec((d, tc), lambda b, i, j: (0, 2 * nc + j)),
                  pl.BlockSpec((CONV_K, tc), lambda b, i, j: (0, j))],
        out_specs=pl.BlockSpec((1, tm, tc), lambda b, i, j: (b, i, j)),
        out_shape=jax.ShapeDtypeStruct((bsz, seq, d_conv), BF16),
        scratch_shapes=[pltpu.VMEM((tm + HALO, d), BF16), pltpu.VMEM((tm + HALO, tc), F32)],
        compiler_params=_params(3),
        name="conv_branch",
    )(hx, hx, hm, w_in_b, w_in_b, w_in_b, conv_w)


def _pool_kernel(h_ref, halo_ref, hm_ref, wv_ref, band_ref, pw_ref, ps_ref, o_ref, he_ref, vh_ref, vl_ref, *, tm):
    i = pl.program_id(1)
    lead = POOL_PAD - HALO
    he_ref[0:lead, :] = jnp.zeros((lead, he_ref.shape[1]), he_ref.dtype)
    he_ref[POOL_PAD:, :] = h_ref[0]

    @pl.when(i == 0)
    def _():
        he_ref[lead:POOL_PAD, :] = hm_ref[...]

    @pl.when(i > 0)
    def _():
        he_ref[lead:POOL_PAD, :] = halo_ref[0]

    v = _dot(he_ref[...], wv_ref[...])
    v_hi = v.astype(BF16)
    vh_ref[...] = v_hi
    vl_ref[...] = (v - v_hi.astype(F32)).astype(BF16)
    ng, cin, cout = pw_ref.shape
    rows = min(POOL_CHUNK, tm)
    for c in range(tm // rows):
        r0 = c * rows
        for g in range(ng):
            band = band_ref[g]
            win = (slice(r0, r0 + rows + POOL_PAD), slice(g * cin, (g + 1) * cin))
            pooled = _dot(band, vh_ref[win]) + _dot(band, vl_ref[win])
            y = _dot(pooled.astype(BF16), pw_ref[g]) * ps_ref[:, g * cout:(g + 1) * cout]
            o_ref[0, r0:r0 + rows, g * cout:(g + 1) * cout] = y.astype(o_ref.dtype)


def _pool_band(rows):
    t = jnp.arange(rows, dtype=I32)[:, None] + POOL_PAD
    s = jnp.arange(rows + POOL_PAD, dtype=I32)[None, :]
    bands = []
    for w in POOL_WINDOWS:
        inside = ((s <= t) & (s > t - w)).astype(F32) / w
        bands.append(inside - (s == t).astype(F32))
    return jnp.stack(bands).astype(BF16)


def _pool_branch(hx, hm, w_in_b, pool_w_b, pool_scale, col0, tm):
    bsz, seq, d = hx.shape
    ng, cin, cout = pool_w_b.shape
    d_pool = ng * cin
    hb = tm // HALO
    band = _pool_band(min(POOL_CHUNK, tm))
    return pl.pallas_call(
        functools.partial(_pool_kernel, tm=tm),
        grid=(bsz, seq // tm),
        in_specs=[pl.BlockSpec((1, tm, d), lambda b, i: (b, i, 0)),
                  pl.BlockSpec((1, HALO, d), lambda b, i: (b, jnp.maximum(i * hb - 1, 0), 0)),
                  pl.BlockSpec((HALO, d), lambda b, i: (0, 0)),
                  pl.BlockSpec((d, d_pool), lambda b, i: (0, col0 // d_pool)),
                  pl.BlockSpec(band.shape, lambda b, i: (0, 0, 0)),
                  pl.BlockSpec((ng, cin, cout), lambda b, i: (0, 0, 0)),
                  pl.BlockSpec((1, ng * cout), lambda b, i: (0, 0))],
        out_specs=pl.BlockSpec((1, tm, ng * cout), lambda b, i: (b, i, 0)),
        out_shape=jax.ShapeDtypeStruct((bsz, seq, ng * cout), BF16),
        scratch_shapes=[pltpu.VMEM((tm + POOL_PAD, d), BF16),
                        pltpu.VMEM((tm + POOL_PAD, d_pool), BF16),
                        pltpu.VMEM((tm + POOL_PAD, d_pool), BF16)],
        compiler_params=_params(2),
        name="pool_branch",
    )(hx, hx, hm, w_in_b, band, pool_w_b, pool_scale)


def _merge_kernel(h_ref, a_ref, yb_ref, wga_ref, wgb_ref, wao_ref, o_ref):
    h = h_ref[...]
    ga = jax.nn.sigmoid(_dot(h, wga_ref[...]))
    gb = jax.nn.sigmoid(_dot(h, wgb_ref[...]))
    ya = _dot(a_ref[...], wao_ref[...])
    o_ref[...] = (ga * ya + gb * yb_ref[...].astype(F32)).astype(o_ref.dtype)


def _merge(hx2, a2, yb2, w_in_b, w_a_out_b, col0, tm, tj):
    n, d = hx2.shape
    dc = a2.shape[1]
    nj = d // tj
    return pl.pallas_call(
        _merge_kernel,
        grid=(n // tm, nj),
        in_specs=[pl.BlockSpec((tm, d), lambda i, j: (i, 0)),
                  pl.BlockSpec((tm, dc), lambda i, j: (i, 0)),
                  pl.BlockSpec((tm, tj), lambda i, j: (i, j)),
                  pl.BlockSpec((d, tj), lambda i, j: (0, col0 // tj + j)),
                  pl.BlockSpec((d, tj), lambda i, j: (0, col0 // tj + nj + j)),
                  pl.BlockSpec((dc, tj), lambda i, j: (0, j))],
        out_specs=pl.BlockSpec((tm, tj), lambda i, j: (i, j)),
        out_shape=jax.ShapeDtypeStruct((n, d), BF16),
        compiler_params=_params(2),
        name="merge",
    )(hx2, a2, yb2, w_in_b, w_in_b, w_a_out_b)


def _mix_kernel(m_ref, x_ref, lig_ref, lib_ref, wo_ref, g1_ref, b1_ref, rwh_ref, rwl_ref, rb_ref, tri_ref,
                h1_ref, h1p_ref, eid_ref, wts_ref, rank_ref, cnt_ref, carry_ref, *, rows):
    step = pl.program_id(0)

    @pl.when(step == 0)
    def _():
        carry_ref[...] = jnp.zeros(carry_ref.shape, F32)

    for r0 in range(0, m_ref.shape[0], rows):
        rs = slice(r0, r0 + rows)
        h0 = _layer_norm(x_ref[rs, :], lig_ref[...], lib_ref[...])
        z = DEEPNORM_ALPHA * h0 + _dot(m_ref[rs, :], wo_ref[...])
        h1 = _layer_norm(z, g1_ref[...], b1_ref[...])
        h1_ref[rs, :] = h1
        h_hi = h1.astype(BF16)
        half = h1.shape[1] // 2
        h1p_ref[rs, :] = _pack_pairs(h_hi[:, :half], h_hi[:, half:])

        h_lo = (h1 - h_hi.astype(F32)).astype(BF16)
        nt = (((1,), (1,)), ((), ()))
        rwh = rwh_ref[...]
        logits = (lax.dot_general(rwh, h_hi, nt, preferred_element_type=F32)
                  + lax.dot_general(rwh, h_lo, nt, preferred_element_type=F32)
                  + lax.dot_general(rwl_ref[...], h_hi, nt, preferred_element_type=F32)
                  + rb_ref[...])
        n_exp = logits.shape[0]
        e_iota = lax.broadcasted_iota(I32, logits.shape, 0)
        work = logits
        vals, hots = [], []
        for k in range(TOP_K):
            m = jnp.max(work, axis=0, keepdims=True)
            idx = jnp.min(jnp.where(work == m, e_iota, n_exp), axis=0, keepdims=True)
            hot = e_iota == idx
            vals.append(m)
            hots.append(hot)
            eid_ref[k:k + 1, rs] = idx
            work = jnp.where(hot, -jnp.inf, work)
        exps = [jnp.exp(v - vals[0]) for v in vals]
        denom = exps[0]
        for e in exps[1:]:
            denom = denom + e
        for k in range(TOP_K):
            wts_ref[k:k + 1, rs] = exps[k] / denom

        multi = hots[0]
        for hot in hots[1:]:
            multi = multi | hot
        multi_f = jnp.where(multi, 1.0, 0.0).astype(F32)
        prefix = _dot(multi_f.astype(BF16), tri_ref[...]) + carry_ref[:, 0:1]
        for k in range(TOP_K):
            rank_ref[k:k + 1, rs] = jnp.sum(jnp.where(hots[k], prefix, 0.0), axis=0, keepdims=True).astype(I32)
        carry_ref[...] = carry_ref[...] + jnp.sum(multi_f, axis=1, keepdims=True)
    cnt_ref[...] = carry_ref[...]


def _mix_ln1(m2, x2, ln_in_g, ln_in_b, w_o_b, ln1_g, ln1_b, rw_hi, rw_lo, rb, tm):
    n, d = x2.shape
    n_exp = rw_hi.shape[0]
    rows = min(MIX_HALF, tm)
    tri = (jnp.arange(rows, dtype=I32)[:, None] < jnp.arange(rows, dtype=I32)[None, :]).astype(BF16)
    row = lambda i: (i, 0)
    fixed = lambda i: (0, 0)
    col = lambda i: (0, i)
    return pl.pallas_call(
        functools.partial(_mix_kernel, rows=rows),
        grid=(n // tm,),
        in_specs=[pl.BlockSpec((tm, d), row), pl.BlockSpec((tm, d), row),
                  pl.BlockSpec((1, d), fixed), pl.BlockSpec((1, d), fixed),
                  pl.BlockSpec((d, d), fixed),
                  pl.BlockSpec((1, d), fixed), pl.BlockSpec((1, d), fixed),
                  pl.BlockSpec((n_exp, d), fixed), pl.BlockSpec((n_exp, d), fixed),
                  pl.BlockSpec((n_exp, 1), fixed),
                  pl.BlockSpec((rows, rows), fixed)],
        out_specs=[pl.BlockSpec((tm, d), row), pl.BlockSpec((tm, d // 2), row),
                   pl.BlockSpec((TOP_K, tm), col), pl.BlockSpec((TOP_K, tm), col), pl.BlockSpec((TOP_K, tm), col),
                   pl.BlockSpec((n_exp, 128), fixed)],
        out_shape=[jax.ShapeDtypeStruct((n, d), F32),
                   jax.ShapeDtypeStruct((n, d // 2), U32),
                   jax.ShapeDtypeStruct((TOP_K, n), I32),
                   jax.ShapeDtypeStruct((TOP_K, n), F32),
                   jax.ShapeDtypeStruct((TOP_K, n), I32),
                   jax.ShapeDtypeStruct((n_exp, 128), F32)],
        scratch_shapes=[pltpu.VMEM((n_exp, 128), F32)],
        compiler_params=_params(1),
        name="mix_ln1",
    )(m2, x2, ln_in_g, ln_in_b, w_o_b, ln1_g, ln1_b, rw_hi, rw_lo, rb, tri)


def _row_copy(src_ref, src_row, dst_ref, dst_row, sem):
    return pltpu.make_async_copy(src_ref.at[pl.ds(src_row, 1)], dst_ref.at[pl.ds(dst_row, 1)], sem)


def _subtile(q):
    return pl.ds(pl.multiple_of(q * SLOT_ROWS, SLOT_ROWS), SLOT_ROWS)


def _dispatch_kernel(pos_ref, gl_ref, gn_ref, used_ref, h_ref, xs_ref, zero_ref, sem, zsem, *, tt):
    base = pl.program_id(0) * (TOP_K * tt)

    @pl.when(pl.program_id(0) == 0)
    def _():
        zero_ref[...] = jnp.zeros(zero_ref.shape, zero_ref.dtype)
        n_exp = gl_ref.shape[0]
        n_sub = xs_ref.shape[0] // SLOT_ROWS

        def zero_copy(q):
            return pltpu.make_async_copy(zero_ref, xs_ref.at[_subtile(q)], zsem)

        def group_tail(e, start):
            @pl.when(gn_ref[e] > 0)
            def _():
                c = zero_copy(gl_ref[e])
                c.start() if start else c.wait()

        def unused(q, start):
            c = zero_copy(q)
            c.start() if start else c.wait()

        for start in (True, False):
            lax.fori_loop(0, n_exp, lambda e, c: (group_tail(e, start), c)[1], 0)
            lax.fori_loop(used_ref[1], n_sub, lambda q, c: (unused(q, start), c)[1], 0)

    def issue(t, c):
        for k in range(TOP_K):
            _row_copy(h_ref, t, xs_ref, pos_ref[base + k * tt + t], sem).start(priority=k % 2)
        return c

    lax.fori_loop(0, tt, issue, 0)
    for k in range(TOP_K):
        pltpu.make_async_copy(h_ref, xs_ref.at[pl.ds(0, tt)], sem).wait()


def _dispatch(h1p, pos_tiles, group_last, group_nsub, used, n_slots, tt):
    n, dw = h1p.shape
    grid_spec = pltpu.PrefetchScalarGridSpec(
        num_scalar_prefetch=4,
        grid=(n // tt,),
        in_specs=[pl.BlockSpec((tt, dw), lambda i, *_: (i, 0))],
        out_specs=pl.BlockSpec(memory_space=pl.ANY),
        scratch_shapes=[pltpu.VMEM((SLOT_ROWS, dw), h1p.dtype),
                        pltpu.SemaphoreType.DMA(()), pltpu.SemaphoreType.DMA(())],
    )
    return pl.pallas_call(
        functools.partial(_dispatch_kernel, tt=tt),
        grid_spec=grid_spec,
        out_shape=jax.ShapeDtypeStruct((n_slots, dw), h1p.dtype),
        compiler_params=_params(1),
        name="dispatch",
    )(pos_tiles, group_last, group_nsub, used, h1p)


def _expert_kernel(ge_ref, gs_ref, gn_ref, nu_ref, xs_ref, wg_ref, wl_ref, wd_ref, bg_ref, bl_ref, bd_ref,
                   ys_ref, xb_ref, acc_ref, stage_ref, wgb_ref, wlb_ref, wdb_ref, sem_in, sem_out, *, nj):
    t = pl.program_id(0)
    n_groups = nu_ref[0]
    n_items = n_groups * nj
    c = jnp.maximum(t - 1, 0)
    s = c // nj
    j = c - s * nj
    cast_slot = t % 2
    use_slot = 1 - cast_slot
    d, tn = wgb_ref.shape[1:]
    half = d // 2
    rows = _subtile

    def cast_slice(i):
        r = pl.ds(pl.multiple_of(i * (d // CAST_SLICES), d // CAST_SLICES), d // CAST_SLICES)
        wgb_ref[cast_slot, r, :] = wg_ref[0, r, :].astype(BF16)
        wlb_ref[cast_slot, r, :] = wl_ref[0, r, :].astype(BF16)
        r = pl.ds(pl.multiple_of(i * (tn // CAST_SLICES), tn // CAST_SLICES), tn // CAST_SLICES)
        wdb_ref[cast_slot, r, :] = wd_ref[0, r, :].astype(BF16)

    def cast_range(lo, hi):
        lax.fori_loop(lo, hi, lambda i, z: (cast_slice(i), z)[1], 0)

    def out_copy(first_sub, q):
        return pltpu.make_async_copy(acc_ref.at[rows(q)], ys_ref.at[rows(first_sub + q)], sem_out.at[q])

    def in_copy(first_sub, q, slot):
        return pltpu.make_async_copy(xs_ref.at[rows(first_sub + q)], stage_ref.at[slot], sem_in.at[slot])

    @pl.when(t == 0)
    def _():
        cast_range(0, CAST_SLICES)

    @pl.when((t >= 1) & (t <= n_items))
    def _():
        first_sub = gs_ref[s]
        nsub = gn_ref[s]

        @pl.when(j == 0)
        def _():
            @pl.when(s > 0)
            def _():
                prev_first = gs_ref[s - 1]
                lax.fori_loop(0, gn_ref[s - 1], lambda q, z: (out_copy(prev_first, q).wait(), z)[1], 0)

            in_copy(first_sub, 0, 0).start()

            def load(q, z):
                slot = q % 2
                in_copy(first_sub, q, slot).wait()

                @pl.when(q + 1 < nsub)
                def _():
                    in_copy(first_sub, q + 1, 1 - slot).start()

                lo, hi = _unpack_pairs(stage_ref[slot])
                xb_ref[rows(q), 0:half] = lo
                xb_ref[rows(q), half:] = hi
                acc_ref[rows(q), :] = jnp.broadcast_to(bd_ref[0], (SLOT_ROWS, d))
                return z

            lax.fori_loop(0, nsub, load, 0)

        def run(q, n):
            r = pl.ds(pl.multiple_of(q * SLOT_ROWS, SLOT_ROWS), n * SLOT_ROWS)
            x = xb_ref[r, :]
            gate = _dot(x, wgb_ref[use_slot]) + bg_ref[0]
            lin = _dot(x, wlb_ref[use_slot]) + bl_ref[0]
            gate = jnp.minimum(gate, SWIGLU_LIMIT)
            lin = jnp.clip(lin, -SWIGLU_LIMIT, SWIGLU_LIMIT)
            act = (lin + 1.0) * gate * jax.nn.sigmoid(SWIGLU_ALPHA * gate)
            acc_ref[r, :] += _dot(act.astype(BF16), wdb_ref[use_slot])

        a = nsub // 3
        r = nsub - 3 * a
        n_triples = jnp.where((r == 1) & (a >= 1), a - 1, a)
        lax.fori_loop(0, n_triples, lambda p, z: (run(3 * p, 3), z)[1], 0)

        @pl.when((r == 1) & (a >= 1))
        def _():
            run(nsub - 4, 4)

        @pl.when(r == 2)
        def _():
            run(nsub - 2, 2)

        @pl.when(nsub == 1)
        def _():
            run(0, 1)

        cast_range(0, CAST_SLICES)

        @pl.when(j == nj - 1)
        def _():
            lax.fori_loop(0, nsub, lambda q, z: (out_copy(first_sub, q).start(), z)[1], 0)

        @pl.when(t == n_items)
        def _():
            lax.fori_loop(0, nsub, lambda q, z: (out_copy(first_sub, q).wait(), z)[1], 0)
            acc_ref[rows(0), :] = jnp.zeros((SLOT_ROWS, d), F32)
            n_sub = ys_ref.shape[0] // SLOT_ROWS

            def tail_copy(q):
                return pltpu.make_async_copy(acc_ref.at[rows(0)], ys_ref.at[rows(q)], sem_out.at[0])

            lax.fori_loop(nu_ref[1], n_sub, lambda q, z: (tail_copy(q).start(), z)[1], 0)
            lax.fori_loop(nu_ref[1], n_sub, lambda q, z: (tail_copy(q).wait(), z)[1], 0)


def _experts(xs, group_expert, group_first, group_nsub, used, w_gate_up, b_gate_up, w_down, b_down, tn):
    n_slots, dw = xs.shape
    d = 2 * dw
    n_exp, _, de2 = w_gate_up.shape
    de = de2 // 2
    nj = de // tn
    max_items = group_expert.shape[0] * nj
    rmax = GROUP_SUBTILES * SLOT_ROWS

    def fetched(t, nu):
        w = jnp.minimum(t, nu[0] * nj - 1)
        return w // nj, w % nj

    def computed(t, nu):
        w = jnp.clip(t - 1, 0, nu[0] * nj - 1)
        return w // nj, w % nj

    def wg_map(t, ge, gs, gn, nu):
        g, j = fetched(t, nu)
        return ge[g], 0, j

    def wl_map(t, ge, gs, gn, nu):
        g, j = fetched(t, nu)
        return ge[g], 0, nj + j

    def wd_map(t, ge, gs, gn, nu):
        g, j = fetched(t, nu)
        return ge[g], j, 0

    def bg_map(t, ge, gs, gn, nu):
        g, j = computed(t, nu)
        return ge[g], 0, j

    def bl_map(t, ge, gs, gn, nu):
        g, j = computed(t, nu)
        return ge[g], 0, nj + j

    def bd_map(t, ge, gs, gn, nu):
        g, _ = computed(t, nu)
        return ge[g], 0, 0

    grid_spec = pltpu.PrefetchScalarGridSpec(
        num_scalar_prefetch=4,
        grid=(max_items + 1,),
        in_specs=[pl.BlockSpec(memory_space=pl.ANY),
                  pl.BlockSpec((1, d, tn), wg_map), pl.BlockSpec((1, d, tn), wl_map),
                  pl.BlockSpec((1, tn, d), wd_map),
                  pl.BlockSpec((1, 1, tn), bg_map), pl.BlockSpec((1, 1, tn), bl_map),
                  pl.BlockSpec((1, 1, d), bd_map)],
        out_specs=pl.BlockSpec(memory_space=pl.ANY),
        scratch_shapes=[pltpu.VMEM((rmax, d), BF16),
                        pltpu.VMEM((rmax, d), F32),
                        pltpu.VMEM((2, SLOT_ROWS, dw), U32),
                        pltpu.VMEM((2, d, tn), BF16), pltpu.VMEM((2, d, tn), BF16), pltpu.VMEM((2, tn, d), BF16),
                        pltpu.SemaphoreType.DMA((2,)),
                        pltpu.SemaphoreType.DMA((GROUP_SUBTILES,))],
    )
    return pl.pallas_call(
        functools.partial(_expert_kernel, nj=nj),
        grid_spec=grid_spec,
        out_shape=jax.ShapeDtypeStruct((n_slots, d), F32),
        compiler_params=_params(1),
        name="experts",
    )(group_expert, group_first, group_nsub, used, xs, w_gate_up, w_gate_up, w_down,
      b_gate_up.reshape(n_exp, 1, de2), b_gate_up.reshape(n_exp, 1, de2), b_down.reshape(n_exp, 1, d))


def _combine_kernel(pos_ref, h_ref, w_ref, g_ref, b_ref, ys_ref, o_ref, buf_ref, sem, *, tt, n_steps):
    i = pl.program_id(0)
    slot = i % 2
    ch = min(COMBINE_CHUNK, tt)

    def issue_rows(step, dst_slot, t0):
        base = step * (TOP_K * tt)
        for r in range(ch):
            for k in range(TOP_K):
                _row_copy(ys_ref, pos_ref[base + k * tt + t0 + r], buf_ref.at[dst_slot, k], t0 + r,
                          sem.at[dst_slot]).start(priority=k % 2)

    def wait_rows(src_slot):
        for k in range(TOP_K):
            pltpu.make_async_copy(ys_ref.at[pl.ds(0, tt)], buf_ref.at[src_slot, k], sem.at[src_slot]).wait()

    @pl.when(i == 0)
    def _():
        lax.fori_loop(0, tt // ch, lambda c, z: (issue_rows(0, 0, c * ch), z)[1], 0)

    wait_rows(slot)
    nxt = lax.rem(i + 1, n_steps)

    def chunk(c, z):
        r0 = pl.multiple_of(c * ch, ch)
        issue_rows(nxt, 1 - slot, r0)
        rs = pl.ds(r0, ch)
        w = w_ref[rs, :]
        y = w[:, 0:1] * buf_ref[slot, 0, rs, :]
        for k in range(1, TOP_K):
            y = y + w[:, k:k + 1] * buf_ref[slot, k, rs, :]
        o_ref[rs, :] = _layer_norm(DEEPNORM_ALPHA * h_ref[rs, :] + y, g_ref[...], b_ref[...])
        return z

    lax.fori_loop(0, tt // ch, chunk, 0)

    @pl.when(i == n_steps - 1)
    def _():
        wait_rows(1 - slot)


def _combine(h1, pos_tiles, wts_t, ys, ln2_g, ln2_b, tt):
    n, d = h1.shape
    n_steps = n // tt
    grid_spec = pltpu.PrefetchScalarGridSpec(
        num_scalar_prefetch=1,
        grid=(n_steps,),
        in_specs=[pl.BlockSpec((tt, d), lambda i, pos: (i, 0)),
                  pl.BlockSpec((tt, TOP_K), lambda i, pos: (i, 0)),
                  pl.BlockSpec((1, d), lambda i, pos: (0, 0)),
                  pl.BlockSpec((1, d), lambda i, pos: (0, 0)),
                  pl.BlockSpec(memory_space=pl.ANY)],
        out_specs=pl.BlockSpec((tt, d), lambda i, pos: (i, 0)),
        scratch_shapes=[pltpu.VMEM((2, TOP_K, tt, d), F32), pltpu.SemaphoreType.DMA((2,))],
    )
    return pl.pallas_call(
        functools.partial(_combine_kernel, tt=tt, n_steps=n_steps),
        grid_spec=grid_spec,
        out_shape=jax.ShapeDtypeStruct((n, d), F32),
        compiler_params=_params(1),
        name="combine",
    )(pos_tiles, h1, wts_t, ln2_g, ln2_b, ys)


def _routing_tables(counts, eid, rank, n_tok):
    n_exp = counts.shape[0]
    e_ids = jnp.arange(n_exp, dtype=I32)
    nsub = (counts + SLOT_ROWS - 1) // SLOT_ROWS
    ngrp = (nsub + GROUP_SUBTILES - 1) // GROUP_SUBTILES
    base = nsub // jnp.maximum(ngrp, 1)
    rem = nsub - base * ngrp
    first_sub = jnp.cumsum(nsub) - nsub
    pos = jnp.sum(jnp.where(eid[..., None] == e_ids, first_sub * SLOT_ROWS, 0), axis=-1) + rank

    n_subtiles = (n_tok * TOP_K) // SLOT_ROWS + n_exp
    max_groups = (n_subtiles + GROUP_SUBTILES - 1) // GROUP_SUBTILES + n_exp
    grp_end = jnp.cumsum(ngrp)
    g_ids = jnp.arange(max_groups, dtype=I32)
    g_exp = jnp.minimum(jnp.sum((grp_end[None, :] <= g_ids[:, None]).astype(I32), axis=1), n_exp - 1)
    pick = lambda tab: jnp.sum(jnp.where(g_exp[:, None] == e_ids[None, :], tab[None, :], 0), axis=1)
    local = g_ids - pick(grp_end - ngrp)
    g_base, g_rem = pick(base), pick(rem)
    g_nsub = g_base + (local < g_rem).astype(I32)
    g_first = pick(first_sub) + local * g_base + jnp.minimum(local, g_rem)
    used = jnp.stack([grp_end[-1], jnp.sum(nsub)]).astype(I32)
    return dict(pos=pos, n_slots=n_subtiles * SLOT_ROWS, used=used,
                group_expert=g_exp.astype(I32), group_first=g_first.astype(I32), group_nsub=g_nsub.astype(I32),
                expert_last=(first_sub + nsub - 1).astype(I32), expert_nsub=nsub.astype(I32))


def _tiles(seq, n_tok):
    return dict(
        ln_rows=min(512, n_tok),
        conv_rows=min(1024, seq), conv_cols=512,
        pool_rows=min(1024, seq),
        merge_rows=min(1024, n_tok), merge_cols=512,
        mix_rows=min(512, n_tok),
        route_rows=min(128, n_tok),
        expert_cols=256,
    )


def kernel(x, meta_tokens, ln_in_g, ln_in_b, w_in, conv_w, w_a_out, pool_w, pool_scale, w_o, ln1_g, ln1_b,
           router_w, router_b, w_gate_up, b_gate_up, w_down, b_down, ln2_g, ln2_b):
    bsz, seq, d = x.shape
    assert w_in.shape[0] == DEPTH and meta_tokens.shape[0] == N_META == HALO
    d_conv = conv_w.shape[-1]
    n_groups, pool_cin, pool_cout = pool_w.shape[1:]
    d_pool = n_groups * pool_cin
    assert n_groups == len(POOL_WINDOWS) and n_groups * pool_cout == d
    n_exp = router_w.shape[-1]
    n_tok = bsz * seq
    t = _tiles(seq, n_tok)
    row = lambda v: v.reshape(1, -1).astype(F32)

    w_in_b = w_in[0].astype(BF16)
    x2 = x.reshape(n_tok, d)
    lig, lib = row(ln_in_g), row(ln_in_b)

    hx2 = _ln_cast(x2, lig, lib, t["ln_rows"])
    hm = _ln_cast(meta_tokens.astype(F32), lig, lib, N_META)
    hx = hx2.reshape(bsz, seq, d)

    a = _conv_branch(hx, hm, w_in_b, conv_w[0], d_conv, t["conv_rows"], min(t["conv_cols"], d_conv))
    yb = _pool_branch(hx, hm, w_in_b, pool_w[0].astype(BF16), row(pool_scale[0]), 3 * d_conv, t["pool_rows"])
    m2 = _merge(hx2, a.reshape(n_tok, d_conv), yb.reshape(n_tok, d), w_in_b, w_a_out[0].astype(BF16),
                3 * d_conv + d_pool, t["merge_rows"], min(t["merge_cols"], d))

    rw_t = router_w[0].T
    rw_hi = rw_t.astype(BF16)
    rw_lo = (rw_t - rw_hi.astype(F32)).astype(BF16)
    h1, h1p, eid, wts, rank, cnt = _mix_ln1(m2, x2, lig, lib, w_o[0].astype(BF16), row(ln1_g[0]), row(ln1_b[0]),
                                            rw_hi, rw_lo, router_b[0].reshape(n_exp, 1), t["mix_rows"])

    rt = _routing_tables(cnt[:, 0].astype(I32), eid, rank, n_tok)
    tt = t["route_rows"]
    pos_tiles = rt["pos"].reshape(TOP_K, n_tok // tt, tt).transpose(1, 0, 2).reshape(-1)

    xs = _dispatch(h1p, pos_tiles, rt["expert_last"], rt["expert_nsub"], rt["used"], rt["n_slots"], tt)
    ys = _experts(xs, rt["group_expert"], rt["group_first"], rt["group_nsub"], rt["used"],
                  w_gate_up[0], b_gate_up[0], w_down[0], b_down[0], min(t["expert_cols"], w_down.shape[-2]))
    out = _combine(h1, pos_tiles, wts.T, ys, row(ln2_g[0]), row(ln2_b[0]), tt)
    return out.reshape(bsz, seq, d)
```

```python
import functools

import jax
import jax.numpy as jnp
from jax import lax
from jax.experimental import pallas as pl
from jax.experimental.pallas import tpu as pltpu

F32 = jnp.float32
BF16 = jnp.bfloat16
I32 = jnp.int32
U32 = jnp.uint32

N_META = 16
CONV_K = 3
POOL_WINDOWS = (2, 4, 8, 16)
TOP_K = 4
SWIGLU_LIMIT = 7.0
SWIGLU_ALPHA = 1.702
LN_EPS = 1e-5
DEPTH = 1
DEEPNORM_ALPHA = (2.0 * DEPTH) ** 0.25

HALO = 16
POOL_PAD = 128
POOL_CHUNK = 256
SLOT_ROWS = 256
GROUP_SUBTILES = 9
CAST_SLICES = 8
MIX_HALF = 256
COMBINE_CHUNK = 32
V7X_VMEM_LIMIT = 58 * 1024 * 1024

_ARB = "arbitrary"


def _params(n_axes):
    return pltpu.CompilerParams(dimension_semantics=(_ARB,) * n_axes, vmem_limit_bytes=V7X_VMEM_LIMIT)


def _layer_norm(x, g, b):
    mu = jnp.mean(x, axis=-1, keepdims=True)
    xc = x - mu
    var = jnp.mean(xc * xc, axis=-1, keepdims=True)
    return xc * lax.rsqrt(var + LN_EPS) * g + b


def _dot(a, b):
    return jnp.dot(a, b, preferred_element_type=F32)


def _pack_pairs(lo_bf16, hi_bf16):
    lo = lax.bitcast_convert_type(lo_bf16.astype(F32), U32) >> 16
    hi = lax.bitcast_convert_type(hi_bf16.astype(F32), U32) & jnp.uint32(0xFFFF0000)
    return hi | lo


def _unpack_pairs(words):
    lo = lax.bitcast_convert_type(words << 16, F32).astype(BF16)
    hi = lax.bitcast_convert_type(words & jnp.uint32(0xFFFF0000), F32).astype(BF16)
    return lo, hi


def _ln_cast_kernel(x_ref, g_ref, b_ref, o_ref):
    o_ref[...] = _layer_norm(x_ref[...], g_ref[...], b_ref[...]).astype(o_ref.dtype)


def _ln_cast(x2d, g, b, tr):
    n, d = x2d.shape
    return pl.pallas_call(
        _ln_cast_kernel,
        grid=(n // tr,),
        in_specs=[pl.BlockSpec((tr, d), lambda i: (i, 0)),
                  pl.BlockSpec((1, d), lambda i: (0, 0)),
                  pl.BlockSpec((1, d), lambda i: (0, 0))],
        out_specs=pl.BlockSpec((tr, d), lambda i: (i, 0)),
        out_shape=jax.ShapeDtypeStruct((n, d), BF16),
        compiler_params=_params(1),
        name="ln_cast",
    )(x2d, g, b)


def _conv_kernel(h_ref, halo_ref, hm_ref, wb_ref, wc_ref, wu_ref, cw_ref, o_ref, he_ref, cu_ref, *, tm):
    i = pl.program_id(1)
    j = pl.program_id(2)

    @pl.when(j == 0)
    def _():
        he_ref[HALO:, :] = h_ref[0]

    @pl.when((j == 0) & (i == 0))
    def _():
        he_ref[0:HALO, :] = hm_ref[...]

    @pl.when((j == 0) & (i > 0))
    def _():
        he_ref[0:HALO, :] = halo_ref[0]

    he = he_ref[...]
    cu_ref[...] = _dot(he, wc_ref[...]) * _dot(he, wu_ref[...])
    bg = _dot(he_ref[HALO:, :], wb_ref[...])
    cw = cw_ref[...]
    y = cw[CONV_K - 1:CONV_K, :] * cu_ref[HALO:HALO + tm, :]
    for k in range(CONV_K - 1):
        s = HALO - (CONV_K - 1) + k
        y = y + cw[k:k + 1, :] * cu_ref[s:s + tm, :]
    o_ref[0] = (bg * y).astype(o_ref.dtype)


def _conv_branch(hx, hm, w_in_b, conv_w, d_conv, tm, tc):
    bsz, seq, d = hx.shape
    nc = d_conv // tc
    hb = tm // HALO
    return pl.pallas_call(
        functools.partial(_conv_kernel, tm=tm),
        grid=(bsz, seq // tm, nc),
        in_specs=[pl.BlockSpec((1, tm, d), lambda b, i, j: (b, i, 0)),
                  pl.BlockSpec((1, HALO, d), lambda b, i, j: (b, jnp.maximum(i * hb - 1, 0), 0)),
                  pl.BlockSpec((HALO, d), lambda b, i, j: (0, 0)),
                  pl.BlockSpec((d, tc), lambda b, i, j: (0, j)),
                  pl.BlockSpec((d, tc), lambda b, i, j: (0, nc + j)),
                  pl.BlockSpec((d, tc), lambda b, i, j: (0, 2 * nc + j)),
                  pl.BlockSpec((CONV_K, tc), lambda b, i, j: (0, j))],
        out_specs=pl.BlockSpec((1, tm, tc), lambda b, i, j: (b, i, j)),
        out_shape=jax.ShapeDtypeStruct((bsz, seq, d_conv), BF16),
        scratch_shapes=[pltpu.VMEM((tm + HALO, d), BF16), pltpu.VMEM((tm + HALO, tc), F32)],
        compiler_params=_params(3),
        name="conv_branch",
    )(hx, hx, hm, w_in_b, w_in_b, w_in_b, conv_w)


def _pool_kernel(h_ref, halo_ref, hm_ref, wv_ref, band_ref, pw_ref, ps_ref, o_ref, he_ref, vh_ref, vl_ref, *, tm):
    i = pl.program_id(1)
    lead = POOL_PAD - HALO
    he_ref[0:lead, :] = jnp.zeros((lead, he_ref.shape[1]), he_ref.dtype)
    he_ref[POOL_PAD:, :] = h_ref[0]

    @pl.when(i == 0)
    def _():
        he_ref[lead:POOL_PAD, :] = hm_ref[...]

    @pl.when(i > 0)
    def _():
        he_ref[lead:POOL_PAD, :] = halo_ref[0]

    v = _dot(he_ref[...], wv_ref[...])
    v_hi = v.astype(BF16)
    vh_ref[...] = v_hi
    vl_ref[...] = (v - v_hi.astype(F32)).astype(BF16)
    ng, cin, cout = pw_ref.shape
    rows = min(POOL_CHUNK, tm)
    for c in range(tm // rows):
        r0 = c * rows
        for g in range(ng):
            band = band_ref[g]
            win = (slice(r0, r0 + rows + POOL_PAD), slice(g * cin, (g + 1) * cin))
            pooled = _dot(band, vh_ref[win]) + _dot(band, vl_ref[win])
            y = _dot(pooled.astype(BF16), pw_ref[g]) * ps_ref[:, g * cout:(g + 1) * cout]
            o_ref[0, r0:r0 + rows, g * cout:(g + 1) * cout] = y.astype(o_ref.dtype)


def _pool_band(rows):
    t = jnp.arange(rows, dtype=I32)[:, None] + POOL_PAD
    s = jnp.arange(rows + POOL_PAD, dtype=I32)[None, :]
    bands = []
    for w in POOL_WINDOWS:
        inside = ((s <= t) & (s > t - w)).astype(F32) / w
        bands.append(inside - (s == t).astype(F32))
    return jnp.stack(bands).astype(BF16)


def _pool_branch(hx, hm, w_in_b, pool_w_b, pool_scale, col0, tm):
    bsz, seq, d = hx.shape
    ng, cin, cout = pool_w_b.shape
    d_pool = ng * cin
    hb = tm // HALO
    band = _pool_band(min(POOL_CHUNK, tm))
    return pl.pallas_call(
        functools.partial(_pool_kernel, tm=tm),
        grid=(bsz, seq // tm),
        in_specs=[pl.BlockSpec((1, tm, d), lambda b, i: (b, i, 0)),
                  pl.BlockSpec((1, HALO, d), lambda b, i: (b, jnp.maximum(i * hb - 1, 0), 0)),
                  pl.BlockSpec((HALO, d), lambda b, i: (0, 0)),
                  pl.BlockSpec((d, d_pool), lambda b, i: (0, col0 // d_pool)),
                  pl.BlockSpec(band.shape, lambda b, i: (0, 0, 0)),
                  pl.BlockSpec((ng, cin, cout), lambda b, i: (0, 0, 0)),
                  pl.BlockSpec((1, ng * cout), lambda b, i: (0, 0))],
        out_specs=pl.BlockSpec((1, tm, ng * cout), lambda b, i: (b, i, 0)),
        out_shape=jax.ShapeDtypeStruct((bsz, seq, ng * cout), BF16),
        scratch_shapes=[pltpu.VMEM((tm + POOL_PAD, d), BF16),
                        pltpu.VMEM((tm + POOL_PAD, d_pool), BF16),
                        pltpu.VMEM((tm + POOL_PAD, d_pool), BF16)],
        compiler_params=_params(2),
        name="pool_branch",
    )(hx, hx, hm, w_in_b, band, pool_w_b, pool_scale)


def _merge_kernel(h_ref, a_ref, yb_ref, wga_ref, wgb_ref, wao_ref, o_ref):
    h = h_ref[...]
    ga = jax.nn.sigmoid(_dot(h, wga_ref[...]))
    gb = jax.nn.sigmoid(_dot(h, wgb_ref[...]))
    ya = _dot(a_ref[...], wao_ref[...])
    o_ref[...] = (ga * ya + gb * yb_ref[...].astype(F32)).astype(o_ref.dtype)


def _merge(hx2, a2, yb2, w_in_b, w_a_out_b, col0, tm, tj):
    n, d = hx2.shape
    dc = a2.shape[1]
    nj = d // tj
    return pl.pallas_call(
        _merge_kernel,
        grid=(n // tm, nj),
        in_specs=[pl.BlockSpec((tm, d), lambda i, j: (i, 0)),
                  pl.BlockSpec((tm, dc), lambda i, j: (i, 0)),
                  pl.BlockSpec((tm, tj), lambda i, j: (i, j)),
                  pl.BlockSpec((d, tj), lambda i, j: (0, col0 // tj + j)),
                  pl.BlockSpec((d, tj), lambda i, j: (0, col0 // tj + nj + j)),
                  pl.BlockSpec((dc, tj), lambda i, j: (0, j))],
        out_specs=pl.BlockSpec((tm, tj), lambda i, j: (i, j)),
        out_shape=jax.ShapeDtypeStruct((n, d), BF16),
        compiler_params=_params(2),
        name="merge",
    )(hx2, a2, yb2, w_in_b, w_in_b, w_a_out_b)


def _mix_kernel(m_ref, x_ref, lig_ref, lib_ref, wo_ref, g1_ref, b1_ref, rwh_ref, rwl_ref, rb_ref, tri_ref,
                h1_ref, h1p_ref, eid_ref, wts_ref, rank_ref, cnt_ref, carry_ref, z_ref, *, rows):
    step = pl.program_id(0)

    @pl.when(step == 0)
    def _():
        carry_ref[...] = jnp.zeros(carry_ref.shape, F32)

    for r0 in range(0, m_ref.shape[0], rows):
        rs = slice(r0, r0 + rows)
        h0 = _layer_norm(x_ref[rs, :], lig_ref[...], lib_ref[...])
        z_ref[rs, :] = DEEPNORM_ALPHA * h0 + _dot(m_ref[rs, :], wo_ref[...])

    for r0 in range(0, m_ref.shape[0], rows):
        rs = slice(r0, r0 + rows)
        h1 = _layer_norm(z_ref[rs, :], g1_ref[...], b1_ref[...])
        h1_ref[rs, :] = h1
        h_hi = h1.astype(BF16)
        half = h1.shape[1] // 2
        h1p_ref[rs, :] = _pack_pairs(h_hi[:, :half], h_hi[:, half:])

        h_lo = (h1 - h_hi.astype(F32)).astype(BF16)
        nt = (((1,), (1,)), ((), ()))
        rwh = rwh_ref[...]
        logits = (lax.dot_general(rwh, h_hi, nt, preferred_element_type=F32)
                  + lax.dot_general(rwh, h_lo, nt, preferred_element_type=F32)
                  + lax.dot_general(rwl_ref[...], h_hi, nt, preferred_element_type=F32)
                  + rb_ref[...])
        n_exp = logits.shape[0]
        e_iota = lax.broadcasted_iota(I32, logits.shape, 0)
        work = logits
        vals, hots = [], []
        for k in range(TOP_K):
            m = jnp.max(work, axis=0, keepdims=True)
            idx = jnp.min(jnp.where(work == m, e_iota, n_exp), axis=0, keepdims=True)
            hot = e_iota == idx
            vals.append(m)
            hots.append(hot)
            eid_ref[k:k + 1, rs] = idx
            work = jnp.where(hot, -jnp.inf, work)
        exps = [jnp.exp(v - vals[0]) for v in vals]
        denom = exps[0]
        for e in exps[1:]:
            denom = denom + e
        for k in range(TOP_K):
            wts_ref[k:k + 1, rs] = exps[k] / denom

        multi = hots[0]
        for hot in hots[1:]:
            multi = multi | hot
        multi_f = jnp.where(multi, 1.0, 0.0).astype(F32)
        prefix = _dot(multi_f.astype(BF16), tri_ref[...]) + carry_ref[:, 0:1]
        for k in range(TOP_K):
            rank_ref[k:k + 1, rs] = jnp.sum(jnp.where(hots[k], prefix, 0.0), axis=0, keepdims=True).astype(I32)
        carry_ref[...] = carry_ref[...] + jnp.sum(multi_f, axis=1, keepdims=True)
    cnt_ref[...] = carry_ref[...]


def _mix_ln1(m2, x2, ln_in_g, ln_in_b, w_o_b, ln1_g, ln1_b, rw_hi, rw_lo, rb, tm):
    n, d = x2.shape
    n_exp = rw_hi.shape[0]
    rows = min(MIX_HALF, tm)
    tri = (jnp.arange(rows, dtype=I32)[:, None] < jnp.arange(rows, dtype=I32)[None, :]).astype(BF16)
    row = lambda i: (i, 0)
    fixed = lambda i: (0, 0)
    col = lambda i: (0, i)
    return pl.pallas_call(
        functools.partial(_mix_kernel, rows=rows),
        grid=(n // tm,),
        in_specs=[pl.BlockSpec((tm, d), row), pl.BlockSpec((tm, d), row),
                  pl.BlockSpec((1, d), fixed), pl.BlockSpec((1, d), fixed),
                  pl.BlockSpec((d, d), fixed),
                  pl.BlockSpec((1, d), fixed), pl.BlockSpec((1, d), fixed),
                  pl.BlockSpec((n_exp, d), fixed), pl.BlockSpec((n_exp, d), fixed),
                  pl.BlockSpec((n_exp, 1), fixed),
                  pl.BlockSpec((rows, rows), fixed)],
        out_specs=[pl.BlockSpec((tm, d), row), pl.BlockSpec((tm, d // 2), row),
                   pl.BlockSpec((TOP_K, tm), col), pl.BlockSpec((TOP_K, tm), col), pl.BlockSpec((TOP_K, tm), col),
                   pl.BlockSpec((n_exp, 128), fixed)],
        out_shape=[jax.ShapeDtypeStruct((n, d), F32),
                   jax.ShapeDtypeStruct((n, d // 2), U32),
                   jax.ShapeDtypeStruct((TOP_K, n), I32),
                   jax.ShapeDtypeStruct((TOP_K, n), F32),
                   jax.ShapeDtypeStruct((TOP_K, n), I32),
                   jax.ShapeDtypeStruct((n_exp, 128), F32)],
        scratch_shapes=[pltpu.VMEM((n_exp, 128), F32), pltpu.VMEM((tm, d), F32)],
        compiler_params=_params(1),
        name="mix_ln1",
    )(m2, x2, ln_in_g, ln_in_b, w_o_b, ln1_g, ln1_b, rw_hi, rw_lo, rb, tri)


def _row_copy(src_ref, src_row, dst_ref, dst_row, sem):
    return pltpu.make_async_copy(src_ref.at[pl.ds(src_row, 1)], dst_ref.at[pl.ds(dst_row, 1)], sem)


def _subtile(q):
    return pl.ds(pl.multiple_of(q * SLOT_ROWS, SLOT_ROWS), SLOT_ROWS)


def _dispatch_kernel(pos_ref, gl_ref, gn_ref, used_ref, h_ref, xs_ref, zero_ref, sem, zsem, *, tt):
    base = pl.program_id(0) * (TOP_K * tt)

    @pl.when(pl.program_id(0) == 0)
    def _():
        zero_ref[...] = jnp.zeros(zero_ref.shape, zero_ref.dtype)
        n_exp = gl_ref.shape[0]
        n_sub = xs_ref.shape[0] // SLOT_ROWS

        def zero_copy(q):
            return pltpu.make_async_copy(zero_ref, xs_ref.at[_subtile(q)], zsem)

        def group_tail(e, start):
            @pl.when(gn_ref[e] > 0)
            def _():
                c = zero_copy(gl_ref[e])
                c.start() if start else c.wait()

        def unused(q, start):
            c = zero_copy(q)
            c.start() if start else c.wait()

        for start in (True, False):
            lax.fori_loop(0, n_exp, lambda e, c: (group_tail(e, start), c)[1], 0)
            lax.fori_loop(used_ref[1], n_sub, lambda q, c: (unused(q, start), c)[1], 0)

    def issue(t, c):
        for k in range(TOP_K):
            _row_copy(h_ref, t, xs_ref, pos_ref[base + k * tt + t], sem).start(priority=k % 2)
        return c

    lax.fori_loop(0, tt, issue, 0)
    for k in range(TOP_K):
        pltpu.make_async_copy(h_ref, xs_ref.at[pl.ds(0, tt)], sem).wait()


def _dispatch(h1p, pos_tiles, group_last, group_nsub, used, n_slots, tt):
    n, dw = h1p.shape
    grid_spec = pltpu.PrefetchScalarGridSpec(
        num_scalar_prefetch=4,
        grid=(n // tt,),
        in_specs=[pl.BlockSpec((tt, dw), lambda i, *_: (i, 0))],
        out_specs=pl.BlockSpec(memory_space=pl.ANY),
        scratch_shapes=[pltpu.VMEM((SLOT_ROWS, dw), h1p.dtype),
                        pltpu.SemaphoreType.DMA(()), pltpu.SemaphoreType.DMA(())],
    )
    return pl.pallas_call(
        functools.partial(_dispatch_kernel, tt=tt),
        grid_spec=grid_spec,
        out_shape=jax.ShapeDtypeStruct((n_slots, dw), h1p.dtype),
        compiler_params=_params(1),
        name="dispatch",
    )(pos_tiles, group_last, group_nsub, used, h1p)


def _expert_kernel(ge_ref, gs_ref, gn_ref, nu_ref, xs_ref, wg_ref, wl_ref, wd_ref, bg_ref, bl_ref, bd_ref,
                   ys_ref, xb_ref, acc_ref, stage_ref, wgb_ref, wlb_ref, wdb_ref, sem_in, sem_out, *, nj):
    t = pl.program_id(0)
    n_groups = nu_ref[0]
    n_items = n_groups * nj
    c = jnp.maximum(t - 1, 0)
    s = c // nj
    j = c - s * nj
    cast_slot = t % 2
    use_slot = 1 - cast_slot
    d, tn = wgb_ref.shape[1:]
    half = d // 2
    rows = _subtile

    def cast_slice(i):
        r = pl.ds(pl.multiple_of(i * (d // CAST_SLICES), d // CAST_SLICES), d // CAST_SLICES)
        wgb_ref[cast_slot, r, :] = wg_ref[0, r, :].astype(BF16)
        wlb_ref[cast_slot, r, :] = wl_ref[0, r, :].astype(BF16)
        r = pl.ds(pl.multiple_of(i * (tn // CAST_SLICES), tn // CAST_SLICES), tn // CAST_SLICES)
        wdb_ref[cast_slot, r, :] = wd_ref[0, r, :].astype(BF16)

    def cast_range(lo, hi):
        lax.fori_loop(lo, hi, lambda i, z: (cast_slice(i), z)[1], 0)

    def out_copy(first_sub, q):
        return pltpu.make_async_copy(acc_ref.at[rows(q)], ys_ref.at[rows(first_sub + q)], sem_out.at[q])

    def in_copy(first_sub, q, slot):
        return pltpu.make_async_copy(xs_ref.at[rows(first_sub + q)], stage_ref.at[slot], sem_in.at[slot])

    @pl.when(t == 0)
    def _():
        cast_range(0, CAST_SLICES)

    @pl.when((t >= 1) & (t <= n_items))
    def _():
        first_sub = gs_ref[s]
        nsub = gn_ref[s]

        @pl.when(j == 0)
        def _():
            @pl.when(s > 0)
            def _():
                prev_first = gs_ref[s - 1]
                lax.fori_loop(0, gn_ref[s - 1], lambda q, z: (out_copy(prev_first, q).wait(), z)[1], 0)

            in_copy(first_sub, 0, 0).start()

            def load(q, z):
                slot = q % 2
                in_copy(first_sub, q, slot).wait()

                @pl.when(q + 1 < nsub)
                def _():
                    in_copy(first_sub, q + 1, 1 - slot).start()

                lo, hi = _unpack_pairs(stage_ref[slot])
                xb_ref[rows(q), 0:half] = lo
                xb_ref[rows(q), half:] = hi
                acc_ref[rows(q), :] = jnp.broadcast_to(bd_ref[0], (SLOT_ROWS, d))
                return z

            lax.fori_loop(0, nsub, load, 0)

        def run(q, n):
            r = pl.ds(pl.multiple_of(q * SLOT_ROWS, SLOT_ROWS), n * SLOT_ROWS)
            x = xb_ref[r, :]
            gate = _dot(x, wgb_ref[use_slot]) + bg_ref[0]
            lin = _dot(x, wlb_ref[use_slot]) + bl_ref[0]
            gate = jnp.minimum(gate, SWIGLU_LIMIT)
            lin = jnp.clip(lin, -SWIGLU_LIMIT, SWIGLU_LIMIT)
            act = (lin + 1.0) * gate * jax.nn.sigmoid(SWIGLU_ALPHA * gate)
            acc_ref[r, :] += _dot(act.astype(BF16), wdb_ref[use_slot])

            @pl.when(j == nj - 1)
            def _():
                for u in range(n):
                    out_copy(first_sub, q + u).start()

        a = nsub // 3
        r = nsub - 3 * a
        n_triples = jnp.where((r == 1) & (a >= 1), a - 1, a)
        lax.fori_loop(0, n_triples, lambda p, z: (run(3 * p, 3), z)[1], 0)

        @pl.when((r == 1) & (a >= 1))
        def _():
            run(nsub - 4, 4)

        @pl.when(r == 2)
        def _():
            run(nsub - 2, 2)

        @pl.when(nsub == 1)
        def _():
            run(0, 1)

        cast_range(0, CAST_SLICES)

        @pl.when(t == n_items)
        def _():
            lax.fori_loop(0, nsub, lambda q, z: (out_copy(first_sub, q).wait(), z)[1], 0)
            acc_ref[rows(0), :] = jnp.zeros((SLOT_ROWS, d), F32)
            n_sub = ys_ref.shape[0] // SLOT_ROWS

            def tail_copy(q):
                return pltpu.make_async_copy(acc_ref.at[rows(0)], ys_ref.at[rows(q)], sem_out.at[0])

            lax.fori_loop(nu_ref[1], n_sub, lambda q, z: (tail_copy(q).start(), z)[1], 0)
            lax.fori_loop(nu_ref[1], n_sub, lambda q, z: (tail_copy(q).wait(), z)[1], 0)


def _experts(xs, group_expert, group_first, group_nsub, used, w_gate_up, b_gate_up, w_down, b_down, tn):
    n_slots, dw = xs.shape
    d = 2 * dw
    n_exp, _, de2 = w_gate_up.shape
    de = de2 // 2
    nj = de // tn
    max_items = group_expert.shape[0] * nj
    rmax = GROUP_SUBTILES * SLOT_ROWS

    def fetched(t, nu):
        w = jnp.minimum(t, nu[0] * nj - 1)
        return w // nj, w % nj

    def computed(t, nu):
        w = jnp.clip(t - 1, 0, nu[0] * nj - 1)
        return w // nj, w % nj

    def wg_map(t, ge, gs, gn, nu):
        g, j = fetched(t, nu)
        return ge[g], 0, j

    def wl_map(t, ge, gs, gn, nu):
        g, j = fetched(t, nu)
        return ge[g], 0, nj + j

    def wd_map(t, ge, gs, gn, nu):
        g, j = fetched(t, nu)
        return ge[g], j, 0

    def bg_map(t, ge, gs, gn, nu):
        g, j = computed(t, nu)
        return ge[g], 0, j

    def bl_map(t, ge, gs, gn, nu):
        g, j = computed(t, nu)
        return ge[g], 0, nj + j

    def bd_map(t, ge, gs, gn, nu):
        g, _ = computed(t, nu)
        return ge[g], 0, 0

    grid_spec = pltpu.PrefetchScalarGridSpec(
        num_scalar_prefetch=4,
        grid=(used[0] * nj + 1,),
        in_specs=[pl.BlockSpec(memory_space=pl.ANY),
                  pl.BlockSpec((1, d, tn), wg_map), pl.BlockSpec((1, d, tn), wl_map),
                  pl.BlockSpec((1, tn, d), wd_map),
                  pl.BlockSpec((1, 1, tn), bg_map), pl.BlockSpec((1, 1, tn), bl_map),
                  pl.BlockSpec((1, 1, d), bd_map)],
        out_specs=pl.BlockSpec(memory_space=pl.ANY),
        scratch_shapes=[pltpu.VMEM((rmax, d), BF16),
                        pltpu.VMEM((rmax, d), F32),
                        pltpu.VMEM((2, SLOT_ROWS, dw), U32),
                        pltpu.VMEM((2, d, tn), BF16), pltpu.VMEM((2, d, tn), BF16), pltpu.VMEM((2, tn, d), BF16),
                        pltpu.SemaphoreType.DMA((2,)),
                        pltpu.SemaphoreType.DMA((GROUP_SUBTILES,))],
    )
    return pl.pallas_call(
        functools.partial(_expert_kernel, nj=nj),
        grid_spec=grid_spec,
        out_shape=jax.ShapeDtypeStruct((n_slots, d), F32),
        compiler_params=_params(1),
        name="experts",
    )(group_expert, group_first, group_nsub, used, xs, w_gate_up, w_gate_up, w_down,
      b_gate_up.reshape(n_exp, 1, de2), b_gate_up.reshape(n_exp, 1, de2), b_down.reshape(n_exp, 1, d))


def _combine_kernel(pos_ref, h_ref, w_ref, g_ref, b_ref, ys_ref, o_ref, buf_ref, sem, *, tt, n_steps):
    i = pl.program_id(0)
    slot = i % 2
    ch = min(COMBINE_CHUNK, tt)

    def issue_rows(step, dst_slot, t0):
        base = step * (TOP_K * tt)
        for r in range(ch):
            for k in range(TOP_K):
                _row_copy(ys_ref, pos_ref[base + k * tt + t0 + r], buf_ref.at[dst_slot, k], t0 + r,
                          sem.at[dst_slot]).start(priority=k % 2)

    def wait_rows(src_slot):
        for k in range(TOP_K):
            pltpu.make_async_copy(ys_ref.at[pl.ds(0, tt)], buf_ref.at[src_slot, k], sem.at[src_slot]).wait()

    @pl.when(i == 0)
    def _():
        lax.fori_loop(0, tt // ch, lambda c, z: (issue_rows(0, 0, c * ch), z)[1], 0)

    wait_rows(slot)
    nxt = lax.rem(i + 1, n_steps)

    def chunk(c, z):
        r0 = pl.multiple_of(c * ch, ch)
        issue_rows(nxt, 1 - slot, r0)
        rs = pl.ds(r0, ch)
        w = w_ref[rs, :]
        y = w[:, 0:1] * buf_ref[slot, 0, rs, :]
        for k in range(1, TOP_K):
            y = y + w[:, k:k + 1] * buf_ref[slot, k, rs, :]
        o_ref[rs, :] = _layer_norm(DEEPNORM_ALPHA * h_ref[rs, :] + y, g_ref[...], b_ref[...])
        return z

    lax.fori_loop(0, tt // ch, chunk, 0)

    @pl.when(i == n_steps - 1)
    def _():
        wait_rows(1 - slot)


def _combine(h1, pos_tiles, wts_t, ys, ln2_g, ln2_b, tt):
    n, d = h1.shape
    n_steps = n // tt
    grid_spec = pltpu.PrefetchScalarGridSpec(
        num_scalar_prefetch=1,
        grid=(n_steps,),
        in_specs=[pl.BlockSpec((tt, d), lambda i, pos: (i, 0)),
                  pl.BlockSpec((tt, TOP_K), lambda i, pos: (i, 0)),
                  pl.BlockSpec((1, d), lambda i, pos: (0, 0)),
                  pl.BlockSpec((1, d), lambda i, pos: (0, 0)),
                  pl.BlockSpec(memory_space=pl.ANY)],
        out_specs=pl.BlockSpec((tt, d), lambda i, pos: (i, 0)),
        scratch_shapes=[pltpu.VMEM((2, TOP_K, tt, d), F32), pltpu.SemaphoreType.DMA((2,))],
    )
    return pl.pallas_call(
        functools.partial(_combine_kernel, tt=tt, n_steps=n_steps),
        grid_spec=grid_spec,
        out_shape=jax.ShapeDtypeStruct((n, d), F32),
        compiler_params=_params(1),
        name="combine",
    )(pos_tiles, h1, wts_t, ln2_g, ln2_b, ys)


def _routing_tables(counts, eid, rank, n_tok):
    n_exp = counts.shape[0]
    e_ids = jnp.arange(n_exp, dtype=I32)
    nsub = (counts + SLOT_ROWS - 1) // SLOT_ROWS
    ngrp = (nsub + GROUP_SUBTILES - 1) // GROUP_SUBTILES
    base = nsub // jnp.maximum(ngrp, 1)
    rem = nsub - base * ngrp
    first_sub = jnp.cumsum(nsub) - nsub
    pos = jnp.sum(jnp.where(eid[..., None] == e_ids, first_sub * SLOT_ROWS, 0), axis=-1) + rank

    n_subtiles = (n_tok * TOP_K) // SLOT_ROWS + n_exp
    max_groups = (n_subtiles + GROUP_SUBTILES - 1) // GROUP_SUBTILES + n_exp
    grp_end = jnp.cumsum(ngrp)
    g_ids = jnp.arange(max_groups, dtype=I32)
    g_exp = jnp.minimum(jnp.sum((grp_end[None, :] <= g_ids[:, None]).astype(I32), axis=1), n_exp - 1)
    pick = lambda tab: jnp.sum(jnp.where(g_exp[:, None] == e_ids[None, :], tab[None, :], 0), axis=1)
    local = g_ids - pick(grp_end - ngrp)
    g_base, g_rem = pick(base), pick(rem)
    g_nsub = g_base + (local < g_rem).astype(I32)
    g_first = pick(first_sub) + local * g_base + jnp.minimum(local, g_rem)
    used = jnp.stack([grp_end[-1], jnp.sum(nsub)]).astype(I32)
    return dict(pos=pos, n_slots=n_subtiles * SLOT_ROWS, used=used,
                group_expert=g_exp.astype(I32), group_first=g_first.astype(I32), group_nsub=g_nsub.astype(I32),
                expert_last=(first_sub + nsub - 1).astype(I32), expert_nsub=nsub.astype(I32))


def _tiles(seq, n_tok):
    return dict(
        ln_rows=min(512, n_tok),
        conv_rows=min(1024, seq), conv_cols=512,
        pool_rows=min(1024, seq),
        merge_rows=min(1024, n_tok), merge_cols=512,
        mix_rows=min(512, n_tok),
        route_rows=min(128, n_tok),
        expert_cols=256,
    )


def kernel(x, meta_tokens, ln_in_g, ln_in_b, w_in, conv_w, w_a_out, pool_w, pool_scale, w_o, ln1_g, ln1_b,
           router_w, router_b, w_gate_up, b_gate_up, w_down, b_down, ln2_g, ln2_b):
    bsz, seq, d = x.shape
    assert w_in.shape[0] == DEPTH and meta_tokens.shape[0] == N_META == HALO
    d_conv = conv_w.shape[-1]
    n_groups, pool_cin, pool_cout = pool_w.shape[1:]
    d_pool = n_groups * pool_cin
    assert n_groups == len(POOL_WINDOWS) and n_groups * pool_cout == d
    n_exp = router_w.shape[-1]
    n_tok = bsz * seq
    t = _tiles(seq, n_tok)
    row = lambda v: v.reshape(1, -1).astype(F32)

    w_in_b = w_in[0].astype(BF16)
    x2 = x.reshape(n_tok, d)
    lig, lib = row(ln_in_g), row(ln_in_b)

    hx2 = _ln_cast(x2, lig, lib, t["ln_rows"])
    hm = _ln_cast(meta_tokens.astype(F32), lig, lib, N_META)
    hx = hx2.reshape(bsz, seq, d)

    a = _conv_branch(hx, hm, w_in_b, conv_w[0], d_conv, t["conv_rows"], min(t["conv_cols"], d_conv))
    yb = _pool_branch(hx, hm, w_in_b, pool_w[0].astype(BF16), row(pool_scale[0]), 3 * d_conv, t["pool_rows"])
    m2 = _merge(hx2, a.reshape(n_tok, d_conv), yb.reshape(n_tok, d), w_in_b, w_a_out[0].astype(BF16),
                3 * d_conv + d_pool, t["merge_rows"], min(t["merge_cols"], d))

    rw_t = router_w[0].T
    rw_hi = rw_t.astype(BF16)
    rw_lo = (rw_t - rw_hi.astype(F32)).astype(BF16)
    h1, h1p, eid, wts, rank, cnt = _mix_ln1(m2, x2, lig, lib, w_o[0].astype(BF16), row(ln1_g[0]), row(ln1_b[0]),
                                            rw_hi, rw_lo, router_b[0].reshape(n_exp, 1), t["mix_rows"])

    rt = _routing_tables(cnt[:, 0].astype(I32), eid, rank, n_tok)
    tt = t["route_rows"]
    pos_tiles = rt["pos"].reshape(TOP_K, n_tok // tt, tt).transpose(1, 0, 2).reshape(-1)

    xs = _dispatch(h1p, pos_tiles, rt["expert_last"], rt["expert_nsub"], rt["used"], rt["n_slots"], tt)
    ys = _experts(xs, rt["group_expert"], rt["group_first"], rt["group_nsub"], rt["used"],
                  w_gate_up[0], b_gate_up[0], w_down[0], b_down[0], min(t["expert_cols"], w_down.shape[-2]))
    out = _combine(h1, pos_tiles, wts.T, ys, row(ln2_g[0]), row(ln2_b[0]), tt)
    return out.reshape(bsz, seq, d)
```

```python
import functools

import jax
import jax.numpy as jnp
from jax import lax
from jax.experimental import pallas as pl
from jax.experimental.pallas import tpu as pltpu

F32 = jnp.float32
BF16 = jnp.bfloat16
I32 = jnp.int32
U32 = jnp.uint32

N_META = 16
CONV_K = 3
POOL_WINDOWS = (2, 4, 8, 16)
TOP_K = 4
SWIGLU_LIMIT = 7.0
SWIGLU_ALPHA = 1.702
LN_EPS = 1e-5
DEPTH = 1
DEEPNORM_ALPHA = (2.0 * DEPTH) ** 0.25

HALO = 16
POOL_PAD = 128
POOL_CHUNK = 256
SLOT_ROWS = 256
GROUP_SUBTILES = 9
CAST_SLICES = 8
STAGE_DEPTH = 6
MIX_HALF = 256
COMBINE_CHUNK = 32
V7X_VMEM_LIMIT = 58 * 1024 * 1024

_ARB = "arbitrary"


def _params(n_axes):
    return pltpu.CompilerParams(dimension_semantics=(_ARB,) * n_axes, vmem_limit_bytes=V7X_VMEM_LIMIT)


def _layer_norm(x, g, b):
    mu = jnp.mean(x, axis=-1, keepdims=True)
    xc = x - mu
    var = jnp.mean(xc * xc, axis=-1, keepdims=True)
    return xc * lax.rsqrt(var + LN_EPS) * g + b


def _dot(a, b):
    return jnp.dot(a, b, preferred_element_type=F32)


def _pack_pairs(lo_bf16, hi_bf16):
    lo = lax.bitcast_convert_type(lo_bf16.astype(F32), U32) >> 16
    hi = lax.bitcast_convert_type(hi_bf16.astype(F32), U32) & jnp.uint32(0xFFFF0000)
    return hi | lo


def _unpack_pairs(words):
    lo = lax.bitcast_convert_type(words << 16, F32).astype(BF16)
    hi = lax.bitcast_convert_type(words & jnp.uint32(0xFFFF0000), F32).astype(BF16)
    return lo, hi


def _ln_cast_kernel(x_ref, g_ref, b_ref, o_ref):
    o_ref[...] = _layer_norm(x_ref[...], g_ref[...], b_ref[...]).astype(o_ref.dtype)


def _ln_cast(x2d, g, b, tr):
    n, d = x2d.shape
    return pl.pallas_call(
        _ln_cast_kernel,
        grid=(n // tr,),
        in_specs=[pl.BlockSpec((tr, d), lambda i: (i, 0)),
                  pl.BlockSpec((1, d), lambda i: (0, 0)),
                  pl.BlockSpec((1, d), lambda i: (0, 0))],
        out_specs=pl.BlockSpec((tr, d), lambda i: (i, 0)),
        out_shape=jax.ShapeDtypeStruct((n, d), BF16),
        compiler_params=_params(1),
        name="ln_cast",
    )(x2d, g, b)


def _conv_kernel(h_ref, halo_ref, hm_ref, wb_ref, wc_ref, wu_ref, cw_ref, o_ref, he_ref, cu_ref, *, tm):
    i = pl.program_id(1)
    j = pl.program_id(2)

    @pl.when(j == 0)
    def _():
        he_ref[HALO:, :] = h_ref[0]

    @pl.when((j == 0) & (i == 0))
    def _():
        he_ref[0:HALO, :] = hm_ref[...]

    @pl.when((j == 0) & (i > 0))
    def _():
        he_ref[0:HALO, :] = halo_ref[0]

    he = he_ref[...]
    cu_ref[...] = _dot(he, wc_ref[...]) * _dot(he, wu_ref[...])
    bg = _dot(he_ref[HALO:, :], wb_ref[...])
    cw = cw_ref[...]
    y = cw[CONV_K - 1:CONV_K, :] * cu_ref[HALO:HALO + tm, :]
    for k in range(CONV_K - 1):
        s = HALO - (CONV_K - 1) + k
        y = y + cw[k:k + 1, :] * cu_ref[s:s + tm, :]
    o_ref[0] = (bg * y).astype(o_ref.dtype)


def _conv_branch(hx, hm, w_in_b, conv_w, d_conv, tm, tc):
    bsz, seq, d = hx.shape
    nc = d_conv // tc
    hb = tm // HALO
    return pl.pallas_call(
        functools.partial(_conv_kernel, tm=tm),
        grid=(bsz, seq // tm, nc),
        in_specs=[pl.BlockSpec((1, tm, d), lambda b, i, j: (b, i, 0)),
                  pl.BlockSpec((1, HALO, d), lambda b, i, j: (b, jnp.maximum(i * hb - 1, 0), 0)),
                  pl.BlockSpec((HALO, d), lambda b, i, j: (0, 0)),
                  pl.BlockSpec((d, tc), lambda b, i, j: (0, j)),
                  pl.BlockSpec((d, tc), lambda b, i, j: (0, nc + j)),
                  pl.BlockSpec((d, tc), lambda b, i, j: (0, 2 * nc + j)),
                  pl.BlockSpec((CONV_K, tc), lambda b, i, j: (0, j))],
        out_specs=pl.BlockSpec((1, tm, tc), lambda b, i, j: (b, i, j)),
        out_shape=jax.ShapeDtypeStruct((bsz, seq, d_conv), BF16),
        scratch_shapes=[pltpu.VMEM((tm + HALO, d), BF16), pltpu.VMEM((tm + HALO, tc), F32)],
        compiler_params=_params(3),
        name="conv_branch",
    )(hx, hx, hm, w_in_b, w_in_b, w_in_b, conv_w)


def _pool_kernel(h_ref, halo_ref, hm_ref, wv_ref, band_ref, pw_ref, ps_ref, o_ref, he_ref, vh_ref, vl_ref, *, tm):
    i = pl.program_id(1)
    lead = POOL_PAD - HALO
    he_ref[0:lead, :] = jnp.zeros((lead, he_ref.shape[1]), he_ref.dtype)
    he_ref[POOL_PAD:, :] = h_ref[0]

    @pl.when(i == 0)
    def _():
        he_ref[lead:POOL_PAD, :] = hm_ref[...]

    @pl.when(i > 0)
    def _():
        he_ref[lead:POOL_PAD, :] = halo_ref[0]

    v = _dot(he_ref[...], wv_ref[...])
    v_hi = v.astype(BF16)
    vh_ref[...] = v_hi
    vl_ref[...] = (v - v_hi.astype(F32)).astype(BF16)
    ng, cin, cout = pw_ref.shape
    rows = min(POOL_CHUNK, tm)
    for c in range(tm // rows):
        r0 = c * rows
        for g in range(ng):
            band = band_ref[g]
            win = (slice(r0, r0 + rows + POOL_PAD), slice(g * cin, (g + 1) * cin))
            pooled = _dot(band, vh_ref[win]) + _dot(band, vl_ref[win])
            y = _dot(pooled.astype(BF16), pw_ref[g]) * ps_ref[:, g * cout:(g + 1) * cout]
            o_ref[0, r0:r0 + rows, g * cout:(g + 1) * cout] = y.astype(o_ref.dtype)


def _pool_band(rows):
    t = jnp.arange(rows, dtype=I32)[:, None] + POOL_PAD
    s = jnp.arange(rows + POOL_PAD, dtype=I32)[None, :]
    bands = []
    for w in POOL_WINDOWS:
        inside = ((s <= t) & (s > t - w)).astype(F32) / w
        bands.append(inside - (s == t).astype(F32))
    return jnp.stack(bands).astype(BF16)


def _pool_branch(hx, hm, w_in_b, pool_w_b, pool_scale, col0, tm):
    bsz, seq, d = hx.shape
    ng, cin, cout = pool_w_b.shape
    d_pool = ng * cin
    hb = tm // HALO
    band = _pool_band(min(POOL_CHUNK, tm))
    return pl.pallas_call(
        functools.partial(_pool_kernel, tm=tm),
        grid=(bsz, seq // tm),
        in_specs=[pl.BlockSpec((1, tm, d), lambda b, i: (b, i, 0)),
                  pl.BlockSpec((1, HALO, d), lambda b, i: (b, jnp.maximum(i * hb - 1, 0), 0)),
                  pl.BlockSpec((HALO, d), lambda b, i: (0, 0)),
                  pl.BlockSpec((d, d_pool), lambda b, i: (0, col0 // d_pool)),
                  pl.BlockSpec(band.shape, lambda b, i: (0, 0, 0)),
                  pl.BlockSpec((ng, cin, cout), lambda b, i: (0, 0, 0)),
                  pl.BlockSpec((1, ng * cout), lambda b, i: (0, 0))],
        out_specs=pl.BlockSpec((1, tm, ng * cout), lambda b, i: (b, i, 0)),
        out_shape=jax.ShapeDtypeStruct((bsz, seq, ng * cout), BF16),
        scratch_shapes=[pltpu.VMEM((tm + POOL_PAD, d), BF16),
                        pltpu.VMEM((tm + POOL_PAD, d_pool), BF16),
                        pltpu.VMEM((tm + POOL_PAD, d_pool), BF16)],
        compiler_params=_params(2),
        name="pool_branch",
    )(hx, hx, hm, w_in_b, band, pool_w_b, pool_scale)


def _merge_kernel(h_ref, a_ref, yb_ref, wga_ref, wgb_ref, wao_ref, o_ref):
    h = h_ref[...]
    ga = jax.nn.sigmoid(_dot(h, wga_ref[...]))
    gb = jax.nn.sigmoid(_dot(h, wgb_ref[...]))
    ya = _dot(a_ref[...], wao_ref[...])
    o_ref[...] = (ga * ya + gb * yb_ref[...].astype(F32)).astype(o_ref.dtype)


def _merge(hx2, a2, yb2, w_in_b, w_a_out_b, col0, tm, tj):
    n, d = hx2.shape
    dc = a2.shape[1]
    nj = d // tj
    return pl.pallas_call(
        _merge_kernel,
        grid=(n // tm, nj),
        in_specs=[pl.BlockSpec((tm, d), lambda i, j: (i, 0)),
                  pl.BlockSpec((tm, dc), lambda i, j: (i, 0)),
                  pl.BlockSpec((tm, tj), lambda i, j: (i, j)),
                  pl.BlockSpec((d, tj), lambda i, j: (0, col0 // tj + j)),
                  pl.BlockSpec((d, tj), lambda i, j: (0, col0 // tj + nj + j)),
                  pl.BlockSpec((dc, tj), lambda i, j: (0, j))],
        out_specs=pl.BlockSpec((tm, tj), lambda i, j: (i, j)),
        out_shape=jax.ShapeDtypeStruct((n, d), BF16),
        compiler_params=_params(2),
        name="merge",
    )(hx2, a2, yb2, w_in_b, w_in_b, w_a_out_b)


def _mix_kernel(m_ref, x_ref, lig_ref, lib_ref, wo_ref, g1_ref, b1_ref, rwh_ref, rwl_ref, rb_ref, tri_ref,
                h1_ref, h1p_ref, eid_ref, wts_ref, rank_ref, cnt_ref, carry_ref, z_ref, *, rows):
    step = pl.program_id(0)

    @pl.when(step == 0)
    def _():
        carry_ref[...] = jnp.zeros(carry_ref.shape, F32)

    for r0 in range(0, m_ref.shape[0], rows):
        rs = slice(r0, r0 + rows)
        h0 = _layer_norm(x_ref[rs, :], lig_ref[...], lib_ref[...])
        z_ref[rs, :] = DEEPNORM_ALPHA * h0 + _dot(m_ref[rs, :], wo_ref[...])

    for r0 in range(0, m_ref.shape[0], rows):
        rs = slice(r0, r0 + rows)
        h1 = _layer_norm(z_ref[rs, :], g1_ref[...], b1_ref[...])
        h1_ref[rs, :] = h1
        h_hi = h1.astype(BF16)
        half = h1.shape[1] // 2
        h1p_ref[rs, :] = _pack_pairs(h_hi[:, :half], h_hi[:, half:])

        h_lo = (h1 - h_hi.astype(F32)).astype(BF16)
        nt = (((1,), (1,)), ((), ()))
        rwh = rwh_ref[...]
        logits = (lax.dot_general(rwh, h_hi, nt, preferred_element_type=F32)
                  + lax.dot_general(rwh, h_lo, nt, preferred_element_type=F32)
                  + lax.dot_general(rwl_ref[...], h_hi, nt, preferred_element_type=F32)
                  + rb_ref[...])
        n_exp = logits.shape[0]
        e_iota = lax.broadcasted_iota(I32, logits.shape, 0)
        work = logits
        vals, hots = [], []
        for k in range(TOP_K):
            m = jnp.max(work, axis=0, keepdims=True)
            idx = jnp.min(jnp.where(work == m, e_iota, n_exp), axis=0, keepdims=True)
            hot = e_iota == idx
            vals.append(m)
            hots.append(hot)
            eid_ref[k:k + 1, rs] = idx
            work = jnp.where(hot, -jnp.inf, work)
        exps = [jnp.exp(v - vals[0]) for v in vals]
        denom = exps[0]
        for e in exps[1:]:
            denom = denom + e
        for k in range(TOP_K):
            wts_ref[k:k + 1, rs] = exps[k] / denom

        multi = hots[0]
        for hot in hots[1:]:
            multi = multi | hot
        multi_f = jnp.where(multi, 1.0, 0.0).astype(F32)
        prefix = _dot(multi_f.astype(BF16), tri_ref[...]) + carry_ref[:, 0:1]
        for k in range(TOP_K):
            rank_ref[k:k + 1, rs] = jnp.sum(jnp.where(hots[k], prefix, 0.0), axis=0, keepdims=True).astype(I32)
        carry_ref[...] = carry_ref[...] + jnp.sum(multi_f, axis=1, keepdims=True)
    cnt_ref[...] = carry_ref[...]


def _mix_ln1(m2, x2, ln_in_g, ln_in_b, w_o_b, ln1_g, ln1_b, rw_hi, rw_lo, rb, tm):
    n, d = x2.shape
    n_exp = rw_hi.shape[0]
    rows = min(MIX_HALF, tm)
    tri = (jnp.arange(rows, dtype=I32)[:, None] < jnp.arange(rows, dtype=I32)[None, :]).astype(BF16)
    row = lambda i: (i, 0)
    fixed = lambda i: (0, 0)
    col = lambda i: (0, i)
    return pl.pallas_call(
        functools.partial(_mix_kernel, rows=rows),
        grid=(n // tm,),
        in_specs=[pl.BlockSpec((tm, d), row), pl.BlockSpec((tm, d), row),
                  pl.BlockSpec((1, d), fixed), pl.BlockSpec((1, d), fixed),
                  pl.BlockSpec((d, d), fixed),
                  pl.BlockSpec((1, d), fixed), pl.BlockSpec((1, d), fixed),
                  pl.BlockSpec((n_exp, d), fixed), pl.BlockSpec((n_exp, d), fixed),
                  pl.BlockSpec((n_exp, 1), fixed),
                  pl.BlockSpec((rows, rows), fixed)],
        out_specs=[pl.BlockSpec((tm, d), row), pl.BlockSpec((tm, d // 2), row),
                   pl.BlockSpec((TOP_K, tm), col), pl.BlockSpec((TOP_K, tm), col), pl.BlockSpec((TOP_K, tm), col),
                   pl.BlockSpec((n_exp, 128), fixed)],
        out_shape=[jax.ShapeDtypeStruct((n, d), F32),
                   jax.ShapeDtypeStruct((n, d // 2), U32),
                   jax.ShapeDtypeStruct((TOP_K, n), I32),
                   jax.ShapeDtypeStruct((TOP_K, n), F32),
                   jax.ShapeDtypeStruct((TOP_K, n), I32),
                   jax.ShapeDtypeStruct((n_exp, 128), F32)],
        scratch_shapes=[pltpu.VMEM((n_exp, 128), F32), pltpu.VMEM((tm, d), F32)],
        compiler_params=_params(1),
        name="mix_ln1",
    )(m2, x2, ln_in_g, ln_in_b, w_o_b, ln1_g, ln1_b, rw_hi, rw_lo, rb, tri)


def _row_copy(src_ref, src_row, dst_ref, dst_row, sem):
    return pltpu.make_async_copy(src_ref.at[pl.ds(src_row, 1)], dst_ref.at[pl.ds(dst_row, 1)], sem)


def _subtile(q):
    return pl.ds(pl.multiple_of(q * SLOT_ROWS, SLOT_ROWS), SLOT_ROWS)


def _dispatch_kernel(pos_ref, gl_ref, gn_ref, used_ref, h_ref, xs_ref, zero_ref, sem, zsem, *, tt):
    base = pl.program_id(0) * (TOP_K * tt)

    @pl.when(pl.program_id(0) == 0)
    def _():
        zero_ref[...] = jnp.zeros(zero_ref.shape, zero_ref.dtype)
        n_exp = gl_ref.shape[0]
        n_sub = xs_ref.shape[0] // SLOT_ROWS

        def zero_copy(q):
            return pltpu.make_async_copy(zero_ref, xs_ref.at[_subtile(q)], zsem)

        def group_tail(e, start):
            @pl.when(gn_ref[e] > 0)
            def _():
                c = zero_copy(gl_ref[e])
                c.start() if start else c.wait()

        def unused(q, start):
            c = zero_copy(q)
            c.start() if start else c.wait()

        for start in (True, False):
            lax.fori_loop(0, n_exp, lambda e, c: (group_tail(e, start), c)[1], 0)
            lax.fori_loop(used_ref[1], n_sub, lambda q, c: (unused(q, start), c)[1], 0)

    def issue(t, c):
        for k in range(TOP_K):
            _row_copy(h_ref, t, xs_ref, pos_ref[base + k * tt + t], sem).start(priority=k % 2)
        return c

    lax.fori_loop(0, tt, issue, 0)
    for k in range(TOP_K):
        pltpu.make_async_copy(h_ref, xs_ref.at[pl.ds(0, tt)], sem).wait()


def _dispatch(h1p, pos_tiles, group_last, group_nsub, used, n_slots, tt):
    n, dw = h1p.shape
    grid_spec = pltpu.PrefetchScalarGridSpec(
        num_scalar_prefetch=4,
        grid=(n // tt,),
        in_specs=[pl.BlockSpec((tt, dw), lambda i, *_: (i, 0))],
        out_specs=pl.BlockSpec(memory_space=pl.ANY),
        scratch_shapes=[pltpu.VMEM((SLOT_ROWS, dw), h1p.dtype),
                        pltpu.SemaphoreType.DMA(()), pltpu.SemaphoreType.DMA(())],
    )
    return pl.pallas_call(
        functools.partial(_dispatch_kernel, tt=tt),
        grid_spec=grid_spec,
        out_shape=jax.ShapeDtypeStruct((n_slots, dw), h1p.dtype),
        compiler_params=_params(1),
        name="dispatch",
    )(pos_tiles, group_last, group_nsub, used, h1p)


def _expert_kernel(ge_ref, gs_ref, gn_ref, nu_ref, xs_ref, wg_ref, wl_ref, wd_ref, bg_ref, bl_ref, bd_ref,
                   ys_ref, xb_ref, acc_ref, stage_ref, wgb_ref, wlb_ref, wdb_ref, sem_in, sem_out, *, nj):
    t = pl.program_id(0)
    n_groups = nu_ref[0]
    n_items = n_groups * nj
    c = jnp.maximum(t - 1, 0)
    s = c // nj
    j = c - s * nj
    cast_slot = t % 2
    use_slot = 1 - cast_slot
    d, tn = wgb_ref.shape[1:]
    half = d // 2
    rows = _subtile

    def cast_slice(i):
        r = pl.ds(pl.multiple_of(i * (d // CAST_SLICES), d // CAST_SLICES), d // CAST_SLICES)
        wgb_ref[cast_slot, r, :] = wg_ref[0, r, :].astype(BF16)
        wlb_ref[cast_slot, r, :] = wl_ref[0, r, :].astype(BF16)
        r = pl.ds(pl.multiple_of(i * (tn // CAST_SLICES), tn // CAST_SLICES), tn // CAST_SLICES)
        wdb_ref[cast_slot, r, :] = wd_ref[0, r, :].astype(BF16)

    def cast_range(lo, hi):
        lax.fori_loop(lo, hi, lambda i, z: (cast_slice(i), z)[1], 0)

    def out_copy(first_sub, q):
        return pltpu.make_async_copy(acc_ref.at[rows(q)], ys_ref.at[rows(first_sub + q)], sem_out.at[q])

    def in_copy(first_sub, q, slot):
        return pltpu.make_async_copy(xs_ref.at[rows(first_sub + q)], stage_ref.at[slot], sem_in.at[slot])

    @pl.when(t == 0)
    def _():
        cast_range(0, CAST_SLICES)

    @pl.when((t >= 1) & (t <= n_items))
    def _():
        first_sub = gs_ref[s]
        nsub = gn_ref[s]

        @pl.when(j == 0)
        def _():
            sp = jnp.maximum(s - 1, 0)
            prev_first = gs_ref[sp]
            prev_nsub = jnp.where(s > 0, gn_ref[sp], 0)
            depth = stage_ref.shape[0]
            lax.fori_loop(0, jnp.minimum(nsub, depth),
                          lambda q, z: (in_copy(first_sub, q, q).start(), z)[1], 0)

            def load(q, z):
                slot = q % depth
                in_copy(first_sub, q, slot).wait()
                lo, hi = _unpack_pairs(stage_ref[slot])
                xb_ref[rows(q), 0:half] = lo
                xb_ref[rows(q), half:] = hi

                @pl.when(q + depth < nsub)
                def _():
                    in_copy(first_sub, q + depth, slot).start()

                @pl.when(q < prev_nsub)
                def _():
                    out_copy(prev_first, q).wait()

                acc_ref[rows(q), :] = jnp.broadcast_to(bd_ref[0], (SLOT_ROWS, d))
                return z

            lax.fori_loop(0, nsub, load, 0)
            lax.fori_loop(nsub, prev_nsub, lambda q, z: (out_copy(prev_first, q).wait(), z)[1], 0)

        def run(q, n):
            r = pl.ds(pl.multiple_of(q * SLOT_ROWS, SLOT_ROWS), n * SLOT_ROWS)
            x = xb_ref[r, :]
            gate = _dot(x, wgb_ref[use_slot]) + bg_ref[0]
            lin = _dot(x, wlb_ref[use_slot]) + bl_ref[0]
            gate = jnp.minimum(gate, SWIGLU_LIMIT)
            lin = jnp.clip(lin, -SWIGLU_LIMIT, SWIGLU_LIMIT)
            act = (lin + 1.0) * gate * jax.nn.sigmoid(SWIGLU_ALPHA * gate)
            acc_ref[r, :] += _dot(act.astype(BF16), wdb_ref[use_slot])

            @pl.when(j == nj - 1)
            def _():
                for u in range(n):
                    out_copy(first_sub, q + u).start()

        a = nsub // 3
        r = nsub - 3 * a
        n_triples = jnp.where((r == 1) & (a >= 1), a - 1, a)
        lax.fori_loop(0, n_triples, lambda p, z: (run(3 * p, 3), z)[1], 0)

        @pl.when((r == 1) & (a >= 1))
        def _():
            run(nsub - 4, 4)

        @pl.when(r == 2)
        def _():
            run(nsub - 2, 2)

        @pl.when(nsub == 1)
        def _():
            run(0, 1)

        cast_range(0, CAST_SLICES)

        @pl.when(t == n_items)
        def _():
            lax.fori_loop(0, nsub, lambda q, z: (out_copy(first_sub, q).wait(), z)[1], 0)
            acc_ref[rows(0), :] = jnp.zeros((SLOT_ROWS, d), F32)
            n_sub = ys_ref.shape[0] // SLOT_ROWS

            def tail_copy(q):
                return pltpu.make_async_copy(acc_ref.at[rows(0)], ys_ref.at[rows(q)], sem_out.at[0])

            lax.fori_loop(nu_ref[1], n_sub, lambda q, z: (tail_copy(q).start(), z)[1], 0)
            lax.fori_loop(nu_ref[1], n_sub, lambda q, z: (tail_copy(q).wait(), z)[1], 0)


def _experts(xs, group_expert, group_first, group_nsub, used, w_gate_up, b_gate_up, w_down, b_down, tn):
    n_slots, dw = xs.shape
    d = 2 * dw
    n_exp, _, de2 = w_gate_up.shape
    de = de2 // 2
    nj = de // tn
    max_items = group_expert.shape[0] * nj
    rmax = GROUP_SUBTILES * SLOT_ROWS

    def fetched(t, nu):
        w = jnp.minimum(t, nu[0] * nj - 1)
        return w // nj, w % nj

    def computed(t, nu):
        w = jnp.clip(t - 1, 0, nu[0] * nj - 1)
        return w // nj, w % nj

    def wg_map(t, ge, gs, gn, nu):
        g, j = fetched(t, nu)
        return ge[g], 0, j

    def wl_map(t, ge, gs, gn, nu):
        g, j = fetched(t, nu)
        return ge[g], 0, nj + j

    def wd_map(t, ge, gs, gn, nu):
        g, j = fetched(t, nu)
        return ge[g], j, 0

    def bg_map(t, ge, gs, gn, nu):
        g, j = computed(t, nu)
        return ge[g], 0, j

    def bl_map(t, ge, gs, gn, nu):
        g, j = computed(t, nu)
        return ge[g], 0, nj + j

    def bd_map(t, ge, gs, gn, nu):
        g, _ = computed(t, nu)
        return ge[g], 0, 0

    grid_spec = pltpu.PrefetchScalarGridSpec(
        num_scalar_prefetch=4,
        grid=(used[0] * nj + 1,),
        in_specs=[pl.BlockSpec(memory_space=pl.ANY),
                  pl.BlockSpec((1, d, tn), wg_map), pl.BlockSpec((1, d, tn), wl_map),
                  pl.BlockSpec((1, tn, d), wd_map),
                  pl.BlockSpec((1, 1, tn), bg_map), pl.BlockSpec((1, 1, tn), bl_map),
                  pl.BlockSpec((1, 1, d), bd_map)],
        out_specs=pl.BlockSpec(memory_space=pl.ANY),
        scratch_shapes=[pltpu.VMEM((rmax, d), BF16),
                        pltpu.VMEM((rmax, d), F32),
                        pltpu.VMEM((STAGE_DEPTH, SLOT_ROWS, dw), U32),
                        pltpu.VMEM((2, d, tn), BF16), pltpu.VMEM((2, d, tn), BF16), pltpu.VMEM((2, tn, d), BF16),
                        pltpu.SemaphoreType.DMA((STAGE_DEPTH,)),
                        pltpu.SemaphoreType.DMA((GROUP_SUBTILES,))],
    )
    return pl.pallas_call(
        functools.partial(_expert_kernel, nj=nj),
        grid_spec=grid_spec,
        out_shape=jax.ShapeDtypeStruct((n_slots, d), F32),
        compiler_params=_params(1),
        name="experts",
    )(group_expert, group_first, group_nsub, used, xs, w_gate_up, w_gate_up, w_down,
      b_gate_up.reshape(n_exp, 1, de2), b_gate_up.reshape(n_exp, 1, de2), b_down.reshape(n_exp, 1, d))


def _combine_kernel(pos_ref, h_ref, w_ref, g_ref, b_ref, ys_ref, o_ref, buf_ref, sem, *, tt, n_steps):
    i = pl.program_id(0)
    slot = i % 2
    ch = min(COMBINE_CHUNK, tt)

    def issue_rows(step, dst_slot, t0):
        base = step * (TOP_K * tt)
        for r in range(ch):
            for k in range(TOP_K):
                _row_copy(ys_ref, pos_ref[base + k * tt + t0 + r], buf_ref.at[dst_slot, k], t0 + r,
                          sem.at[dst_slot]).start(priority=k % 2)

    def wait_rows(src_slot):
        for k in range(TOP_K):
            pltpu.make_async_copy(ys_ref.at[pl.ds(0, tt)], buf_ref.at[src_slot, k], sem.at[src_slot]).wait()

    @pl.when(i == 0)
    def _():
        lax.fori_loop(0, tt // ch, lambda c, z: (issue_rows(0, 0, c * ch), z)[1], 0)

    wait_rows(slot)
    nxt = lax.rem(i + 1, n_steps)

    def chunk(c, z):
        r0 = pl.multiple_of(c * ch, ch)
        issue_rows(nxt, 1 - slot, r0)
        rs = pl.ds(r0, ch)
        w = w_ref[rs, :]
        y = w[:, 0:1] * buf_ref[slot, 0, rs, :]
        for k in range(1, TOP_K):
            y = y + w[:, k:k + 1] * buf_ref[slot, k, rs, :]
        o_ref[rs, :] = _layer_norm(DEEPNORM_ALPHA * h_ref[rs, :] + y, g_ref[...], b_ref[...])
        return z

    lax.fori_loop(0, tt // ch, chunk, 0)

    @pl.when(i == n_steps - 1)
    def _():
        wait_rows(1 - slot)


def _combine(h1, pos_tiles, wts_t, ys, ln2_g, ln2_b, tt):
    n, d = h1.shape
    n_steps = n // tt
    grid_spec = pltpu.PrefetchScalarGridSpec(
        num_scalar_prefetch=1,
        grid=(n_steps,),
        in_specs=[pl.BlockSpec((tt, d), lambda i, pos: (i, 0)),
                  pl.BlockSpec((tt, TOP_K), lambda i, pos: (i, 0)),
                  pl.BlockSpec((1, d), lambda i, pos: (0, 0)),
                  pl.BlockSpec((1, d), lambda i, pos: (0, 0)),
                  pl.BlockSpec(memory_space=pl.ANY)],
        out_specs=pl.BlockSpec((tt, d), lambda i, pos: (i, 0)),
        scratch_shapes=[pltpu.VMEM((2, TOP_K, tt, d), F32), pltpu.SemaphoreType.DMA((2,))],
    )
    return pl.pallas_call(
        functools.partial(_combine_kernel, tt=tt, n_steps=n_steps),
        grid_spec=grid_spec,
        out_shape=jax.ShapeDtypeStruct((n, d), F32),
        compiler_params=_params(1),
        name="combine",
    )(pos_tiles, h1, wts_t, ln2_g, ln2_b, ys)


def _routing_tables(counts, eid, rank, n_tok):
    n_exp = counts.shape[0]
    e_ids = jnp.arange(n_exp, dtype=I32)
    nsub = (counts + SLOT_ROWS - 1) // SLOT_ROWS
    ngrp = (nsub + GROUP_SUBTILES - 1) // GROUP_SUBTILES
    base = nsub // jnp.maximum(ngrp, 1)
    rem = nsub - base * ngrp
    first_sub = jnp.cumsum(nsub) - nsub
    pos = jnp.sum(jnp.where(eid[..., None] == e_ids, first_sub * SLOT_ROWS, 0), axis=-1) + rank

    n_subtiles = (n_tok * TOP_K) // SLOT_ROWS + n_exp
    max_groups = (n_subtiles + GROUP_SUBTILES - 1) // GROUP_SUBTILES + n_exp
    grp_end = jnp.cumsum(ngrp)
    g_ids = jnp.arange(max_groups, dtype=I32)
    g_exp = jnp.minimum(jnp.sum((grp_end[None, :] <= g_ids[:, None]).astype(I32), axis=1), n_exp - 1)
    pick = lambda tab: jnp.sum(jnp.where(g_exp[:, None] == e_ids[None, :], tab[None, :], 0), axis=1)
    local = g_ids - pick(grp_end - ngrp)
    g_base, g_rem = pick(base), pick(rem)
    g_nsub = g_base + (local < g_rem).astype(I32)
    g_first = pick(first_sub) + local * g_base + jnp.minimum(local, g_rem)
    used = jnp.stack([grp_end[-1], jnp.sum(nsub)]).astype(I32)
    return dict(pos=pos, n_slots=n_subtiles * SLOT_ROWS, used=used,
                group_expert=g_exp.astype(I32), group_first=g_first.astype(I32), group_nsub=g_nsub.astype(I32),
                expert_last=(first_sub + nsub - 1).astype(I32), expert_nsub=nsub.astype(I32))


def _tiles(seq, n_tok):
    return dict(
        ln_rows=min(512, n_tok),
        conv_rows=min(1024, seq), conv_cols=512,
        pool_rows=min(1024, seq),
        merge_rows=min(1024, n_tok), merge_cols=512,
        mix_rows=min(512, n_tok),
        route_rows=min(128, n_tok),
        expert_cols=256,
    )


def kernel(x, meta_tokens, ln_in_g, ln_in_b, w_in, conv_w, w_a_out, pool_w, pool_scale, w_o, ln1_g, ln1_b,
           router_w, router_b, w_gate_up, b_gate_up, w_down, b_down, ln2_g, ln2_b):
    bsz, seq, d = x.shape
    assert w_in.shape[0] == DEPTH and meta_tokens.shape[0] == N_META == HALO
    d_conv = conv_w.shape[-1]
    n_groups, pool_cin, pool_cout = pool_w.shape[1:]
    d_pool = n_groups * pool_cin
    assert n_groups == len(POOL_WINDOWS) and n_groups * pool_cout == d
    n_exp = router_w.shape[-1]
    n_tok = bsz * seq
    t = _tiles(seq, n_tok)
    row = lambda v: v.reshape(1, -1).astype(F32)

    w_in_b = w_in[0].astype(BF16)
    x2 = x.reshape(n_tok, d)
    lig, lib = row(ln_in_g), row(ln_in_b)

    hx2 = _ln_cast(x2, lig, lib, t["ln_rows"])
    hm = _ln_cast(meta_tokens.astype(F32), lig, lib, N_META)
    hx = hx2.reshape(bsz, seq, d)

    a = _conv_branch(hx, hm, w_in_b, conv_w[0], d_conv, t["conv_rows"], min(t["conv_cols"], d_conv))
    yb = _pool_branch(hx, hm, w_in_b, pool_w[0].astype(BF16), row(pool_scale[0]), 3 * d_conv, t["pool_rows"])
    m2 = _merge(hx2, a.reshape(n_tok, d_conv), yb.reshape(n_tok, d), w_in_b, w_a_out[0].astype(BF16),
                3 * d_conv + d_pool, t["merge_rows"], min(t["merge_cols"], d))

    rw_t = router_w[0].T
    rw_hi = rw_t.astype(BF16)
    rw_lo = (rw_t - rw_hi.astype(F32)).astype(BF16)
    h1, h1p, eid, wts, rank, cnt = _mix_ln1(m2, x2, lig, lib, w_o[0].astype(BF16), row(ln1_g[0]), row(ln1_b[0]),
                                            rw_hi, rw_lo, router_b[0].reshape(n_exp, 1), t["mix_rows"])

    rt = _routing_tables(cnt[:, 0].astype(I32), eid, rank, n_tok)
    tt = t["route_rows"]
    pos_tiles = rt["pos"].reshape(TOP_K, n_tok // tt, tt).transpose(1, 0, 2).reshape(-1)

    xs = _dispatch(h1p, pos_tiles, rt["expert_last"], rt["expert_nsub"], rt["used"], rt["n_slots"], tt)
    ys = _experts(xs, rt["group_expert"], rt["group_first"], rt["group_nsub"], rt["used"],
                  w_gate_up[0], b_gate_up[0], w_down[0], b_down[0], min(t["expert_cols"], w_down.shape[-2]))
    out = _combine(h1, pos_tiles, wts.T, ys, row(ln2_g[0]), row(ln2_b[0]), tt)
    return out.reshape(bsz, seq, d)
```

```python
import functools

import jax
import jax.numpy as jnp
from jax import lax
from jax.experimental import pallas as pl
from jax.experimental.pallas import tpu as pltpu

F32 = jnp.float32
BF16 = jnp.bfloat16
I32 = jnp.int32
U32 = jnp.uint32

N_META = 16
CONV_K = 3
POOL_WINDOWS = (2, 4, 8, 16)
TOP_K = 4
SWIGLU_LIMIT = 7.0
SWIGLU_ALPHA = 1.702
LN_EPS = 1e-5
DEPTH = 1
DEEPNORM_ALPHA = (2.0 * DEPTH) ** 0.25

HALO = 16
POOL_PAD = 128
POOL_CHUNK = 256
SLOT_ROWS = 256
GROUP_SUBTILES = 9
CAST_SLICES = 8
STAGE_DEPTH = 6
MIX_HALF = 256
COMBINE_CHUNK = 32
V7X_VMEM_LIMIT = 58 * 1024 * 1024

_ARB = "arbitrary"


def _params(n_axes):
    return pltpu.CompilerParams(dimension_semantics=(_ARB,) * n_axes, vmem_limit_bytes=V7X_VMEM_LIMIT)


def _layer_norm(x, g, b):
    mu = jnp.mean(x, axis=-1, keepdims=True)
    xc = x - mu
    var = jnp.mean(xc * xc, axis=-1, keepdims=True)
    return xc * lax.rsqrt(var + LN_EPS) * g + b


def _dot(a, b):
    return jnp.dot(a, b, preferred_element_type=F32)


def _pack_pairs(lo_bf16, hi_bf16):
    lo = lax.bitcast_convert_type(lo_bf16.astype(F32), U32) >> 16
    hi = lax.bitcast_convert_type(hi_bf16.astype(F32), U32) & jnp.uint32(0xFFFF0000)
    return hi | lo


def _unpack_pairs(words):
    lo = lax.bitcast_convert_type(words << 16, F32).astype(BF16)
    hi = lax.bitcast_convert_type(words & jnp.uint32(0xFFFF0000), F32).astype(BF16)
    return lo, hi


def _ln_cast_kernel(x_ref, g_ref, b_ref, o_ref):
    o_ref[...] = _layer_norm(x_ref[...], g_ref[...], b_ref[...]).astype(o_ref.dtype)


def _ln_cast(x2d, g, b, tr):
    n, d = x2d.shape
    return pl.pallas_call(
        _ln_cast_kernel,
        grid=(n // tr,),
        in_specs=[pl.BlockSpec((tr, d), lambda i: (i, 0)),
                  pl.BlockSpec((1, d), lambda i: (0, 0)),
                  pl.BlockSpec((1, d), lambda i: (0, 0))],
        out_specs=pl.BlockSpec((tr, d), lambda i: (i, 0)),
        out_shape=jax.ShapeDtypeStruct((n, d), BF16),
        compiler_params=_params(1),
        name="ln_cast",
    )(x2d, g, b)


def _conv_kernel(h_ref, halo_ref, hm_ref, wb_ref, wc_ref, wu_ref, cw_ref, o_ref, he_ref, cu_ref, *, tm):
    i = pl.program_id(1)
    j = pl.program_id(2)

    @pl.when(j == 0)
    def _():
        he_ref[HALO:, :] = h_ref[0]

    @pl.when((j == 0) & (i == 0))
    def _():
        he_ref[0:HALO, :] = hm_ref[...]

    @pl.when((j == 0) & (i > 0))
    def _():
        he_ref[0:HALO, :] = halo_ref[0]

    he = he_ref[...]
    cu_ref[...] = _dot(he, wc_ref[...]) * _dot(he, wu_ref[...])
    bg = _dot(he_ref[HALO:, :], wb_ref[...])
    cw = cw_ref[...]
    y = cw[CONV_K - 1:CONV_K, :] * cu_ref[HALO:HALO + tm, :]
    for k in range(CONV_K - 1):
        s = HALO - (CONV_K - 1) + k
        y = y + cw[k:k + 1, :] * cu_ref[s:s + tm, :]
    o_ref[0] = (bg * y).astype(o_ref.dtype)


def _conv_branch(hx, hm, w_in_b, conv_w, d_conv, tm, tc):
    bsz, seq, d = hx.shape
    nc = d_conv // tc
    hb = tm // HALO
    return pl.pallas_call(
        functools.partial(_conv_kernel, tm=tm),
        grid=(bsz, seq // tm, nc),
        in_specs=[pl.BlockSpec((1, tm, d), lambda b, i, j: (b, i, 0)),
                  pl.BlockSpec((1, HALO, d), lambda b, i, j: (b, jnp.maximum(i * hb - 1, 0), 0)),
                  pl.BlockSpec((HALO, d), lambda b, i, j: (0, 0)),
                  pl.BlockSpec((d, tc), lambda b, i, j: (0, j)),
                  pl.BlockSpec((d, tc), lambda b, i, j: (0, nc + j)),
                  pl.BlockSpec((d, tc), lambda b, i, j: (0, 2 * nc + j)),
                  pl.BlockSpec((CONV_K, tc), lambda b, i, j: (0, j))],
        out_specs=pl.BlockSpec((1, tm, tc), lambda b, i, j: (b, i, j)),
        out_shape=jax.ShapeDtypeStruct((bsz, seq, d_conv), BF16),
        scratch_shapes=[pltpu.VMEM((tm + HALO, d), BF16), pltpu.VMEM((tm + HALO, tc), F32)],
        compiler_params=_params(3),
        name="conv_branch",
    )(hx, hx, hm, w_in_b, w_in_b, w_in_b, conv_w)


def _pool_kernel(h_ref, halo_ref, hm_ref, wv_ref, band_ref, pw_ref, ps_ref, o_ref, he_ref, vh_ref, vl_ref, *, tm):
    i = pl.program_id(1)
    lead = POOL_PAD - HALO
    he_ref[0:lead, :] = jnp.zeros((lead, he_ref.shape[1]), he_ref.dtype)
    he_ref[POOL_PAD:, :] = h_ref[0]

    @pl.when(i == 0)
    def _():
        he_ref[lead:POOL_PAD, :] = hm_ref[...]

    @pl.when(i > 0)
    def _():
        he_ref[lead:POOL_PAD, :] = halo_ref[0]

    v = _dot(he_ref[...], wv_ref[...])
    v_hi = v.astype(BF16)
    vh_ref[...] = v_hi
    vl_ref[...] = (v - v_hi.astype(F32)).astype(BF16)
    ng, cin, cout = pw_ref.shape
    rows = min(POOL_CHUNK, tm)
    for c in range(tm // rows):
        r0 = c * rows
        for g in range(ng):
            band = band_ref[g]
            win = (slice(r0, r0 + rows + POOL_PAD), slice(g * cin, (g + 1) * cin))
            pooled = _dot(band, vh_ref[win]) + _dot(band, vl_ref[win])
            y = _dot(pooled.astype(BF16), pw_ref[g]) * ps_ref[:, g * cout:(g + 1) * cout]
            o_ref[0, r0:r0 + rows, g * cout:(g + 1) * cout] = y.astype(o_ref.dtype)


def _pool_band(rows):
    t = jnp.arange(rows, dtype=I32)[:, None] + POOL_PAD
    s = jnp.arange(rows + POOL_PAD, dtype=I32)[None, :]
    bands = []
    for w in POOL_WINDOWS:
        inside = ((s <= t) & (s > t - w)).astype(F32) / w
        bands.append(inside - (s == t).astype(F32))
    return jnp.stack(bands).astype(BF16)


def _pool_branch(hx, hm, w_in_b, pool_w_b, pool_scale, col0, tm):
    bsz, seq, d = hx.shape
    ng, cin, cout = pool_w_b.shape
    d_pool = ng * cin
    hb = tm // HALO
    band = _pool_band(min(POOL_CHUNK, tm))
    return pl.pallas_call(
        functools.partial(_pool_kernel, tm=tm),
        grid=(bsz, seq // tm),
        in_specs=[pl.BlockSpec((1, tm, d), lambda b, i: (b, i, 0)),
                  pl.BlockSpec((1, HALO, d), lambda b, i: (b, jnp.maximum(i * hb - 1, 0), 0)),
                  pl.BlockSpec((HALO, d), lambda b, i: (0, 0)),
                  pl.BlockSpec((d, d_pool), lambda b, i: (0, col0 // d_pool)),
                  pl.BlockSpec(band.shape, lambda b, i: (0, 0, 0)),
                  pl.BlockSpec((ng, cin, cout), lambda b, i: (0, 0, 0)),
                  pl.BlockSpec((1, ng * cout), lambda b, i: (0, 0))],
        out_specs=pl.BlockSpec((1, tm, ng * cout), lambda b, i: (b, i, 0)),
        out_shape=jax.ShapeDtypeStruct((bsz, seq, ng * cout), BF16),
        scratch_shapes=[pltpu.VMEM((tm + POOL_PAD, d), BF16),
                        pltpu.VMEM((tm + POOL_PAD, d_pool), BF16),
                        pltpu.VMEM((tm + POOL_PAD, d_pool), BF16)],
        compiler_params=_params(2),
        name="pool_branch",
    )(hx, hx, hm, w_in_b, band, pool_w_b, pool_scale)


def _merge_kernel(h_ref, a_ref, yb_ref, wga_ref, wgb_ref, wao_ref, o_ref):
    h = h_ref[...]
    ga = jax.nn.sigmoid(_dot(h, wga_ref[...]))
    gb = jax.nn.sigmoid(_dot(h, wgb_ref[...]))
    ya = _dot(a_ref[...], wao_ref[...])
    o_ref[...] = (ga * ya + gb * yb_ref[...].astype(F32)).astype(o_ref.dtype)


def _merge(hx2, a2, yb2, w_in_b, w_a_out_b, col0, tm, tj):
    n, d = hx2.shape
    dc = a2.shape[1]
    nj = d // tj
    return pl.pallas_call(
        _merge_kernel,
        grid=(n // tm, nj),
        in_specs=[pl.BlockSpec((tm, d), lambda i, j: (i, 0)),
                  pl.BlockSpec((tm, dc), lambda i, j: (i, 0)),
                  pl.BlockSpec((tm, tj), lambda i, j: (i, j)),
                  pl.BlockSpec((d, tj), lambda i, j: (0, col0 // tj + j)),
                  pl.BlockSpec((d, tj), lambda i, j: (0, col0 // tj + nj + j)),
                  pl.BlockSpec((dc, tj), lambda i, j: (0, j))],
        out_specs=pl.BlockSpec((tm, tj), lambda i, j: (i, j)),
        out_shape=jax.ShapeDtypeStruct((n, d), BF16),
        compiler_params=_params(2),
        name="merge",
    )(hx2, a2, yb2, w_in_b, w_in_b, w_a_out_b)


def _mix_kernel(m_ref, x_ref, lig_ref, lib_ref, wo_ref, g1_ref, b1_ref, rwh_ref, rwl_ref, rb_ref, tri_ref,
                h1_ref, h1p_ref, eid_ref, wts_ref, rank_ref, cnt_ref, carry_ref, z_ref, *, rows):
    step = pl.program_id(0)

    @pl.when(step == 0)
    def _():
        carry_ref[...] = jnp.zeros(carry_ref.shape, F32)

    for r0 in range(0, m_ref.shape[0], rows):
        rs = slice(r0, r0 + rows)
        h0 = _layer_norm(x_ref[rs, :], lig_ref[...], lib_ref[...])
        z_ref[rs, :] = DEEPNORM_ALPHA * h0 + _dot(m_ref[rs, :], wo_ref[...])

    for r0 in range(0, m_ref.shape[0], rows):
        rs = slice(r0, r0 + rows)
        h1 = _layer_norm(z_ref[rs, :], g1_ref[...], b1_ref[...])
        h1_ref[rs, :] = h1
        h_hi = h1.astype(BF16)
        half = h1.shape[1] // 2
        h1p_ref[rs, :] = _pack_pairs(h_hi[:, :half], h_hi[:, half:])

        h_lo = (h1 - h_hi.astype(F32)).astype(BF16)
        nt = (((1,), (1,)), ((), ()))
        rwh = rwh_ref[...]
        logits = (lax.dot_general(rwh, h_hi, nt, preferred_element_type=F32)
                  + lax.dot_general(rwh, h_lo, nt, preferred_element_type=F32)
                  + lax.dot_general(rwl_ref[...], h_hi, nt, preferred_element_type=F32)
                  + rb_ref[...])
        n_exp = logits.shape[0]
        e_iota = lax.broadcasted_iota(I32, logits.shape, 0)
        work = logits
        vals, hots = [], []
        for k in range(TOP_K):
            m = jnp.max(work, axis=0, keepdims=True)
            idx = jnp.min(jnp.where(work == m, e_iota, n_exp), axis=0, keepdims=True)
            hot = e_iota == idx
            vals.append(m)
            hots.append(hot)
            eid_ref[k:k + 1, rs] = idx
            work = jnp.where(hot, -jnp.inf, work)
        exps = [jnp.exp(v - vals[0]) for v in vals]
        denom = exps[0]
        for e in exps[1:]:
            denom = denom + e
        for k in range(TOP_K):
            wts_ref[k:k + 1, rs] = exps[k] / denom

        multi = hots[0]
        for hot in hots[1:]:
            multi = multi | hot
        multi_f = jnp.where(multi, 1.0, 0.0).astype(F32)
        prefix = _dot(multi_f.astype(BF16), tri_ref[...]) + carry_ref[:, 0:1]
        for k in range(TOP_K):
            rank_ref[k:k + 1, rs] = jnp.sum(jnp.where(hots[k], prefix, 0.0), axis=0, keepdims=True).astype(I32)
        carry_ref[...] = carry_ref[...] + jnp.sum(multi_f, axis=1, keepdims=True)
    cnt_ref[...] = carry_ref[...]


def _mix_ln1(m2, x2, ln_in_g, ln_in_b, w_o_b, ln1_g, ln1_b, rw_hi, rw_lo, rb, tm):
    n, d = x2.shape
    n_exp = rw_hi.shape[0]
    rows = min(MIX_HALF, tm)
    tri = (jnp.arange(rows, dtype=I32)[:, None] < jnp.arange(rows, dtype=I32)[None, :]).astype(BF16)
    row = lambda i: (i, 0)
    fixed = lambda i: (0, 0)
    col = lambda i: (0, i)
    return pl.pallas_call(
        functools.partial(_mix_kernel, rows=rows),
        grid=(n // tm,),
        in_specs=[pl.BlockSpec((tm, d), row), pl.BlockSpec((tm, d), row),
                  pl.BlockSpec((1, d), fixed), pl.BlockSpec((1, d), fixed),
                  pl.BlockSpec((d, d), fixed),
                  pl.BlockSpec((1, d), fixed), pl.BlockSpec((1, d), fixed),
                  pl.BlockSpec((n_exp, d), fixed), pl.BlockSpec((n_exp, d), fixed),
                  pl.BlockSpec((n_exp, 1), fixed),
                  pl.BlockSpec((rows, rows), fixed)],
        out_specs=[pl.BlockSpec((tm, d), row), pl.BlockSpec((tm, d // 2), row),
                   pl.BlockSpec((TOP_K, tm), col), pl.BlockSpec((TOP_K, tm), col), pl.BlockSpec((TOP_K, tm), col),
                   pl.BlockSpec((n_exp, 128), fixed)],
        out_shape=[jax.ShapeDtypeStruct((n, d), F32),
                   jax.ShapeDtypeStruct((n, d // 2), U32),
                   jax.ShapeDtypeStruct((TOP_K, n), I32),
                   jax.ShapeDtypeStruct((TOP_K, n), F32),
                   jax.ShapeDtypeStruct((TOP_K, n), I32),
                   jax.ShapeDtypeStruct((n_exp, 128), F32)],
        scratch_shapes=[pltpu.VMEM((n_exp, 128), F32), pltpu.VMEM((tm, d), F32)],
        compiler_params=_params(1),
        name="mix_ln1",
    )(m2, x2, ln_in_g, ln_in_b, w_o_b, ln1_g, ln1_b, rw_hi, rw_lo, rb, tri)


def _row_copy(src_ref, src_row, dst_ref, dst_row, sem):
    return pltpu.make_async_copy(src_ref.at[pl.ds(src_row, 1)], dst_ref.at[pl.ds(dst_row, 1)], sem)


def _subtile(q):
    return pl.ds(pl.multiple_of(q * SLOT_ROWS, SLOT_ROWS), SLOT_ROWS)


def _dispatch_kernel(pos_ref, gl_ref, gn_ref, used_ref, h_ref, xs_ref, zero_ref, sem, zsem, *, tt):
    base = pl.program_id(0) * (TOP_K * tt)

    @pl.when(pl.program_id(0) == 0)
    def _():
        zero_ref[...] = jnp.zeros(zero_ref.shape, zero_ref.dtype)
        n_exp = gl_ref.shape[0]
        n_sub = xs_ref.shape[0] // SLOT_ROWS

        def zero_copy(q):
            return pltpu.make_async_copy(zero_ref, xs_ref.at[_subtile(q)], zsem)

        def group_tail(e, start):
            @pl.when(gn_ref[e] > 0)
            def _():
                c = zero_copy(gl_ref[e])
                c.start() if start else c.wait()

        def unused(q, start):
            c = zero_copy(q)
            c.start() if start else c.wait()

        for start in (True, False):
            lax.fori_loop(0, n_exp, lambda e, c: (group_tail(e, start), c)[1], 0)
            lax.fori_loop(used_ref[1], n_sub, lambda q, c: (unused(q, start), c)[1], 0)

    def issue(t, c):
        for k in range(TOP_K):
            _row_copy(h_ref, t, xs_ref, pos_ref[base + k * tt + t], sem).start(priority=k % 2)
        return c

    lax.fori_loop(0, tt, issue, 0)
    for k in range(TOP_K):
        pltpu.make_async_copy(h_ref, xs_ref.at[pl.ds(0, tt)], sem).wait()


def _dispatch(h1p, pos_tiles, group_last, group_nsub, used, n_slots, tt):
    n, dw = h1p.shape
    grid_spec = pltpu.PrefetchScalarGridSpec(
        num_scalar_prefetch=4,
        grid=(n // tt,),
        in_specs=[pl.BlockSpec((tt, dw), lambda i, *_: (i, 0))],
        out_specs=pl.BlockSpec(memory_space=pl.ANY),
        scratch_shapes=[pltpu.VMEM((SLOT_ROWS, dw), h1p.dtype),
                        pltpu.SemaphoreType.DMA(()), pltpu.SemaphoreType.DMA(())],
    )
    return pl.pallas_call(
        functools.partial(_dispatch_kernel, tt=tt),
        grid_spec=grid_spec,
        out_shape=jax.ShapeDtypeStruct((n_slots, dw), h1p.dtype),
        compiler_params=_params(1),
        name="dispatch",
    )(pos_tiles, group_last, group_nsub, used, h1p)


def _expert_kernel(ge_ref, gs_ref, gn_ref, nu_ref, xs_ref, wg_ref, wl_ref, wd_ref, bg_ref, bl_ref, bd_ref,
                   ys_ref, xb_ref, acc_ref, stage_ref, wgb_ref, wlb_ref, wdb_ref, sem_in, sem_out, *, nj):
    t = pl.program_id(0)
    n_groups = nu_ref[0]
    n_items = n_groups * nj
    c = jnp.maximum(t - 1, 0)
    s = c // nj
    j = c - s * nj
    cast_slot = t % 2
    use_slot = 1 - cast_slot
    d, tn = wgb_ref.shape[1:]
    half = d // 2
    rows = _subtile

    def cast_slice(i):
        r = pl.ds(pl.multiple_of(i * (d // CAST_SLICES), d // CAST_SLICES), d // CAST_SLICES)
        wgb_ref[cast_slot, r, :] = wg_ref[0, r, :].astype(BF16)
        wlb_ref[cast_slot, r, :] = wl_ref[0, r, :].astype(BF16)
        r = pl.ds(pl.multiple_of(i * (tn // CAST_SLICES), tn // CAST_SLICES), tn // CAST_SLICES)
        wdb_ref[cast_slot, r, :] = wd_ref[0, r, :].astype(BF16)

    def cast_range(lo, hi):
        lax.fori_loop(lo, hi, lambda i, z: (cast_slice(i), z)[1], 0)

    def out_copy(first_sub, q):
        return pltpu.make_async_copy(acc_ref.at[rows(q)], ys_ref.at[rows(first_sub + q)], sem_out.at[q])

    def in_copy(first_sub, q, slot):
        return pltpu.make_async_copy(xs_ref.at[rows(first_sub + q)], stage_ref.at[slot], sem_in.at[slot])

    def prefetch(first_sub, nsub):
        lax.fori_loop(0, jnp.minimum(nsub, stage_ref.shape[0]),
                      lambda q, z: (in_copy(first_sub, q, q).start(), z)[1], 0)

    @pl.when(t == 0)
    def _():
        cast_range(0, CAST_SLICES)

    @pl.when((t >= 1) & (t <= n_items))
    def _():
        first_sub = gs_ref[s]
        nsub = gn_ref[s]

        @pl.when(j == 0)
        def _():
            sp = jnp.maximum(s - 1, 0)
            prev_first = gs_ref[sp]
            prev_nsub = jnp.where(s > 0, gn_ref[sp], 0)
            depth = stage_ref.shape[0]

            @pl.when(s == 0)
            def _():
                prefetch(first_sub, nsub)

            def load(q, z):
                slot = q % depth
                in_copy(first_sub, q, slot).wait()
                lo, hi = _unpack_pairs(stage_ref[slot])
                xb_ref[rows(q), 0:half] = lo
                xb_ref[rows(q), half:] = hi

                @pl.when(q + depth < nsub)
                def _():
                    in_copy(first_sub, q + depth, slot).start()

                @pl.when(q < prev_nsub)
                def _():
                    out_copy(prev_first, q).wait()

                acc_ref[rows(q), :] = jnp.broadcast_to(bd_ref[0], (SLOT_ROWS, d))
                return z

            lax.fori_loop(0, nsub, load, 0)
            lax.fori_loop(nsub, prev_nsub, lambda q, z: (out_copy(prev_first, q).wait(), z)[1], 0)

        def run(q, n):
            r = pl.ds(pl.multiple_of(q * SLOT_ROWS, SLOT_ROWS), n * SLOT_ROWS)
            x = xb_ref[r, :]
            gate = _dot(x, wgb_ref[use_slot]) + bg_ref[0]
            lin = _dot(x, wlb_ref[use_slot]) + bl_ref[0]
            gate = jnp.minimum(gate, SWIGLU_LIMIT)
            lin = jnp.clip(lin, -SWIGLU_LIMIT, SWIGLU_LIMIT)
            act = (lin + 1.0) * gate * jax.nn.sigmoid(SWIGLU_ALPHA * gate)
            acc_ref[r, :] += _dot(act.astype(BF16), wdb_ref[use_slot])

            @pl.when(j == nj - 1)
            def _():
                for u in range(n):
                    out_copy(first_sub, q + u).start()

        a = nsub // 3
        r = nsub - 3 * a
        first_n = jnp.where(nsub == 1, 1, jnp.where(r == 0, 3, jnp.where(r == 1, 4, 2)))
        for n in (1, 2, 3, 4):
            @pl.when(first_n == n)
            def _():
                run(0, n)
                for i in range(CAST_SLICES):
                    cast_slice(i)

        lax.fori_loop(0, (nsub - first_n) // 3, lambda p, z: (run(first_n + 3 * p, 3), z)[1], 0)

        @pl.when((j == nj - 1) & (s + 1 < n_groups))
        def _():
            nxt = jnp.minimum(s + 1, n_groups - 1)
            prefetch(gs_ref[nxt], gn_ref[nxt])

        @pl.when(t == n_items)
        def _():
            lax.fori_loop(0, nsub, lambda q, z: (out_copy(first_sub, q).wait(), z)[1], 0)
            acc_ref[rows(0), :] = jnp.zeros((SLOT_ROWS, d), F32)
            n_sub = ys_ref.shape[0] // SLOT_ROWS

            def tail_copy(q):
                return pltpu.make_async_copy(acc_ref.at[rows(0)], ys_ref.at[rows(q)], sem_out.at[0])

            lax.fori_loop(nu_ref[1], n_sub, lambda q, z: (tail_copy(q).start(), z)[1], 0)
            lax.fori_loop(nu_ref[1], n_sub, lambda q, z: (tail_copy(q).wait(), z)[1], 0)


def _experts(xs, group_expert, group_first, group_nsub, used, w_gate_up, b_gate_up, w_down, b_down, tn):
    n_slots, dw = xs.shape
    d = 2 * dw
    n_exp, _, de2 = w_gate_up.shape
    de = de2 // 2
    nj = de // tn
    max_items = group_expert.shape[0] * nj
    rmax = GROUP_SUBTILES * SLOT_ROWS

    def fetched(t, nu):
        w = jnp.minimum(t, nu[0] * nj - 1)
        return w // nj, w % nj

    def computed(t, nu):
        w = jnp.clip(t - 1, 0, nu[0] * nj - 1)
        return w // nj, w % nj

    def wg_map(t, ge, gs, gn, nu):
        g, j = fetched(t, nu)
        return ge[g], 0, j

    def wl_map(t, ge, gs, gn, nu):
        g, j = fetched(t, nu)
        return ge[g], 0, nj + j

    def wd_map(t, ge, gs, gn, nu):
        g, j = fetched(t, nu)
        return ge[g], j, 0

    def bg_map(t, ge, gs, gn, nu):
        g, j = computed(t, nu)
        return ge[g], 0, j

    def bl_map(t, ge, gs, gn, nu):
        g, j = computed(t, nu)
        return ge[g], 0, nj + j

    def bd_map(t, ge, gs, gn, nu):
        g, _ = computed(t, nu)
        return ge[g], 0, 0

    grid_spec = pltpu.PrefetchScalarGridSpec(
        num_scalar_prefetch=4,
        grid=(used[0] * nj + 1,),
        in_specs=[pl.BlockSpec(memory_space=pl.ANY),
                  pl.BlockSpec((1, d, tn), wg_map), pl.BlockSpec((1, d, tn), wl_map),
                  pl.BlockSpec((1, tn, d), wd_map),
                  pl.BlockSpec((1, 1, tn), bg_map), pl.BlockSpec((1, 1, tn), bl_map),
                  pl.BlockSpec((1, 1, d), bd_map)],
        out_specs=pl.BlockSpec(memory_space=pl.ANY),
        scratch_shapes=[pltpu.VMEM((rmax, d), BF16),
                        pltpu.VMEM((rmax, d), F32),
                        pltpu.VMEM((STAGE_DEPTH, SLOT_ROWS, dw), U32),
                        pltpu.VMEM((2, d, tn), BF16), pltpu.VMEM((2, d, tn), BF16), pltpu.VMEM((2, tn, d), BF16),
                        pltpu.SemaphoreType.DMA((STAGE_DEPTH,)),
                        pltpu.SemaphoreType.DMA((GROUP_SUBTILES,))],
    )
    return pl.pallas_call(
        functools.partial(_expert_kernel, nj=nj),
        grid_spec=grid_spec,
        out_shape=jax.ShapeDtypeStruct((n_slots, d), F32),
        compiler_params=_params(1),
        name="experts",
    )(group_expert, group_first, group_nsub, used, xs, w_gate_up, w_gate_up, w_down,
      b_gate_up.reshape(n_exp, 1, de2), b_gate_up.reshape(n_exp, 1, de2), b_down.reshape(n_exp, 1, d))


def _combine_kernel(pos_ref, h_ref, w_ref, g_ref, b_ref, ys_ref, o_ref, buf_ref, sem, *, tt, n_steps):
    i = pl.program_id(0)
    slot = i % 2
    ch = min(COMBINE_CHUNK, tt)

    def issue_rows(step, dst_slot, t0):
        base = step * (TOP_K * tt)
        for r in range(ch):
            for k in range(TOP_K):
                _row_copy(ys_ref, pos_ref[base + k * tt + t0 + r], buf_ref.at[dst_slot, k], t0 + r,
                          sem.at[dst_slot]).start(priority=k % 2)

    def wait_rows(src_slot):
        for k in range(TOP_K):
            pltpu.make_async_copy(ys_ref.at[pl.ds(0, tt)], buf_ref.at[src_slot, k], sem.at[src_slot]).wait()

    @pl.when(i == 0)
    def _():
        lax.fori_loop(0, tt // ch, lambda c, z: (issue_rows(0, 0, c * ch), z)[1], 0)

    wait_rows(slot)
    nxt = lax.rem(i + 1, n_steps)

    def chunk(c, z):
        r0 = pl.multiple_of(c * ch, ch)
        issue_rows(nxt, 1 - slot, r0)
        rs = pl.ds(r0, ch)
        w = w_ref[rs, :]
        y = w[:, 0:1] * buf_ref[slot, 0, rs, :]
        for k in range(1, TOP_K):
            y = y + w[:, k:k + 1] * buf_ref[slot, k, rs, :]
        o_ref[rs, :] = _layer_norm(DEEPNORM_ALPHA * h_ref[rs, :] + y, g_ref[...], b_ref[...])
        return z

    lax.fori_loop(0, tt // ch, chunk, 0)

    @pl.when(i == n_steps - 1)
    def _():
        wait_rows(1 - slot)


def _combine(h1, pos_tiles, wts_t, ys, ln2_g, ln2_b, tt):
    n, d = h1.shape
    n_steps = n // tt
    grid_spec = pltpu.PrefetchScalarGridSpec(
        num_scalar_prefetch=1,
        grid=(n_steps,),
        in_specs=[pl.BlockSpec((tt, d), lambda i, pos: (i, 0)),
                  pl.BlockSpec((tt, TOP_K), lambda i, pos: (i, 0)),
                  pl.BlockSpec((1, d), lambda i, pos: (0, 0)),
                  pl.BlockSpec((1, d), lambda i, pos: (0, 0)),
                  pl.BlockSpec(memory_space=pl.ANY)],
        out_specs=pl.BlockSpec((tt, d), lambda i, pos: (i, 0)),
        scratch_shapes=[pltpu.VMEM((2, TOP_K, tt, d), F32), pltpu.SemaphoreType.DMA((2,))],
    )
    return pl.pallas_call(
        functools.partial(_combine_kernel, tt=tt, n_steps=n_steps),
        grid_spec=grid_spec,
        out_shape=jax.ShapeDtypeStruct((n, d), F32),
        compiler_params=_params(1),
        name="combine",
    )(pos_tiles, h1, wts_t, ln2_g, ln2_b, ys)


def _routing_tables(counts, eid, rank, n_tok):
    n_exp = counts.shape[0]
    e_ids = jnp.arange(n_exp, dtype=I32)
    nsub = (counts + SLOT_ROWS - 1) // SLOT_ROWS
    ngrp = (nsub + GROUP_SUBTILES - 1) // GROUP_SUBTILES
    base = nsub // jnp.maximum(ngrp, 1)
    rem = nsub - base * ngrp
    first_sub = jnp.cumsum(nsub) - nsub
    pos = jnp.sum(jnp.where(eid[..., None] == e_ids, first_sub * SLOT_ROWS, 0), axis=-1) + rank

    n_subtiles = (n_tok * TOP_K) // SLOT_ROWS + n_exp
    max_groups = (n_subtiles + GROUP_SUBTILES - 1) // GROUP_SUBTILES + n_exp
    grp_end = jnp.cumsum(ngrp)
    g_ids = jnp.arange(max_groups, dtype=I32)
    g_exp = jnp.minimum(jnp.sum((grp_end[None, :] <= g_ids[:, None]).astype(I32), axis=1), n_exp - 1)
    pick = lambda tab: jnp.sum(jnp.where(g_exp[:, None] == e_ids[None, :], tab[None, :], 0), axis=1)
    local = g_ids - pick(grp_end - ngrp)
    g_base, g_rem = pick(base), pick(rem)
    g_nsub = g_base + (local < g_rem).astype(I32)
    g_first = pick(first_sub) + local * g_base + jnp.minimum(local, g_rem)
    used = jnp.stack([grp_end[-1], jnp.sum(nsub)]).astype(I32)
    return dict(pos=pos, n_slots=n_subtiles * SLOT_ROWS, used=used,
                group_expert=g_exp.astype(I32), group_first=g_first.astype(I32), group_nsub=g_nsub.astype(I32),
                expert_last=(first_sub + nsub - 1).astype(I32), expert_nsub=nsub.astype(I32))


def _tiles(seq, n_tok):
    return dict(
        ln_rows=min(512, n_tok),
        conv_rows=min(1024, seq), conv_cols=512,
        pool_rows=min(1024, seq),
        merge_rows=min(1024, n_tok), merge_cols=512,
        mix_rows=min(512, n_tok),
        route_rows=min(128, n_tok), dispatch_rows=min(1024, n_tok),
        expert_cols=256,
    )


def kernel(x, meta_tokens, ln_in_g, ln_in_b, w_in, conv_w, w_a_out, pool_w, pool_scale, w_o, ln1_g, ln1_b,
           router_w, router_b, w_gate_up, b_gate_up, w_down, b_down, ln2_g, ln2_b):
    bsz, seq, d = x.shape
    assert w_in.shape[0] == DEPTH and meta_tokens.shape[0] == N_META == HALO
    d_conv = conv_w.shape[-1]
    n_groups, pool_cin, pool_cout = pool_w.shape[1:]
    d_pool = n_groups * pool_cin
    assert n_groups == len(POOL_WINDOWS) and n_groups * pool_cout == d
    n_exp = router_w.shape[-1]
    n_tok = bsz * seq
    t = _tiles(seq, n_tok)
    row = lambda v: v.reshape(1, -1).astype(F32)

    w_in_b = w_in[0].astype(BF16)
    x2 = x.reshape(n_tok, d)
    lig, lib = row(ln_in_g), row(ln_in_b)

    hx2 = _ln_cast(x2, lig, lib, t["ln_rows"])
    hm = _ln_cast(meta_tokens.astype(F32), lig, lib, N_META)
    hx = hx2.reshape(bsz, seq, d)

    a = _conv_branch(hx, hm, w_in_b, conv_w[0], d_conv, t["conv_rows"], min(t["conv_cols"], d_conv))
    yb = _pool_branch(hx, hm, w_in_b, pool_w[0].astype(BF16), row(pool_scale[0]), 3 * d_conv, t["pool_rows"])
    m2 = _merge(hx2, a.reshape(n_tok, d_conv), yb.reshape(n_tok, d), w_in_b, w_a_out[0].astype(BF16),
                3 * d_conv + d_pool, t["merge_rows"], min(t["merge_cols"], d))

    rw_t = router_w[0].T
    rw_hi = rw_t.astype(BF16)
    rw_lo = (rw_t - rw_hi.astype(F32)).astype(BF16)
    h1, h1p, eid, wts, rank, cnt = _mix_ln1(m2, x2, lig, lib, w_o[0].astype(BF16), row(ln1_g[0]), row(ln1_b[0]),
                                            rw_hi, rw_lo, router_b[0].reshape(n_exp, 1), t["mix_rows"])

    rt = _routing_tables(cnt[:, 0].astype(I32), eid, rank, n_tok)
    def pos_tiles_for(rows):
        return rt["pos"].reshape(TOP_K, n_tok // rows, rows).transpose(1, 0, 2).reshape(-1)

    tt = t["route_rows"]
    pos_tiles = pos_tiles_for(tt)
    xs = _dispatch(h1p, pos_tiles_for(t["dispatch_rows"]), rt["expert_last"], rt["expert_nsub"], rt["used"],
                   rt["n_slots"], t["dispatch_rows"])
    ys = _experts(xs, rt["group_expert"], rt["group_first"], rt["group_nsub"], rt["used"],
                  w_gate_up[0], b_gate_up[0], w_down[0], b_down[0], min(t["expert_cols"], w_down.shape[-2]))
    out = _combine(h1, pos_tiles, wts.T, ys, row(ln2_g[0]), row(ln2_b[0]), tt)
    return out.reshape(bsz, seq, d)
```

```python
import functools

import jax
import jax.numpy as jnp
from jax import lax
from jax.experimental import pallas as pl
from jax.experimental.pallas import tpu as pltpu

F32 = jnp.float32
BF16 = jnp.bfloat16
I32 = jnp.int32
U32 = jnp.uint32

N_META = 16
CONV_K = 3
POOL_WINDOWS = (2, 4, 8, 16)
TOP_K = 4
SWIGLU_LIMIT = 7.0
SWIGLU_ALPHA = 1.702
LN_EPS = 1e-5
DEPTH = 1
DEEPNORM_ALPHA = (2.0 * DEPTH) ** 0.25

HALO = 16
POOL_PAD = 128
POOL_CHUNK = 256
SLOT_ROWS = 256
GROUP_SUBTILES = 9
CAST_SLICES = 8
STAGE_DEPTH = 6
MIX_HALF = 256
COMBINE_CHUNK = 16
V7X_VMEM_LIMIT = 58 * 1024 * 1024

_ARB = "arbitrary"


def _params(n_axes):
    return pltpu.CompilerParams(dimension_semantics=(_ARB,) * n_axes, vmem_limit_bytes=V7X_VMEM_LIMIT)


def _layer_norm(x, g, b):
    mu = jnp.mean(x, axis=-1, keepdims=True)
    xc = x - mu
    var = jnp.mean(xc * xc, axis=-1, keepdims=True)
    return xc * lax.rsqrt(var + LN_EPS) * g + b


def _dot(a, b):
    return jnp.dot(a, b, preferred_element_type=F32)


def _pack_pairs(lo_bf16, hi_bf16):
    lo = lax.bitcast_convert_type(lo_bf16.astype(F32), U32) >> 16
    hi = lax.bitcast_convert_type(hi_bf16.astype(F32), U32) & jnp.uint32(0xFFFF0000)
    return hi | lo


def _unpack_pairs(words):
    lo = lax.bitcast_convert_type(words << 16, F32).astype(BF16)
    hi = lax.bitcast_convert_type(words & jnp.uint32(0xFFFF0000), F32).astype(BF16)
    return lo, hi


def _ln_cast_kernel(x_ref, g_ref, b_ref, o_ref):
    o_ref[...] = _layer_norm(x_ref[...], g_ref[...], b_ref[...]).astype(o_ref.dtype)


def _ln_cast(x2d, g, b, tr):
    n, d = x2d.shape
    return pl.pallas_call(
        _ln_cast_kernel,
        grid=(n // tr,),
        in_specs=[pl.BlockSpec((tr, d), lambda i: (i, 0)),
                  pl.BlockSpec((1, d), lambda i: (0, 0)),
                  pl.BlockSpec((1, d), lambda i: (0, 0))],
        out_specs=pl.BlockSpec((tr, d), lambda i: (i, 0)),
        out_shape=jax.ShapeDtypeStruct((n, d), BF16),
        compiler_params=_params(1),
        name="ln_cast",
    )(x2d, g, b)


def _conv_kernel(h_ref, halo_ref, hm_ref, wb_ref, wc_ref, wu_ref, cw_ref, o_ref, he_ref, cu_ref, *, tm):
    i = pl.program_id(1)
    j = pl.program_id(2)

    @pl.when(j == 0)
    def _():
        he_ref[HALO:, :] = h_ref[0]

    @pl.when((j == 0) & (i == 0))
    def _():
        he_ref[0:HALO, :] = hm_ref[...]

    @pl.when((j == 0) & (i > 0))
    def _():
        he_ref[0:HALO, :] = halo_ref[0]

    he = he_ref[...]
    cu_ref[...] = _dot(he, wc_ref[...]) * _dot(he, wu_ref[...])
    bg = _dot(he_ref[HALO:, :], wb_ref[...])
    cw = cw_ref[...]
    y = cw[CONV_K - 1:CONV_K, :] * cu_ref[HALO:HALO + tm, :]
    for k in range(CONV_K - 1):
        s = HALO - (CONV_K - 1) + k
        y = y + cw[k:k + 1, :] * cu_ref[s:s + tm, :]
    o_ref[0] = (bg * y).astype(o_ref.dtype)


def _conv_branch(hx, hm, w_in_b, conv_w, d_conv, tm, tc):
    bsz, seq, d = hx.shape
    nc = d_conv // tc
    hb = tm // HALO
    return pl.pallas_call(
        functools.partial(_conv_kernel, tm=tm),
        grid=(bsz, seq // tm, nc),
        in_specs=[pl.BlockSpec((1, tm, d), lambda b, i, j: (b, i, 0)),
                  pl.BlockSpec((1, HALO, d), lambda b, i, j: (b, jnp.maximum(i * hb - 1, 0), 0)),
                  pl.BlockSpec((HALO, d), lambda b, i, j: (0, 0)),
                  pl.BlockSpec((d, tc), lambda b, i, j: (0, j)),
                  pl.BlockSpec((d, tc), lambda b, i, j: (0, nc + j)),
                  pl.BlockSpec((d, tc), lambda b, i, j: (0, 2 * nc + j)),
                  pl.BlockSpec((CONV_K, tc), lambda b, i, j: (0, j))],
        out_specs=pl.BlockSpec((1, tm, tc), lambda b, i, j: (b, i, j)),
        out_shape=jax.ShapeDtypeStruct((bsz, seq, d_conv), BF16),
        scratch_shapes=[pltpu.VMEM((tm + HALO, d), BF16), pltpu.VMEM((tm + HALO, tc), F32)],
        compiler_params=_params(3),
        name="conv_branch",
    )(hx, hx, hm, w_in_b, w_in_b, w_in_b, conv_w)


def _pool_kernel(h_ref, halo_ref, hm_ref, wv_ref, band_ref, pw_ref, ps_ref, o_ref, he_ref, vh_ref, vl_ref, *, tm):
    i = pl.program_id(1)
    lead = POOL_PAD - HALO
    he_ref[0:lead, :] = jnp.zeros((lead, he_ref.shape[1]), he_ref.dtype)
    he_ref[POOL_PAD:, :] = h_ref[0]

    @pl.when(i == 0)
    def _():
        he_ref[lead:POOL_PAD, :] = hm_ref[...]

    @pl.when(i > 0)
    def _():
        he_ref[lead:POOL_PAD, :] = halo_ref[0]

    v = _dot(he_ref[...], wv_ref[...])
    v_hi = v.astype(BF16)
    vh_ref[...] = v_hi
    vl_ref[...] = (v - v_hi.astype(F32)).astype(BF16)
    ng, cin, cout = pw_ref.shape
    rows = min(POOL_CHUNK, tm)
    for c in range(tm // rows):
        r0 = c * rows
        for g in range(ng):
            band = band_ref[g]
            win = (slice(r0, r0 + rows + POOL_PAD), slice(g * cin, (g + 1) * cin))
            pooled = _dot(band, vh_ref[win]) + _dot(band, vl_ref[win])
            y = _dot(pooled.astype(BF16), pw_ref[g]) * ps_ref[:, g * cout:(g + 1) * cout]
            o_ref[0, r0:r0 + rows, g * cout:(g + 1) * cout] = y.astype(o_ref.dtype)


def _pool_band(rows):
    t = jnp.arange(rows, dtype=I32)[:, None] + POOL_PAD
    s = jnp.arange(rows + POOL_PAD, dtype=I32)[None, :]
    bands = []
    for w in POOL_WINDOWS:
        inside = ((s <= t) & (s > t - w)).astype(F32) / w
        bands.append(inside - (s == t).astype(F32))
    return jnp.stack(bands).astype(BF16)


def _pool_branch(hx, hm, w_in_b, pool_w_b, pool_scale, col0, tm):
    bsz, seq, d = hx.shape
    ng, cin, cout = pool_w_b.shape
    d_pool = ng * cin
    hb = tm // HALO
    band = _pool_band(min(POOL_CHUNK, tm))
    return pl.pallas_call(
        functools.partial(_pool_kernel, tm=tm),
        grid=(bsz, seq // tm),
        in_specs=[pl.BlockSpec((1, tm, d), lambda b, i: (b, i, 0)),
                  pl.BlockSpec((1, HALO, d), lambda b, i: (b, jnp.maximum(i * hb - 1, 0), 0)),
                  pl.BlockSpec((HALO, d), lambda b, i: (0, 0)),
                  pl.BlockSpec((d, d_pool), lambda b, i: (0, col0 // d_pool)),
                  pl.BlockSpec(band.shape, lambda b, i: (0, 0, 0)),
                  pl.BlockSpec((ng, cin, cout), lambda b, i: (0, 0, 0)),
                  pl.BlockSpec((1, ng * cout), lambda b, i: (0, 0))],
        out_specs=pl.BlockSpec((1, tm, ng * cout), lambda b, i: (b, i, 0)),
        out_shape=jax.ShapeDtypeStruct((bsz, seq, ng * cout), BF16),
        scratch_shapes=[pltpu.VMEM((tm + POOL_PAD, d), BF16),
                        pltpu.VMEM((tm + POOL_PAD, d_pool), BF16),
                        pltpu.VMEM((tm + POOL_PAD, d_pool), BF16)],
        compiler_params=_params(2),
        name="pool_branch",
    )(hx, hx, hm, w_in_b, band, pool_w_b, pool_scale)


def _merge_kernel(h_ref, a_ref, yb_ref, wga_ref, wgb_ref, wao_ref, o_ref):
    h = h_ref[...]
    ga = jax.nn.sigmoid(_dot(h, wga_ref[...]))
    gb = jax.nn.sigmoid(_dot(h, wgb_ref[...]))
    ya = _dot(a_ref[...], wao_ref[...])
    o_ref[...] = (ga * ya + gb * yb_ref[...].astype(F32)).astype(o_ref.dtype)


def _merge(hx2, a2, yb2, w_in_b, w_a_out_b, col0, tm, tj):
    n, d = hx2.shape
    dc = a2.shape[1]
    nj = d // tj
    return pl.pallas_call(
        _merge_kernel,
        grid=(n // tm, nj),
        in_specs=[pl.BlockSpec((tm, d), lambda i, j: (i, 0)),
                  pl.BlockSpec((tm, dc), lambda i, j: (i, 0)),
                  pl.BlockSpec((tm, tj), lambda i, j: (i, j)),
                  pl.BlockSpec((d, tj), lambda i, j: (0, col0 // tj + j)),
                  pl.BlockSpec((d, tj), lambda i, j: (0, col0 // tj + nj + j)),
                  pl.BlockSpec((dc, tj), lambda i, j: (0, j))],
        out_specs=pl.BlockSpec((tm, tj), lambda i, j: (i, j)),
        out_shape=jax.ShapeDtypeStruct((n, d), BF16),
        compiler_params=_params(2),
        name="merge",
    )(hx2, a2, yb2, w_in_b, w_in_b, w_a_out_b)


def _mix_kernel(m_ref, x_ref, lig_ref, lib_ref, wo_ref, g1_ref, b1_ref, rwh_ref, rwl_ref, rb_ref, tri_ref,
                h1_ref, h1p_ref, eid_ref, wts_ref, rank_ref, cnt_ref, carry_ref, z_ref, *, rows):
    step = pl.program_id(0)

    @pl.when(step == 0)
    def _():
        carry_ref[...] = jnp.zeros(carry_ref.shape, F32)

    for r0 in range(0, m_ref.shape[0], rows):
        rs = slice(r0, r0 + rows)
        h0 = _layer_norm(x_ref[rs, :], lig_ref[...], lib_ref[...])
        z_ref[rs, :] = DEEPNORM_ALPHA * h0 + _dot(m_ref[rs, :], wo_ref[...])

    for r0 in range(0, m_ref.shape[0], rows):
        rs = slice(r0, r0 + rows)
        h1 = _layer_norm(z_ref[rs, :], g1_ref[...], b1_ref[...])
        h1_ref[rs, :] = h1
        h_hi = h1.astype(BF16)
        half = h1.shape[1] // 2
        h1p_ref[rs, :] = _pack_pairs(h_hi[:, :half], h_hi[:, half:])

        h_lo = (h1 - h_hi.astype(F32)).astype(BF16)
        nt = (((1,), (1,)), ((), ()))
        rwh = rwh_ref[...]
        logits = (lax.dot_general(rwh, h_hi, nt, preferred_element_type=F32)
                  + lax.dot_general(rwh, h_lo, nt, preferred_element_type=F32)
                  + lax.dot_general(rwl_ref[...], h_hi, nt, preferred_element_type=F32)
                  + rb_ref[...])
        n_exp = logits.shape[0]
        e_iota = lax.broadcasted_iota(I32, logits.shape, 0)
        work = logits
        vals, hots = [], []
        for k in range(TOP_K):
            m = jnp.max(work, axis=0, keepdims=True)
            idx = jnp.min(jnp.where(work == m, e_iota, n_exp), axis=0, keepdims=True)
            hot = e_iota == idx
            vals.append(m)
            hots.append(hot)
            eid_ref[k:k + 1, rs] = idx
            work = jnp.where(hot, -jnp.inf, work)
        exps = [jnp.exp(v - vals[0]) for v in vals]
        denom = exps[0]
        for e in exps[1:]:
            denom = denom + e
        for k in range(TOP_K):
            wts_ref[k:k + 1, rs] = exps[k] / denom

        multi = hots[0]
        for hot in hots[1:]:
            multi = multi | hot
        multi_f = jnp.where(multi, 1.0, 0.0).astype(F32)
        prefix = _dot(multi_f.astype(BF16), tri_ref[...]) + carry_ref[:, 0:1]
        for k in range(TOP_K):
            rank_ref[k:k + 1, rs] = jnp.sum(jnp.where(hots[k], prefix, 0.0), axis=0, keepdims=True).astype(I32)
        carry_ref[...] = carry_ref[...] + jnp.sum(multi_f, axis=1, keepdims=True)
    cnt_ref[...] = carry_ref[...]


def _mix_ln1(m2, x2, ln_in_g, ln_in_b, w_o_b, ln1_g, ln1_b, rw_hi, rw_lo, rb, tm):
    n, d = x2.shape
    n_exp = rw_hi.shape[0]
    rows = min(MIX_HALF, tm)
    tri = (jnp.arange(rows, dtype=I32)[:, None] < jnp.arange(rows, dtype=I32)[None, :]).astype(BF16)
    row = lambda i: (i, 0)
    fixed = lambda i: (0, 0)
    col = lambda i: (0, i)
    return pl.pallas_call(
        functools.partial(_mix_kernel, rows=rows),
        grid=(n // tm,),
        in_specs=[pl.BlockSpec((tm, d), row), pl.BlockSpec((tm, d), row),
                  pl.BlockSpec((1, d), fixed), pl.BlockSpec((1, d), fixed),
                  pl.BlockSpec((d, d), fixed),
                  pl.BlockSpec((1, d), fixed), pl.BlockSpec((1, d), fixed),
                  pl.BlockSpec((n_exp, d), fixed), pl.BlockSpec((n_exp, d), fixed),
                  pl.BlockSpec((n_exp, 1), fixed),
                  pl.BlockSpec((rows, rows), fixed)],
        out_specs=[pl.BlockSpec((tm, d), row), pl.BlockSpec((tm, d // 2), row),
                   pl.BlockSpec((TOP_K, tm), col), pl.BlockSpec((TOP_K, tm), col), pl.BlockSpec((TOP_K, tm), col),
                   pl.BlockSpec((n_exp, 128), fixed)],
        out_shape=[jax.ShapeDtypeStruct((n, d), F32),
                   jax.ShapeDtypeStruct((n, d // 2), U32),
                   jax.ShapeDtypeStruct((TOP_K, n), I32),
                   jax.ShapeDtypeStruct((TOP_K, n), F32),
                   jax.ShapeDtypeStruct((TOP_K, n), I32),
                   jax.ShapeDtypeStruct((n_exp, 128), F32)],
        scratch_shapes=[pltpu.VMEM((n_exp, 128), F32), pltpu.VMEM((tm, d), F32)],
        compiler_params=_params(1),
        name="mix_ln1",
    )(m2, x2, ln_in_g, ln_in_b, w_o_b, ln1_g, ln1_b, rw_hi, rw_lo, rb, tri)


def _row_copy(src_ref, src_row, dst_ref, dst_row, sem):
    return pltpu.make_async_copy(src_ref.at[pl.ds(src_row, 1)], dst_ref.at[pl.ds(dst_row, 1)], sem)


def _subtile(q):
    return pl.ds(pl.multiple_of(q * SLOT_ROWS, SLOT_ROWS), SLOT_ROWS)


def _dispatch_kernel(pos_ref, gl_ref, gn_ref, used_ref, h_ref, xs_ref, zero_ref, sem, zsem, *, tt):
    base = pl.program_id(0) * (TOP_K * tt)

    @pl.when(pl.program_id(0) == 0)
    def _():
        zero_ref[...] = jnp.zeros(zero_ref.shape, zero_ref.dtype)
        n_exp = gl_ref.shape[0]
        n_sub = xs_ref.shape[0] // SLOT_ROWS

        def zero_copy(q):
            return pltpu.make_async_copy(zero_ref, xs_ref.at[_subtile(q)], zsem)

        def group_tail(e, start):
            @pl.when(gn_ref[e] > 0)
            def _():
                c = zero_copy(gl_ref[e])
                c.start() if start else c.wait()

        def unused(q, start):
            c = zero_copy(q)
            c.start() if start else c.wait()

        for start in (True, False):
            lax.fori_loop(0, n_exp, lambda e, c: (group_tail(e, start), c)[1], 0)
            lax.fori_loop(used_ref[1], n_sub, lambda q, c: (unused(q, start), c)[1], 0)

    def issue(t, c):
        for k in range(TOP_K):
            _row_copy(h_ref, t, xs_ref, pos_ref[base + k * tt + t], sem).start(priority=k % 2)
        return c

    lax.fori_loop(0, tt, issue, 0)
    for k in range(TOP_K):
        pltpu.make_async_copy(h_ref, xs_ref.at[pl.ds(0, tt)], sem).wait()


def _dispatch(h1p, pos_tiles, group_last, group_nsub, used, n_slots, tt):
    n, dw = h1p.shape
    grid_spec = pltpu.PrefetchScalarGridSpec(
        num_scalar_prefetch=4,
        grid=(n // tt,),
        in_specs=[pl.BlockSpec((tt, dw), lambda i, *_: (i, 0))],
        out_specs=pl.BlockSpec(memory_space=pl.ANY),
        scratch_shapes=[pltpu.VMEM((SLOT_ROWS, dw), h1p.dtype),
                        pltpu.SemaphoreType.DMA(()), pltpu.SemaphoreType.DMA(())],
    )
    return pl.pallas_call(
        functools.partial(_dispatch_kernel, tt=tt),
        grid_spec=grid_spec,
        out_shape=jax.ShapeDtypeStruct((n_slots, dw), h1p.dtype),
        compiler_params=_params(1),
        name="dispatch",
    )(pos_tiles, group_last, group_nsub, used, h1p)


def _expert_kernel(ge_ref, gs_ref, gn_ref, nu_ref, xs_ref, wg_ref, wl_ref, wd_ref, bg_ref, bl_ref, bd_ref,
                   ys_ref, xb_ref, acc_ref, stage_ref, wgb_ref, wlb_ref, wdb_ref, sem_in, sem_out, *, nj):
    t = pl.program_id(0)
    n_groups = nu_ref[0]
    n_items = n_groups * nj
    c = jnp.maximum(t - 1, 0)
    s = c // nj
    j = c - s * nj
    cast_slot = t % 2
    use_slot = 1 - cast_slot
    d, tn = wgb_ref.shape[1:]
    half = d // 2
    rows = _subtile

    def cast_slice(i):
        r = pl.ds(pl.multiple_of(i * (d // CAST_SLICES), d // CAST_SLICES), d // CAST_SLICES)
        wgb_ref[cast_slot, r, :] = wg_ref[0, r, :].astype(BF16)
        wlb_ref[cast_slot, r, :] = wl_ref[0, r, :].astype(BF16)
        r = pl.ds(pl.multiple_of(i * (tn // CAST_SLICES), tn // CAST_SLICES), tn // CAST_SLICES)
        wdb_ref[cast_slot, r, :] = wd_ref[0, r, :].astype(BF16)

    def cast_range(lo, hi):
        lax.fori_loop(lo, hi, lambda i, z: (cast_slice(i), z)[1], 0)

    def out_copy(first_sub, q):
        return pltpu.make_async_copy(acc_ref.at[rows(q)], ys_ref.at[rows(first_sub + q)], sem_out.at[q])

    def in_copy(first_sub, q, slot):
        return pltpu.make_async_copy(xs_ref.at[rows(first_sub + q)], stage_ref.at[slot], sem_in.at[slot])

    def prefetch(first_sub, nsub):
        lax.fori_loop(0, jnp.minimum(nsub, stage_ref.shape[0]),
                      lambda q, z: (in_copy(first_sub, q, q).start(), z)[1], 0)

    @pl.when(t == 0)
    def _():
        cast_range(0, CAST_SLICES)

    @pl.when((t >= 1) & (t <= n_items))
    def _():
        first_sub = gs_ref[s]
        nsub = gn_ref[s]

        @pl.when(j == 0)
        def _():
            sp = jnp.maximum(s - 1, 0)
            prev_first = gs_ref[sp]
            prev_nsub = jnp.where(s > 0, gn_ref[sp], 0)
            depth = stage_ref.shape[0]

            @pl.when(s == 0)
            def _():
                prefetch(first_sub, nsub)

            def load(q, z):
                slot = q % depth
                in_copy(first_sub, q, slot).wait()
                lo, hi = _unpack_pairs(stage_ref[slot])
                xb_ref[rows(q), 0:half] = lo
                xb_ref[rows(q), half:] = hi

                @pl.when(q + depth < nsub)
                def _():
                    in_copy(first_sub, q + depth, slot).start()

                @pl.when(q < prev_nsub)
                def _():
                    out_copy(prev_first, q).wait()

                acc_ref[rows(q), :] = jnp.broadcast_to(bd_ref[0], (SLOT_ROWS, d))
                return z

            lax.fori_loop(0, nsub, load, 0)
            lax.fori_loop(nsub, prev_nsub, lambda q, z: (out_copy(prev_first, q).wait(), z)[1], 0)

        def run(q, n):
            r = pl.ds(pl.multiple_of(q * SLOT_ROWS, SLOT_ROWS), n * SLOT_ROWS)
            x = xb_ref[r, :]
            gate = _dot(x, wgb_ref[use_slot]) + bg_ref[0]
            lin = _dot(x, wlb_ref[use_slot]) + bl_ref[0]
            gate = jnp.minimum(gate, SWIGLU_LIMIT)
            lin = jnp.clip(lin, -SWIGLU_LIMIT, SWIGLU_LIMIT)
            act = (lin + 1.0) * gate * jax.nn.sigmoid(SWIGLU_ALPHA * gate)
            acc_ref[r, :] += _dot(act.astype(BF16), wdb_ref[use_slot])

            @pl.when(j == nj - 1)
            def _():
                for u in range(n):
                    out_copy(first_sub, q + u).start()

        a = nsub // 3
        r = nsub - 3 * a
        two_quads = (r == 2) & (a >= 2)
        first_n = jnp.where(nsub == 1, 1, jnp.where(r == 0, 3, jnp.where((r == 1) | two_quads, 4, 2)))
        for n in (1, 2, 3, 4):
            @pl.when(first_n == n)
            def _():
                run(0, n)
                for i in range(CAST_SLICES):
                    cast_slice(i)

        @pl.when(two_quads)
        def _():
            run(4, 4)

        done = first_n + jnp.where(two_quads, 4, 0)
        lax.fori_loop(0, (nsub - done) // 3, lambda p, z: (run(done + 3 * p, 3), z)[1], 0)

        @pl.when((j == nj - 1) & (s + 1 < n_groups))
        def _():
            nxt = jnp.minimum(s + 1, n_groups - 1)
            prefetch(gs_ref[nxt], gn_ref[nxt])

        @pl.when(t == n_items)
        def _():
            lax.fori_loop(0, nsub, lambda q, z: (out_copy(first_sub, q).wait(), z)[1], 0)
            acc_ref[rows(0), :] = jnp.zeros((SLOT_ROWS, d), F32)
            n_sub = ys_ref.shape[0] // SLOT_ROWS

            def tail_copy(q):
                return pltpu.make_async_copy(acc_ref.at[rows(0)], ys_ref.at[rows(q)], sem_out.at[0])

            lax.fori_loop(nu_ref[1], n_sub, lambda q, z: (tail_copy(q).start(), z)[1], 0)
            lax.fori_loop(nu_ref[1], n_sub, lambda q, z: (tail_copy(q).wait(), z)[1], 0)


def _experts(xs, group_expert, group_first, group_nsub, used, w_gate_up, b_gate_up, w_down, b_down, tn):
    n_slots, dw = xs.shape
    d = 2 * dw
    n_exp, _, de2 = w_gate_up.shape
    de = de2 // 2
    nj = de // tn
    max_items = group_expert.shape[0] * nj
    rmax = GROUP_SUBTILES * SLOT_ROWS

    def fetched(t, nu):
        w = jnp.minimum(t, nu[0] * nj - 1)
        return w // nj, w % nj

    def computed(t, nu):
        w = jnp.clip(t - 1, 0, nu[0] * nj - 1)
        return w // nj, w % nj

    def wg_map(t, ge, gs, gn, nu):
        g, j = fetched(t, nu)
        return ge[g], 0, j

    def wl_map(t, ge, gs, gn, nu):
        g, j = fetched(t, nu)
        return ge[g], 0, nj + j

    def wd_map(t, ge, gs, gn, nu):
        g, j = fetched(t, nu)
        return ge[g], j, 0

    def bg_map(t, ge, gs, gn, nu):
        g, j = computed(t, nu)
        return ge[g], 0, j

    def bl_map(t, ge, gs, gn, nu):
        g, j = computed(t, nu)
        return ge[g], 0, nj + j

    def bd_map(t, ge, gs, gn, nu):
        g, _ = computed(t, nu)
        return ge[g], 0, 0

    grid_spec = pltpu.PrefetchScalarGridSpec(
        num_scalar_prefetch=4,
        grid=(used[0] * nj + 1,),
        in_specs=[pl.BlockSpec(memory_space=pl.ANY),
                  pl.BlockSpec((1, d, tn), wg_map), pl.BlockSpec((1, d, tn), wl_map),
                  pl.BlockSpec((1, tn, d), wd_map),
                  pl.BlockSpec((1, 1, tn), bg_map), pl.BlockSpec((1, 1, tn), bl_map),
                  pl.BlockSpec((1, 1, d), bd_map)],
        out_specs=pl.BlockSpec(memory_space=pl.ANY),
        scratch_shapes=[pltpu.VMEM((rmax, d), BF16),
                        pltpu.VMEM((rmax, d), F32),
                        pltpu.VMEM((STAGE_DEPTH, SLOT_ROWS, dw), U32),
                        pltpu.VMEM((2, d, tn), BF16), pltpu.VMEM((2, d, tn), BF16), pltpu.VMEM((2, tn, d), BF16),
                        pltpu.SemaphoreType.DMA((STAGE_DEPTH,)),
                        pltpu.SemaphoreType.DMA((GROUP_SUBTILES,))],
    )
    return pl.pallas_call(
        functools.partial(_expert_kernel, nj=nj),
        grid_spec=grid_spec,
        out_shape=jax.ShapeDtypeStruct((n_slots, d), F32),
        compiler_params=_params(1),
        name="experts",
    )(group_expert, group_first, group_nsub, used, xs, w_gate_up, w_gate_up, w_down,
      b_gate_up.reshape(n_exp, 1, de2), b_gate_up.reshape(n_exp, 1, de2), b_down.reshape(n_exp, 1, d))


def _combine_kernel(pos_ref, h_ref, w_ref, g_ref, b_ref, ys_ref, o_ref, buf_ref, sem, *, tt, n_steps):
    i = pl.program_id(0)
    slot = i % 2
    ch = min(COMBINE_CHUNK, tt)

    def issue_rows(step, dst_slot, t0):
        base = step * (TOP_K * tt)
        for r in range(ch):
            for k in range(TOP_K):
                _row_copy(ys_ref, pos_ref[base + k * tt + t0 + r], buf_ref.at[dst_slot, k], t0 + r,
                          sem.at[dst_slot]).start(priority=k % 2)

    def wait_rows(src_slot):
        for k in range(TOP_K):
            pltpu.make_async_copy(ys_ref.at[pl.ds(0, tt)], buf_ref.at[src_slot, k], sem.at[src_slot]).wait()

    @pl.when(i == 0)
    def _():
        lax.fori_loop(0, tt // ch, lambda c, z: (issue_rows(0, 0, c * ch), z)[1], 0)

    wait_rows(slot)

    @pl.when(i + 1 < n_steps)
    def _():
        lax.fori_loop(0, tt // ch, lambda c, z: (issue_rows(i + 1, 1 - slot, c * ch), z)[1], 0)

    w = w_ref[...]
    y = DEEPNORM_ALPHA * h_ref[...]
    for k in range(TOP_K):
        y = y + w[:, k:k + 1] * buf_ref[slot, k]
    o_ref[...] = _layer_norm(y, g_ref[...], b_ref[...])


def _combine(h1, pos_tiles, wts_t, ys, ln2_g, ln2_b, tt):
    n, d = h1.shape
    n_steps = n // tt
    grid_spec = pltpu.PrefetchScalarGridSpec(
        num_scalar_prefetch=1,
        grid=(n_steps,),
        in_specs=[pl.BlockSpec((tt, d), lambda i, pos: (i, 0)),
                  pl.BlockSpec((tt, TOP_K), lambda i, pos: (i, 0)),
                  pl.BlockSpec((1, d), lambda i, pos: (0, 0)),
                  pl.BlockSpec((1, d), lambda i, pos: (0, 0)),
                  pl.BlockSpec(memory_space=pl.ANY)],
        out_specs=pl.BlockSpec((tt, d), lambda i, pos: (i, 0)),
        scratch_shapes=[pltpu.VMEM((2, TOP_K, tt, d), F32), pltpu.SemaphoreType.DMA((2,))],
    )
    return pl.pallas_call(
        functools.partial(_combine_kernel, tt=tt, n_steps=n_steps),
        grid_spec=grid_spec,
        out_shape=jax.ShapeDtypeStruct((n, d), F32),
        compiler_params=_params(1),
        name="combine",
    )(pos_tiles, h1, wts_t, ln2_g, ln2_b, ys)


def _routing_tables(counts, eid, rank, n_tok):
    n_exp = counts.shape[0]
    e_ids = jnp.arange(n_exp, dtype=I32)
    nsub = (counts + SLOT_ROWS - 1) // SLOT_ROWS
    ngrp = (nsub + GROUP_SUBTILES - 1) // GROUP_SUBTILES
    base = nsub // jnp.maximum(ngrp, 1)
    rem = nsub - base * ngrp
    first_sub = jnp.cumsum(nsub) - nsub
    pos = jnp.sum(jnp.where(eid[..., None] == e_ids, first_sub * SLOT_ROWS, 0), axis=-1) + rank

    n_subtiles = (n_tok * TOP_K) // SLOT_ROWS + n_exp
    max_groups = (n_subtiles + GROUP_SUBTILES - 1) // GROUP_SUBTILES + n_exp
    grp_end = jnp.cumsum(ngrp)
    g_ids = jnp.arange(max_groups, dtype=I32)
    g_exp = jnp.minimum(jnp.sum((grp_end[None, :] <= g_ids[:, None]).astype(I32), axis=1), n_exp - 1)
    pick = lambda tab: jnp.sum(jnp.where(g_exp[:, None] == e_ids[None, :], tab[None, :], 0), axis=1)
    local = g_ids - pick(grp_end - ngrp)
    g_base, g_rem = pick(base), pick(rem)
    g_nsub = g_base + (local < g_rem).astype(I32)
    g_first = pick(first_sub) + local * g_base + jnp.minimum(local, g_rem)
    used = jnp.stack([grp_end[-1], jnp.sum(nsub)]).astype(I32)
    return dict(pos=pos, n_slots=n_subtiles * SLOT_ROWS, used=used,
                group_expert=g_exp.astype(I32), group_first=g_first.astype(I32), group_nsub=g_nsub.astype(I32),
                expert_last=(first_sub + nsub - 1).astype(I32), expert_nsub=nsub.astype(I32))


def _tiles(seq, n_tok):
    return dict(
        ln_rows=min(512, n_tok),
        conv_rows=min(1024, seq), conv_cols=512,
        pool_rows=min(1024, seq),
        merge_rows=min(1024, n_tok), merge_cols=512,
        mix_rows=min(512, n_tok),
        route_rows=min(128, n_tok), dispatch_rows=min(1024, n_tok),
        expert_cols=256,
    )


def kernel(x, meta_tokens, ln_in_g, ln_in_b, w_in, conv_w, w_a_out, pool_w, pool_scale, w_o, ln1_g, ln1_b,
           router_w, router_b, w_gate_up, b_gate_up, w_down, b_down, ln2_g, ln2_b):
    bsz, seq, d = x.shape
    assert w_in.shape[0] == DEPTH and meta_tokens.shape[0] == N_META == HALO
    d_conv = conv_w.shape[-1]
    n_groups, pool_cin, pool_cout = pool_w.shape[1:]
    d_pool = n_groups * pool_cin
    assert n_groups == len(POOL_WINDOWS) and n_groups * pool_cout == d
    n_exp = router_w.shape[-1]
    n_tok = bsz * seq
    t = _tiles(seq, n_tok)
    row = lambda v: v.reshape(1, -1).astype(F32)

    w_in_b = w_in[0].astype(BF16)
    x2 = x.reshape(n_tok, d)
    lig, lib = row(ln_in_g), row(ln_in_b)

    hx2 = _ln_cast(x2, lig, lib, t["ln_rows"])
    hm = _ln_cast(meta_tokens.astype(F32), lig, lib, N_META)
    hx = hx2.reshape(bsz, seq, d)

    a = _conv_branch(hx, hm, w_in_b, conv_w[0], d_conv, t["conv_rows"], min(t["conv_cols"], d_conv))
    yb = _pool_branch(hx, hm, w_in_b, pool_w[0].astype(BF16), row(pool_scale[0]), 3 * d_conv, t["pool_rows"])
    m2 = _merge(hx2, a.reshape(n_tok, d_conv), yb.reshape(n_tok, d), w_in_b, w_a_out[0].astype(BF16),
                3 * d_conv + d_pool, t["merge_rows"], min(t["merge_cols"], d))

    rw_t = router_w[0].T
    rw_hi = rw_t.astype(BF16)
    rw_lo = (rw_t - rw_hi.astype(F32)).astype(BF16)
    h1, h1p, eid, wts, rank, cnt = _mix_ln1(m2, x2, lig, lib, w_o[0].astype(BF16), row(ln1_g[0]), row(ln1_b[0]),
                                            rw_hi, rw_lo, router_b[0].reshape(n_exp, 1), t["mix_rows"])

    rt = _routing_tables(cnt[:, 0].astype(I32), eid, rank, n_tok)
    def pos_tiles_for(rows):
        return rt["pos"].reshape(TOP_K, n_tok // rows, rows).transpose(1, 0, 2).reshape(-1)

    tt = t["route_rows"]
    pos_tiles = pos_tiles_for(tt)
    xs = _dispatch(h1p, pos_tiles_for(t["dispatch_rows"]), rt["expert_last"], rt["expert_nsub"], rt["used"],
                   rt["n_slots"], t["dispatch_rows"])
    ys = _experts(xs, rt["group_expert"], rt["group_first"], rt["group_nsub"], rt["used"],
                  w_gate_up[0], b_gate_up[0], w_down[0], b_down[0], min(t["expert_cols"], w_down.shape[-2]))
    out = _combine(h1, pos_tiles, wts.T, ys, row(ln2_g[0]), row(ln2_b[0]), tt)
    return out.reshape(bsz, seq, d)
```

```python
import functools

import jax
import jax.numpy as jnp
from jax import lax
from jax.experimental import pallas as pl
from jax.experimental.pallas import tpu as pltpu

F32 = jnp.float32
BF16 = jnp.bfloat16
I32 = jnp.int32
U32 = jnp.uint32

N_META = 16
CONV_K = 3
POOL_WINDOWS = (2, 4, 8, 16)
TOP_K = 4
SWIGLU_LIMIT = 7.0
SWIGLU_ALPHA = 1.702
LN_EPS = 1e-5
DEPTH = 1
DEEPNORM_ALPHA = (2.0 * DEPTH) ** 0.25

HALO = 16
POOL_PAD = 128
POOL_CHUNK = 256
SLOT_ROWS = 256
GROUP_SUBTILES = 9
CAST_SLICES = 8
STAGE_DEPTH = 6
MIX_HALF = 256
COMBINE_CHUNK = 16
V7X_VMEM_LIMIT = 58 * 1024 * 1024

_ARB = "arbitrary"


def _params(n_axes):
    return pltpu.CompilerParams(dimension_semantics=(_ARB,) * n_axes, vmem_limit_bytes=V7X_VMEM_LIMIT)


def _layer_norm(x, g, b):
    mu = jnp.mean(x, axis=-1, keepdims=True)
    xc = x - mu
    var = jnp.mean(xc * xc, axis=-1, keepdims=True)
    return xc * lax.rsqrt(var + LN_EPS) * g + b


def _dot(a, b):
    return jnp.dot(a, b, preferred_element_type=F32)


def _pack_pairs(lo_bf16, hi_bf16):
    lo = lax.bitcast_convert_type(lo_bf16.astype(F32), U32) >> 16
    hi = lax.bitcast_convert_type(hi_bf16.astype(F32), U32) & jnp.uint32(0xFFFF0000)
    return hi | lo


def _unpack_pairs(words):
    lo = lax.bitcast_convert_type(words << 16, F32).astype(BF16)
    hi = lax.bitcast_convert_type(words & jnp.uint32(0xFFFF0000), F32).astype(BF16)
    return lo, hi


def _ln_cast_kernel(x_ref, g_ref, b_ref, o_ref):
    o_ref[...] = _layer_norm(x_ref[...], g_ref[...], b_ref[...]).astype(o_ref.dtype)


def _ln_cast(x2d, g, b, tr):
    n, d = x2d.shape
    return pl.pallas_call(
        _ln_cast_kernel,
        grid=(n // tr,),
        in_specs=[pl.BlockSpec((tr, d), lambda i: (i, 0)),
                  pl.BlockSpec((1, d), lambda i: (0, 0)),
                  pl.BlockSpec((1, d), lambda i: (0, 0))],
        out_specs=pl.BlockSpec((tr, d), lambda i: (i, 0)),
        out_shape=jax.ShapeDtypeStruct((n, d), BF16),
        compiler_params=_params(1),
        name="ln_cast",
    )(x2d, g, b)


def _conv_kernel(h_ref, halo_ref, hm_ref, wb_ref, wc_ref, wu_ref, cw_ref, o_ref, he_ref, cu_ref, *, tm):
    i = pl.program_id(1)
    j = pl.program_id(2)

    @pl.when(j == 0)
    def _():
        he_ref[HALO:, :] = h_ref[0]

    @pl.when((j == 0) & (i == 0))
    def _():
        he_ref[0:HALO, :] = hm_ref[...]

    @pl.when((j == 0) & (i > 0))
    def _():
        he_ref[0:HALO, :] = halo_ref[0]

    he = he_ref[...]
    cu_ref[...] = _dot(he, wc_ref[...]) * _dot(he, wu_ref[...])
    bg = _dot(he_ref[HALO:, :], wb_ref[...])
    cw = cw_ref[...]
    y = cw[CONV_K - 1:CONV_K, :] * cu_ref[HALO:HALO + tm, :]
    for k in range(CONV_K - 1):
        s = HALO - (CONV_K - 1) + k
        y = y + cw[k:k + 1, :] * cu_ref[s:s + tm, :]
    o_ref[0] = (bg * y).astype(o_ref.dtype)


def _conv_branch(hx, hm, w_in_b, conv_w, d_conv, tm, tc):
    bsz, seq, d = hx.shape
    nc = d_conv // tc
    hb = tm // HALO
    return pl.pallas_call(
        functools.partial(_conv_kernel, tm=tm),
        grid=(bsz, seq // tm, nc),
        in_specs=[pl.BlockSpec((1, tm, d), lambda b, i, j: (b, i, 0)),
                  pl.BlockSpec((1, HALO, d), lambda b, i, j: (b, jnp.maximum(i * hb - 1, 0), 0)),
                  pl.BlockSpec((HALO, d), lambda b, i, j: (0, 0)),
                  pl.BlockSpec((d, tc), lambda b, i, j: (0, j)),
                  pl.BlockSpec((d, tc), lambda b, i, j: (0, nc + j)),
                  pl.BlockSpec((d, tc), lambda b, i, j: (0, 2 * nc + j)),
                  pl.BlockSpec((CONV_K, tc), lambda b, i, j: (0, j))],
        out_specs=pl.BlockSpec((1, tm, tc), lambda b, i, j: (b, i, j)),
        out_shape=jax.ShapeDtypeStruct((bsz, seq, d_conv), BF16),
        scratch_shapes=[pltpu.VMEM((tm + HALO, d), BF16), pltpu.VMEM((tm + HALO, tc), F32)],
        compiler_params=_params(3),
        name="conv_branch",
    )(hx, hx, hm, w_in_b, w_in_b, w_in_b, conv_w)


def _pool_kernel(h_ref, halo_ref, hm_ref, wv_ref, band_ref, pw_ref, ps_ref, o_ref, he_ref, vh_ref, vl_ref, *, tm):
    i = pl.program_id(1)
    lead = POOL_PAD - HALO
    he_ref[0:lead, :] = jnp.zeros((lead, he_ref.shape[1]), he_ref.dtype)
    he_ref[POOL_PAD:, :] = h_ref[0]

    @pl.when(i == 0)
    def _():
        he_ref[lead:POOL_PAD, :] = hm_ref[...]

    @pl.when(i > 0)
    def _():
        he_ref[lead:POOL_PAD, :] = halo_ref[0]

    v = _dot(he_ref[...], wv_ref[...])
    v_hi = v.astype(BF16)
    vh_ref[...] = v_hi
    vl_ref[...] = (v - v_hi.astype(F32)).astype(BF16)
    ng, cin, cout = pw_ref.shape
    rows = min(POOL_CHUNK, tm)
    for c in range(tm // rows):
        r0 = c * rows
        for g in range(ng):
            band = band_ref[g]
            win = (slice(r0, r0 + rows + POOL_PAD), slice(g * cin, (g + 1) * cin))
            pooled = _dot(band, vh_ref[win]) + _dot(band, vl_ref[win])
            y = _dot(pooled.astype(BF16), pw_ref[g]) * ps_ref[:, g * cout:(g + 1) * cout]
            o_ref[0, r0:r0 + rows, g * cout:(g + 1) * cout] = y.astype(o_ref.dtype)


def _pool_band(rows):
    t = jnp.arange(rows, dtype=I32)[:, None] + POOL_PAD
    s = jnp.arange(rows + POOL_PAD, dtype=I32)[None, :]
    bands = []
    for w in POOL_WINDOWS:
        inside = ((s <= t) & (s > t - w)).astype(F32) / w
        bands.append(inside - (s == t).astype(F32))
    return jnp.stack(bands).astype(BF16)


def _pool_branch(hx, hm, w_in_b, pool_w_b, pool_scale, col0, tm):
    bsz, seq, d = hx.shape
    ng, cin, cout = pool_w_b.shape
    d_pool = ng * cin
    hb = tm // HALO
    band = _pool_band(min(POOL_CHUNK, tm))
    return pl.pallas_call(
        functools.partial(_pool_kernel, tm=tm),
        grid=(bsz, seq // tm),
        in_specs=[pl.BlockSpec((1, tm, d), lambda b, i: (b, i, 0)),
                  pl.BlockSpec((1, HALO, d), lambda b, i: (b, jnp.maximum(i * hb - 1, 0), 0)),
                  pl.BlockSpec((HALO, d), lambda b, i: (0, 0)),
                  pl.BlockSpec((d, d_pool), lambda b, i: (0, col0 // d_pool)),
                  pl.BlockSpec(band.shape, lambda b, i: (0, 0, 0)),
                  pl.BlockSpec((ng, cin, cout), lambda b, i: (0, 0, 0)),
                  pl.BlockSpec((1, ng * cout), lambda b, i: (0, 0))],
        out_specs=pl.BlockSpec((1, tm, ng * cout), lambda b, i: (b, i, 0)),
        out_shape=jax.ShapeDtypeStruct((bsz, seq, ng * cout), BF16),
        scratch_shapes=[pltpu.VMEM((tm + POOL_PAD, d), BF16),
                        pltpu.VMEM((tm + POOL_PAD, d_pool), BF16),
                        pltpu.VMEM((tm + POOL_PAD, d_pool), BF16)],
        compiler_params=_params(2),
        name="pool_branch",
    )(hx, hx, hm, w_in_b, band, pool_w_b, pool_scale)


def _merge_kernel(h_ref, a_ref, yb_ref, wga_ref, wgb_ref, wao_ref, o_ref):
    h = h_ref[...]
    ga = jax.nn.sigmoid(_dot(h, wga_ref[...]))
    gb = jax.nn.sigmoid(_dot(h, wgb_ref[...]))
    ya = _dot(a_ref[...], wao_ref[...])
    o_ref[...] = (ga * ya + gb * yb_ref[...].astype(F32)).astype(o_ref.dtype)


def _merge(hx2, a2, yb2, w_in_b, w_a_out_b, col0, tm, tj):
    n, d = hx2.shape
    dc = a2.shape[1]
    nj = d // tj
    return pl.pallas_call(
        _merge_kernel,
        grid=(n // tm, nj),
        in_specs=[pl.BlockSpec((tm, d), lambda i, j: (i, 0)),
                  pl.BlockSpec((tm, dc), lambda i, j: (i, 0)),
                  pl.BlockSpec((tm, tj), lambda i, j: (i, j)),
                  pl.BlockSpec((d, tj), lambda i, j: (0, col0 // tj + j)),
                  pl.BlockSpec((d, tj), lambda i, j: (0, col0 // tj + nj + j)),
                  pl.BlockSpec((dc, tj), lambda i, j: (0, j))],
        out_specs=pl.BlockSpec((tm, tj), lambda i, j: (i, j)),
        out_shape=jax.ShapeDtypeStruct((n, d), BF16),
        compiler_params=_params(2),
        name="merge",
    )(hx2, a2, yb2, w_in_b, w_in_b, w_a_out_b)


def _mix_kernel(m_ref, x_ref, lig_ref, lib_ref, wo_ref, g1_ref, b1_ref, rwh_ref, rwl_ref, rb_ref, tri_ref,
                h1_ref, h1p_ref, eid_ref, wts_ref, rank_ref, cnt_ref, carry_ref, z_ref, *, rows):
    step = pl.program_id(0)

    @pl.when(step == 0)
    def _():
        carry_ref[...] = jnp.zeros(carry_ref.shape, F32)

    h0 = _layer_norm(x_ref[...], lig_ref[...], lib_ref[...])
    z_ref[...] = DEEPNORM_ALPHA * h0 + _dot(m_ref[...], wo_ref[...])

    for r0 in range(0, m_ref.shape[0], rows):
        rs = slice(r0, r0 + rows)
        h1 = _layer_norm(z_ref[rs, :], g1_ref[...], b1_ref[...])
        h1_ref[rs, :] = h1
        h_hi = h1.astype(BF16)
        half = h1.shape[1] // 2
        h1p_ref[rs, :] = _pack_pairs(h_hi[:, :half], h_hi[:, half:])

        h_lo = (h1 - h_hi.astype(F32)).astype(BF16)
        nt = (((1,), (1,)), ((), ()))
        rwh = rwh_ref[...]
        logits = (lax.dot_general(rwh, h_hi, nt, preferred_element_type=F32)
                  + lax.dot_general(rwh, h_lo, nt, preferred_element_type=F32)
                  + lax.dot_general(rwl_ref[...], h_hi, nt, preferred_element_type=F32)
                  + rb_ref[...])
        n_exp = logits.shape[0]
        e_iota = lax.broadcasted_iota(I32, logits.shape, 0)
        work = logits
        vals, hots = [], []
        for k in range(TOP_K):
            m = jnp.max(work, axis=0, keepdims=True)
            idx = jnp.min(jnp.where(work == m, e_iota, n_exp), axis=0, keepdims=True)
            hot = e_iota == idx
            vals.append(m)
            hots.append(hot)
            eid_ref[k:k + 1, rs] = idx
            work = jnp.where(hot, -jnp.inf, work)
        exps = [jnp.exp(v - vals[0]) for v in vals]
        denom = exps[0]
        for e in exps[1:]:
            denom = denom + e
        for k in range(TOP_K):
            wts_ref[k:k + 1, rs] = exps[k] / denom

        multi = hots[0]
        for hot in hots[1:]:
            multi = multi | hot
        multi_f = jnp.where(multi, 1.0, 0.0).astype(F32)
        prefix = _dot(multi_f.astype(BF16), tri_ref[...]) + carry_ref[:, 0:1]
        for k in range(TOP_K):
            rank_ref[k:k + 1, rs] = jnp.sum(jnp.where(hots[k], prefix, 0.0), axis=0, keepdims=True).astype(I32)
        carry_ref[...] = carry_ref[...] + jnp.sum(multi_f, axis=1, keepdims=True)
    cnt_ref[...] = carry_ref[...]


def _mix_ln1(m2, x2, ln_in_g, ln_in_b, w_o_b, ln1_g, ln1_b, rw_hi, rw_lo, rb, tm):
    n, d = x2.shape
    n_exp = rw_hi.shape[0]
    rows = min(MIX_HALF, tm)
    tri = (jnp.arange(rows, dtype=I32)[:, None] < jnp.arange(rows, dtype=I32)[None, :]).astype(BF16)
    row = lambda i: (i, 0)
    fixed = lambda i: (0, 0)
    col = lambda i: (0, i)
    return pl.pallas_call(
        functools.partial(_mix_kernel, rows=rows),
        grid=(n // tm,),
        in_specs=[pl.BlockSpec((tm, d), row), pl.BlockSpec((tm, d), row),
                  pl.BlockSpec((1, d), fixed), pl.BlockSpec((1, d), fixed),
                  pl.BlockSpec((d, d), fixed),
                  pl.BlockSpec((1, d), fixed), pl.BlockSpec((1, d), fixed),
                  pl.BlockSpec((n_exp, d), fixed), pl.BlockSpec((n_exp, d), fixed),
                  pl.BlockSpec((n_exp, 1), fixed),
                  pl.BlockSpec((rows, rows), fixed)],
        out_specs=[pl.BlockSpec((tm, d), row), pl.BlockSpec((tm, d // 2), row),
                   pl.BlockSpec((TOP_K, tm), col), pl.BlockSpec((TOP_K, tm), col), pl.BlockSpec((TOP_K, tm), col),
                   pl.BlockSpec((n_exp, 128), fixed)],
        out_shape=[jax.ShapeDtypeStruct((n, d), F32),
                   jax.ShapeDtypeStruct((n, d // 2), U32),
                   jax.ShapeDtypeStruct((TOP_K, n), I32),
                   jax.ShapeDtypeStruct((TOP_K, n), F32),
                   jax.ShapeDtypeStruct((TOP_K, n), I32),
                   jax.ShapeDtypeStruct((n_exp, 128), F32)],
        scratch_shapes=[pltpu.VMEM((n_exp, 128), F32), pltpu.VMEM((tm, d), F32)],
        compiler_params=_params(1),
        name="mix_ln1",
    )(m2, x2, ln_in_g, ln_in_b, w_o_b, ln1_g, ln1_b, rw_hi, rw_lo, rb, tri)


def _row_copy(src_ref, src_row, dst_ref, dst_row, sem):
    return pltpu.make_async_copy(src_ref.at[pl.ds(src_row, 1)], dst_ref.at[pl.ds(dst_row, 1)], sem)


def _subtile(q):
    return pl.ds(pl.multiple_of(q * SLOT_ROWS, SLOT_ROWS), SLOT_ROWS)


def _dispatch_kernel(pos_ref, gl_ref, gn_ref, used_ref, h_ref, xs_ref, zero_ref, sem, zsem, *, tt):
    base = pl.program_id(0) * (TOP_K * tt)

    @pl.when(pl.program_id(0) == 0)
    def _():
        zero_ref[...] = jnp.zeros(zero_ref.shape, zero_ref.dtype)
        n_exp = gl_ref.shape[0]
        n_sub = xs_ref.shape[0] // SLOT_ROWS

        def zero_copy(q):
            return pltpu.make_async_copy(zero_ref, xs_ref.at[_subtile(q)], zsem)

        def group_tail(e, start):
            @pl.when(gn_ref[e] > 0)
            def _():
                c = zero_copy(gl_ref[e])
                c.start() if start else c.wait()

        def unused(q, start):
            c = zero_copy(q)
            c.start() if start else c.wait()

        for start in (True, False):
            lax.fori_loop(0, n_exp, lambda e, c: (group_tail(e, start), c)[1], 0)
            lax.fori_loop(used_ref[1], n_sub, lambda q, c: (unused(q, start), c)[1], 0)

    def issue(t, c):
        for k in range(TOP_K):
            _row_copy(h_ref, t, xs_ref, pos_ref[base + k * tt + t], sem).start(priority=k % 2)
        return c

    lax.fori_loop(0, tt, issue, 0)
    for k in range(TOP_K):
        pltpu.make_async_copy(h_ref, xs_ref.at[pl.ds(0, tt)], sem).wait()


def _dispatch(h1p, pos_tiles, group_last, group_nsub, used, n_slots, tt):
    n, dw = h1p.shape
    grid_spec = pltpu.PrefetchScalarGridSpec(
        num_scalar_prefetch=4,
        grid=(n // tt,),
        in_specs=[pl.BlockSpec((tt, dw), lambda i, *_: (i, 0))],
        out_specs=pl.BlockSpec(memory_space=pl.ANY),
        scratch_shapes=[pltpu.VMEM((SLOT_ROWS, dw), h1p.dtype),
                        pltpu.SemaphoreType.DMA(()), pltpu.SemaphoreType.DMA(())],
    )
    return pl.pallas_call(
        functools.partial(_dispatch_kernel, tt=tt),
        grid_spec=grid_spec,
        out_shape=jax.ShapeDtypeStruct((n_slots, dw), h1p.dtype),
        compiler_params=_params(1),
        name="dispatch",
    )(pos_tiles, group_last, group_nsub, used, h1p)


def _expert_kernel(ge_ref, gs_ref, gn_ref, nu_ref, xs_ref, wg_ref, wl_ref, wd_ref, bg_ref, bl_ref, bd_ref,
                   ys_ref, xb_ref, acc_ref, stage_ref, wgb_ref, wlb_ref, wdb_ref, sem_in, sem_out, *, nj):
    t = pl.program_id(0)
    n_groups = nu_ref[0]
    n_items = n_groups * nj
    c = jnp.maximum(t - 1, 0)
    s = c // nj
    j = c - s * nj
    cast_slot = t % 2
    use_slot = 1 - cast_slot
    d, tn = wgb_ref.shape[1:]
    half = d // 2
    rows = _subtile

    def cast_slice(i):
        r = pl.ds(pl.multiple_of(i * (d // CAST_SLICES), d // CAST_SLICES), d // CAST_SLICES)
        wgb_ref[cast_slot, r, :] = wg_ref[0, r, :].astype(BF16)
        wlb_ref[cast_slot, r, :] = wl_ref[0, r, :].astype(BF16)
        r = pl.ds(pl.multiple_of(i * (tn // CAST_SLICES), tn // CAST_SLICES), tn // CAST_SLICES)
        wdb_ref[cast_slot, r, :] = wd_ref[0, r, :].astype(BF16)

    def cast_range(lo, hi):
        lax.fori_loop(lo, hi, lambda i, z: (cast_slice(i), z)[1], 0)

    def out_copy(first_sub, q):
        return pltpu.make_async_copy(acc_ref.at[rows(q)], ys_ref.at[rows(first_sub + q)], sem_out.at[q])

    def in_copy(first_sub, q, slot):
        return pltpu.make_async_copy(xs_ref.at[rows(first_sub + q)], stage_ref.at[slot], sem_in.at[slot])

    def prefetch(first_sub, nsub):
        lax.fori_loop(0, jnp.minimum(nsub, stage_ref.shape[0]),
                      lambda q, z: (in_copy(first_sub, q, q).start(), z)[1], 0)

    @pl.when(t == 0)
    def _():
        cast_range(0, CAST_SLICES)

    @pl.when((t >= 1) & (t <= n_items))
    def _():
        first_sub = gs_ref[s]
        nsub = gn_ref[s]

        @pl.when(j == 0)
        def _():
            sp = jnp.maximum(s - 1, 0)
            prev_first = gs_ref[sp]
            prev_nsub = jnp.where(s > 0, gn_ref[sp], 0)
            depth = stage_ref.shape[0]

            @pl.when(s == 0)
            def _():
                prefetch(first_sub, nsub)

            def load(q, z):
                slot = q % depth
                in_copy(first_sub, q, slot).wait()
                lo, hi = _unpack_pairs(stage_ref[slot])
                xb_ref[rows(q), 0:half] = lo
                xb_ref[rows(q), half:] = hi

                @pl.when(q + depth < nsub)
                def _():
                    in_copy(first_sub, q + depth, slot).start()

                @pl.when(q < prev_nsub)
                def _():
                    out_copy(prev_first, q).wait()

                acc_ref[rows(q), :] = jnp.broadcast_to(bd_ref[0], (SLOT_ROWS, d))
                return z

            lax.fori_loop(0, nsub, load, 0)
            lax.fori_loop(nsub, prev_nsub, lambda q, z: (out_copy(prev_first, q).wait(), z)[1], 0)

        def run(q, n):
            r = pl.ds(pl.multiple_of(q * SLOT_ROWS, SLOT_ROWS), n * SLOT_ROWS)
            x = xb_ref[r, :]
            gate = _dot(x, wgb_ref[use_slot]) + bg_ref[0]
            lin = _dot(x, wlb_ref[use_slot]) + bl_ref[0]
            gate = jnp.minimum(gate, SWIGLU_LIMIT)
            lin = jnp.clip(lin, -SWIGLU_LIMIT, SWIGLU_LIMIT)
            act = (lin + 1.0) * gate * jax.nn.sigmoid(SWIGLU_ALPHA * gate)
            acc_ref[r, :] += _dot(act.astype(BF16), wdb_ref[use_slot])

            @pl.when(j == nj - 1)
            def _():
                for u in range(n):
                    out_copy(first_sub, q + u).start()

        a = nsub // 3
        r = nsub - 3 * a
        two_quads = (r == 2) & (a >= 2)
        first_n = jnp.where(nsub == 1, 1, jnp.where(r == 0, 3, jnp.where((r == 1) | two_quads, 4, 2)))
        for n in (1, 2, 3, 4):
            @pl.when(first_n == n)
            def _():
                run(0, n)
                for i in range(CAST_SLICES):
                    cast_slice(i)

        @pl.when(two_quads)
        def _():
            run(4, 4)

        done = first_n + jnp.where(two_quads, 4, 0)
        lax.fori_loop(0, (nsub - done) // 3, lambda p, z: (run(done + 3 * p, 3), z)[1], 0)

        @pl.when((j == nj - 1) & (s + 1 < n_groups))
        def _():
            nxt = jnp.minimum(s + 1, n_groups - 1)
            prefetch(gs_ref[nxt], gn_ref[nxt])

        @pl.when(t == n_items)
        def _():
            lax.fori_loop(0, nsub, lambda q, z: (out_copy(first_sub, q).wait(), z)[1], 0)
            acc_ref[rows(0), :] = jnp.zeros((SLOT_ROWS, d), F32)
            n_sub = ys_ref.shape[0] // SLOT_ROWS

            def tail_copy(q):
                return pltpu.make_async_copy(acc_ref.at[rows(0)], ys_ref.at[rows(q)], sem_out.at[0])

            lax.fori_loop(nu_ref[1], n_sub, lambda q, z: (tail_copy(q).start(), z)[1], 0)
            lax.fori_loop(nu_ref[1], n_sub, lambda q, z: (tail_copy(q).wait(), z)[1], 0)


def _experts(xs, group_expert, group_first, group_nsub, used, w_gate_up, b_gate_up, w_down, b_down, tn):
    n_slots, dw = xs.shape
    d = 2 * dw
    n_exp, _, de2 = w_gate_up.shape
    de = de2 // 2
    nj = de // tn
    max_items = group_expert.shape[0] * nj
    rmax = GROUP_SUBTILES * SLOT_ROWS

    def fetched(t, nu):
        w = jnp.minimum(t, nu[0] * nj - 1)
        return w // nj, w % nj

    def computed(t, nu):
        w = jnp.clip(t - 1, 0, nu[0] * nj - 1)
        return w // nj, w % nj

    def wg_map(t, ge, gs, gn, nu):
        g, j = fetched(t, nu)
        return ge[g], 0, j

    def wl_map(t, ge, gs, gn, nu):
        g, j = fetched(t, nu)
        return ge[g], 0, nj + j

    def wd_map(t, ge, gs, gn, nu):
        g, j = fetched(t, nu)
        return ge[g], j, 0

    def bg_map(t, ge, gs, gn, nu):
        g, j = computed(t, nu)
        return ge[g], 0, j

    def bl_map(t, ge, gs, gn, nu):
        g, j = computed(t, nu)
        return ge[g], 0, nj + j

    def bd_map(t, ge, gs, gn, nu):
        g, _ = computed(t, nu)
        return ge[g], 0, 0

    grid_spec = pltpu.PrefetchScalarGridSpec(
        num_scalar_prefetch=4,
        grid=(used[0] * nj + 1,),
        in_specs=[pl.BlockSpec(memory_space=pl.ANY),
                  pl.BlockSpec((1, d, tn), wg_map), pl.BlockSpec((1, d, tn), wl_map),
                  pl.BlockSpec((1, tn, d), wd_map),
                  pl.BlockSpec((1, 1, tn), bg_map), pl.BlockSpec((1, 1, tn), bl_map),
                  pl.BlockSpec((1, 1, d), bd_map)],
        out_specs=pl.BlockSpec(memory_space=pl.ANY),
        scratch_shapes=[pltpu.VMEM((rmax, d), BF16),
                        pltpu.VMEM((rmax, d), F32),
                        pltpu.VMEM((STAGE_DEPTH, SLOT_ROWS, dw), U32),
                        pltpu.VMEM((2, d, tn), BF16), pltpu.VMEM((2, d, tn), BF16), pltpu.VMEM((2, tn, d), BF16),
                        pltpu.SemaphoreType.DMA((STAGE_DEPTH,)),
                        pltpu.SemaphoreType.DMA((GROUP_SUBTILES,))],
    )
    return pl.pallas_call(
        functools.partial(_expert_kernel, nj=nj),
        grid_spec=grid_spec,
        out_shape=jax.ShapeDtypeStruct((n_slots, d), F32),
        compiler_params=_params(1),
        name="experts",
    )(group_expert, group_first, group_nsub, used, xs, w_gate_up, w_gate_up, w_down,
      b_gate_up.reshape(n_exp, 1, de2), b_gate_up.reshape(n_exp, 1, de2), b_down.reshape(n_exp, 1, d))


def _combine_kernel(pos_ref, h_ref, w_ref, g_ref, b_ref, ys_ref, o_ref, buf_ref, sem, *, tt, n_steps):
    i = pl.program_id(0)
    slot = i % 2
    ch = min(COMBINE_CHUNK, tt)

    def issue_rows(step, dst_slot, t0):
        base = step * (TOP_K * tt)
        for r in range(ch):
            for k in range(TOP_K):
                _row_copy(ys_ref, pos_ref[base + k * tt + t0 + r], buf_ref.at[dst_slot, k], t0 + r,
                          sem.at[dst_slot]).start(priority=k % 2)

    def wait_rows(src_slot):
        for k in range(TOP_K):
            pltpu.make_async_copy(ys_ref.at[pl.ds(0, tt)], buf_ref.at[src_slot, k], sem.at[src_slot]).wait()

    @pl.when(i == 0)
    def _():
        lax.fori_loop(0, tt // ch, lambda c, z: (issue_rows(0, 0, c * ch), z)[1], 0)

    wait_rows(slot)

    @pl.when(i + 1 < n_steps)
    def _():
        lax.fori_loop(0, tt // ch, lambda c, z: (issue_rows(i + 1, 1 - slot, c * ch), z)[1], 0)

    w = w_ref[...]
    y = DEEPNORM_ALPHA * h_ref[...]
    for k in range(TOP_K):
        y = y + w[:, k:k + 1] * buf_ref[slot, k]
    o_ref[...] = _layer_norm(y, g_ref[...], b_ref[...])


def _combine(h1, pos_tiles, wts_t, ys, ln2_g, ln2_b, tt):
    n, d = h1.shape
    n_steps = n // tt
    grid_spec = pltpu.PrefetchScalarGridSpec(
        num_scalar_prefetch=1,
        grid=(n_steps,),
        in_specs=[pl.BlockSpec((tt, d), lambda i, pos: (i, 0)),
                  pl.BlockSpec((tt, TOP_K), lambda i, pos: (i, 0)),
                  pl.BlockSpec((1, d), lambda i, pos: (0, 0)),
                  pl.BlockSpec((1, d), lambda i, pos: (0, 0)),
                  pl.BlockSpec(memory_space=pl.ANY)],
        out_specs=pl.BlockSpec((tt, d), lambda i, pos: (i, 0)),
        scratch_shapes=[pltpu.VMEM((2, TOP_K, tt, d), F32), pltpu.SemaphoreType.DMA((2,))],
    )
    return pl.pallas_call(
        functools.partial(_combine_kernel, tt=tt, n_steps=n_steps),
        grid_spec=grid_spec,
        out_shape=jax.ShapeDtypeStruct((n, d), F32),
        compiler_params=_params(1),
        name="combine",
    )(pos_tiles, h1, wts_t, ln2_g, ln2_b, ys)


def _routing_tables(counts, eid, rank, n_tok):
    n_exp = counts.shape[0]
    e_ids = jnp.arange(n_exp, dtype=I32)
    nsub = (counts + SLOT_ROWS - 1) // SLOT_ROWS
    ngrp = (nsub + GROUP_SUBTILES - 1) // GROUP_SUBTILES
    base = nsub // jnp.maximum(ngrp, 1)
    rem = nsub - base * ngrp
    first_sub = jnp.cumsum(nsub) - nsub
    pos = jnp.sum(jnp.where(eid[..., None] == e_ids, first_sub * SLOT_ROWS, 0), axis=-1) + rank

    n_subtiles = (n_tok * TOP_K) // SLOT_ROWS + n_exp
    max_groups = (n_subtiles + GROUP_SUBTILES - 1) // GROUP_SUBTILES + n_exp
    grp_end = jnp.cumsum(ngrp)
    g_ids = jnp.arange(max_groups, dtype=I32)
    g_exp = jnp.minimum(jnp.sum((grp_end[None, :] <= g_ids[:, None]).astype(I32), axis=1), n_exp - 1)
    pick = lambda tab: jnp.sum(jnp.where(g_exp[:, None] == e_ids[None, :], tab[None, :], 0), axis=1)
    local = g_ids - pick(grp_end - ngrp)
    g_base, g_rem = pick(base), pick(rem)
    g_nsub = g_base + (local < g_rem).astype(I32)
    g_first = pick(first_sub) + local * g_base + jnp.minimum(local, g_rem)
    used = jnp.stack([grp_end[-1], jnp.sum(nsub)]).astype(I32)
    return dict(pos=pos, n_slots=n_subtiles * SLOT_ROWS, used=used,
                group_expert=g_exp.astype(I32), group_first=g_first.astype(I32), group_nsub=g_nsub.astype(I32),
                expert_last=(first_sub + nsub - 1).astype(I32), expert_nsub=nsub.astype(I32))


def _tiles(seq, n_tok):
    return dict(
        ln_rows=min(512, n_tok),
        conv_rows=min(1024, seq), conv_cols=512,
        pool_rows=min(1024, seq),
        merge_rows=min(1024, n_tok), merge_cols=512,
        mix_rows=min(512, n_tok),
        route_rows=min(256, n_tok), dispatch_rows=min(1024, n_tok),
        expert_cols=256,
    )


def kernel(x, meta_tokens, ln_in_g, ln_in_b, w_in, conv_w, w_a_out, pool_w, pool_scale, w_o, ln1_g, ln1_b,
           router_w, router_b, w_gate_up, b_gate_up, w_down, b_down, ln2_g, ln2_b):
    bsz, seq, d = x.shape
    assert w_in.shape[0] == DEPTH and meta_tokens.shape[0] == N_META == HALO
    d_conv = conv_w.shape[-1]
    n_groups, pool_cin, pool_cout = pool_w.shape[1:]
    d_pool = n_groups * pool_cin
    assert n_groups == len(POOL_WINDOWS) and n_groups * pool_cout == d
    n_exp = router_w.shape[-1]
    n_tok = bsz * seq
    t = _tiles(seq, n_tok)
    row = lambda v: v.reshape(1, -1).astype(F32)

    w_in_b = w_in[0].astype(BF16)
    x2 = x.reshape(n_tok, d)
    lig, lib = row(ln_in_g), row(ln_in_b)

    hx2 = _ln_cast(x2, lig, lib, t["ln_rows"])
    hm = _ln_cast(meta_tokens.astype(F32), lig, lib, N_META)
    hx = hx2.reshape(bsz, seq, d)

    a = _conv_branch(hx, hm, w_in_b, conv_w[0], d_conv, t["conv_rows"], min(t["conv_cols"], d_conv))
    yb = _pool_branch(hx, hm, w_in_b, pool_w[0].astype(BF16), row(pool_scale[0]), 3 * d_conv, t["pool_rows"])
    m2 = _merge(hx2, a.reshape(n_tok, d_conv), yb.reshape(n_tok, d), w_in_b, w_a_out[0].astype(BF16),
                3 * d_conv + d_pool, t["merge_rows"], min(t["merge_cols"], d))

    rw_t = router_w[0].T
    rw_hi = rw_t.astype(BF16)
    rw_lo = (rw_t - rw_hi.astype(F32)).astype(BF16)
    h1, h1p, eid, wts, rank, cnt = _mix_ln1(m2, x2, lig, lib, w_o[0].astype(BF16), row(ln1_g[0]), row(ln1_b[0]),
                                            rw_hi, rw_lo, router_b[0].reshape(n_exp, 1), t["mix_rows"])

    rt = _routing_tables(cnt[:, 0].astype(I32), eid, rank, n_tok)
    def pos_tiles_for(rows):
        return rt["pos"].reshape(TOP_K, n_tok // rows, rows).transpose(1, 0, 2).reshape(-1)

    tt = t["route_rows"]
    pos_tiles = pos_tiles_for(tt)
    xs = _dispatch(h1p, pos_tiles_for(t["dispatch_rows"]), rt["expert_last"], rt["expert_nsub"], rt["used"],
                   rt["n_slots"], t["dispatch_rows"])
    ys = _experts(xs, rt["group_expert"], rt["group_first"], rt["group_nsub"], rt["used"],
                  w_gate_up[0], b_gate_up[0], w_down[0], b_down[0], min(t["expert_cols"], w_down.shape[-2]))
    out = _combine(h1, pos_tiles, wts.T, ys, row(ln2_g[0]), row(ln2_b[0]), tt)
    return out.reshape(bsz, seq, d)
```

```python
import functools

import jax
import jax.numpy as jnp
from jax import lax
from jax.experimental import pallas as pl
from jax.experimental.pallas import tpu as pltpu

F32 = jnp.float32
BF16 = jnp.bfloat16
I32 = jnp.int32
U32 = jnp.uint32

N_META = 16
CONV_K = 3
POOL_WINDOWS = (2, 4, 8, 16)
TOP_K = 4
SWIGLU_LIMIT = 7.0
SWIGLU_ALPHA = 1.702
LN_EPS = 1e-5
DEPTH = 1
DEEPNORM_ALPHA = (2.0 * DEPTH) ** 0.25

HALO = 16
POOL_PAD = 128
POOL_CHUNK = 256
SLOT_ROWS = 256
GROUP_SUBTILES = 9
CAST_SLICES = 8
STAGE_DEPTH = 8
MIX_HALF = 256
COMBINE_CHUNK = 16
V7X_VMEM_LIMIT = 58 * 1024 * 1024

_ARB = "arbitrary"


def _params(n_axes):
    return pltpu.CompilerParams(dimension_semantics=(_ARB,) * n_axes, vmem_limit_bytes=V7X_VMEM_LIMIT)


def _layer_norm(x, g, b):
    mu = jnp.mean(x, axis=-1, keepdims=True)
    xc = x - mu
    var = jnp.mean(xc * xc, axis=-1, keepdims=True)
    return xc * lax.rsqrt(var + LN_EPS) * g + b


def _dot(a, b):
    return jnp.dot(a, b, preferred_element_type=F32)


def _pack_pairs(lo_bf16, hi_bf16):
    lo = lax.bitcast_convert_type(lo_bf16.astype(F32), U32) >> 16
    hi = lax.bitcast_convert_type(hi_bf16.astype(F32), U32) & jnp.uint32(0xFFFF0000)
    return hi | lo


def _unpack_pairs(words):
    lo = lax.bitcast_convert_type(words << 16, F32).astype(BF16)
    hi = lax.bitcast_convert_type(words & jnp.uint32(0xFFFF0000), F32).astype(BF16)
    return lo, hi


def _ln_cast_kernel(x_ref, g_ref, b_ref, o_ref):
    o_ref[...] = _layer_norm(x_ref[...], g_ref[...], b_ref[...]).astype(o_ref.dtype)


def _ln_cast(x2d, g, b, tr):
    n, d = x2d.shape
    return pl.pallas_call(
        _ln_cast_kernel,
        grid=(n // tr,),
        in_specs=[pl.BlockSpec((tr, d), lambda i: (i, 0)),
                  pl.BlockSpec((1, d), lambda i: (0, 0)),
                  pl.BlockSpec((1, d), lambda i: (0, 0))],
        out_specs=pl.BlockSpec((tr, d), lambda i: (i, 0)),
        out_shape=jax.ShapeDtypeStruct((n, d), BF16),
        compiler_params=_params(1),
        name="ln_cast",
    )(x2d, g, b)


def _conv_kernel(h_ref, halo_ref, hm_ref, wb_ref, wc_ref, wu_ref, cw_ref, o_ref, he_ref, cu_ref, *, tm):
    i = pl.program_id(1)
    j = pl.program_id(2)

    @pl.when(j == 0)
    def _():
        he_ref[HALO:, :] = h_ref[0]

    @pl.when((j == 0) & (i == 0))
    def _():
        he_ref[0:HALO, :] = hm_ref[...]

    @pl.when((j == 0) & (i > 0))
    def _():
        he_ref[0:HALO, :] = halo_ref[0]

    he = he_ref[...]
    cu_ref[...] = _dot(he, wc_ref[...]) * _dot(he, wu_ref[...])
    bg = _dot(he_ref[HALO:, :], wb_ref[...])
    cw = cw_ref[...]
    y = cw[CONV_K - 1:CONV_K, :] * cu_ref[HALO:HALO + tm, :]
    for k in range(CONV_K - 1):
        s = HALO - (CONV_K - 1) + k
        y = y + cw[k:k + 1, :] * cu_ref[s:s + tm, :]
    o_ref[0] = (bg * y).astype(o_ref.dtype)


def _conv_branch(hx, hm, w_in_b, conv_w, d_conv, tm, tc):
    bsz, seq, d = hx.shape
    nc = d_conv // tc
    hb = tm // HALO
    return pl.pallas_call(
        functools.partial(_conv_kernel, tm=tm),
        grid=(bsz, seq // tm, nc),
        in_specs=[pl.BlockSpec((1, tm, d), lambda b, i, j: (b, i, 0)),
                  pl.BlockSpec((1, HALO, d), lambda b, i, j: (b, jnp.maximum(i * hb - 1, 0), 0)),
                  pl.BlockSpec((HALO, d), lambda b, i, j: (0, 0)),
                  pl.BlockSpec((d, tc), lambda b, i, j: (0, j)),
                  pl.BlockSpec((d, tc), lambda b, i, j: (0, nc + j)),
                  pl.BlockSpec((d, tc), lambda b, i, j: (0, 2 * nc + j)),
                  pl.BlockSpec((CONV_K, tc), lambda b, i, j: (0, j))],
        out_specs=pl.BlockSpec((1, tm, tc), lambda b, i, j: (b, i, j)),
        out_shape=jax.ShapeDtypeStruct((bsz, seq, d_conv), BF16),
        scratch_shapes=[pltpu.VMEM((tm + HALO, d), BF16), pltpu.VMEM((tm + HALO, tc), F32)],
        compiler_params=_params(3),
        name="conv_branch",
    )(hx, hx, hm, w_in_b, w_in_b, w_in_b, conv_w)


def _pool_kernel(h_ref, halo_ref, hm_ref, wv_ref, band_ref, pw_ref, ps_ref, o_ref, he_ref, vh_ref, vl_ref, *, tm):
    i = pl.program_id(1)
    lead = POOL_PAD - HALO
    he_ref[0:lead, :] = jnp.zeros((lead, he_ref.shape[1]), he_ref.dtype)
    he_ref[POOL_PAD:, :] = h_ref[0]

    @pl.when(i == 0)
    def _():
        he_ref[lead:POOL_PAD, :] = hm_ref[...]

    @pl.when(i > 0)
    def _():
        he_ref[lead:POOL_PAD, :] = halo_ref[0]

    v = _dot(he_ref[...], wv_ref[...])
    v_hi = v.astype(BF16)
    vh_ref[...] = v_hi
    vl_ref[...] = (v - v_hi.astype(F32)).astype(BF16)
    ng, cin, cout = pw_ref.shape
    rows = min(POOL_CHUNK, tm)
    for c in range(tm // rows):
        r0 = c * rows
        for g in range(ng):
            band = band_ref[g]
            win = (slice(r0, r0 + rows + POOL_PAD), slice(g * cin, (g + 1) * cin))
            pooled = _dot(band, vh_ref[win]) + _dot(band, vl_ref[win])
            y = _dot(pooled.astype(BF16), pw_ref[g]) * ps_ref[:, g * cout:(g + 1) * cout]
            o_ref[0, r0:r0 + rows, g * cout:(g + 1) * cout] = y.astype(o_ref.dtype)


def _pool_band(rows):
    t = jnp.arange(rows, dtype=I32)[:, None] + POOL_PAD
    s = jnp.arange(rows + POOL_PAD, dtype=I32)[None, :]
    bands = []
    for w in POOL_WINDOWS:
        inside = ((s <= t) & (s > t - w)).astype(F32) / w
        bands.append(inside - (s == t).astype(F32))
    return jnp.stack(bands).astype(BF16)


def _pool_branch(hx, hm, w_in_b, pool_w_b, pool_scale, col0, tm):
    bsz, seq, d = hx.shape
    ng, cin, cout = pool_w_b.shape
    d_pool = ng * cin
    hb = tm // HALO
    band = _pool_band(min(POOL_CHUNK, tm))
    return pl.pallas_call(
        functools.partial(_pool_kernel, tm=tm),
        grid=(bsz, seq // tm),
        in_specs=[pl.BlockSpec((1, tm, d), lambda b, i: (b, i, 0)),
                  pl.BlockSpec((1, HALO, d), lambda b, i: (b, jnp.maximum(i * hb - 1, 0), 0)),
                  pl.BlockSpec((HALO, d), lambda b, i: (0, 0)),
                  pl.BlockSpec((d, d_pool), lambda b, i: (0, col0 // d_pool)),
                  pl.BlockSpec(band.shape, lambda b, i: (0, 0, 0)),
                  pl.BlockSpec((ng, cin, cout), lambda b, i: (0, 0, 0)),
                  pl.BlockSpec((1, ng * cout), lambda b, i: (0, 0))],
        out_specs=pl.BlockSpec((1, tm, ng * cout), lambda b, i: (b, i, 0)),
        out_shape=jax.ShapeDtypeStruct((bsz, seq, ng * cout), BF16),
        scratch_shapes=[pltpu.VMEM((tm + POOL_PAD, d), BF16),
                        pltpu.VMEM((tm + POOL_PAD, d_pool), BF16),
                        pltpu.VMEM((tm + POOL_PAD, d_pool), BF16)],
        compiler_params=_params(2),
        name="pool_branch",
    )(hx, hx, hm, w_in_b, band, pool_w_b, pool_scale)


def _merge_kernel(h_ref, a_ref, yb_ref, wga_ref, wgb_ref, wao_ref, o_ref):
    h = h_ref[...]
    ga = jax.nn.sigmoid(_dot(h, wga_ref[...]))
    gb = jax.nn.sigmoid(_dot(h, wgb_ref[...]))
    ya = _dot(a_ref[...], wao_ref[...])
    o_ref[...] = (ga * ya + gb * yb_ref[...].astype(F32)).astype(o_ref.dtype)


def _merge(hx2, a2, yb2, w_in_b, w_a_out_b, col0, tm, tj):
    n, d = hx2.shape
    dc = a2.shape[1]
    nj = d // tj
    return pl.pallas_call(
        _merge_kernel,
        grid=(n // tm, nj),
        in_specs=[pl.BlockSpec((tm, d), lambda i, j: (i, 0)),
                  pl.BlockSpec((tm, dc), lambda i, j: (i, 0)),
                  pl.BlockSpec((tm, tj), lambda i, j: (i, j)),
                  pl.BlockSpec((d, tj), lambda i, j: (0, col0 // tj + j)),
                  pl.BlockSpec((d, tj), lambda i, j: (0, col0 // tj + nj + j)),
                  pl.BlockSpec((dc, tj), lambda i, j: (0, j))],
        out_specs=pl.BlockSpec((tm, tj), lambda i, j: (i, j)),
        out_shape=jax.ShapeDtypeStruct((n, d), BF16),
        compiler_params=_params(2),
        name="merge",
    )(hx2, a2, yb2, w_in_b, w_in_b, w_a_out_b)


def _mix_kernel(m_ref, x_ref, lig_ref, lib_ref, wo_ref, g1_ref, b1_ref, rwh_ref, rwl_ref, rb_ref, tri_ref,
                h1_ref, h1p_ref, eid_ref, wts_ref, rank_ref, cnt_ref, carry_ref, z_ref, *, rows):
    step = pl.program_id(0)

    @pl.when(step == 0)
    def _():
        carry_ref[...] = jnp.zeros(carry_ref.shape, F32)

    for r0 in range(0, m_ref.shape[0], rows):
        rs = slice(r0, r0 + rows)
        h0 = _layer_norm(x_ref[rs, :], lig_ref[...], lib_ref[...])
        z_ref[rs, :] = DEEPNORM_ALPHA * h0 + _dot(m_ref[rs, :], wo_ref[...])

    for r0 in range(0, m_ref.shape[0], rows):
        rs = slice(r0, r0 + rows)
        h1 = _layer_norm(z_ref[rs, :], g1_ref[...], b1_ref[...])
        h1_ref[rs, :] = h1
        h_hi = h1.astype(BF16)
        half = h1.shape[1] // 2
        h1p_ref[rs, :] = _pack_pairs(h_hi[:, :half], h_hi[:, half:])

        h_lo = (h1 - h_hi.astype(F32)).astype(BF16)
        nt = (((1,), (1,)), ((), ()))
        rwh = rwh_ref[...]
        logits = (lax.dot_general(rwh, h_hi, nt, preferred_element_type=F32)
                  + lax.dot_general(rwh, h_lo, nt, preferred_element_type=F32)
                  + lax.dot_general(rwl_ref[...], h_hi, nt, preferred_element_type=F32)
                  + rb_ref[...])
        n_exp = logits.shape[0]
        e_iota = lax.broadcasted_iota(I32, logits.shape, 0)
        work = logits
        vals, hots = [], []
        for k in range(TOP_K):
            m = jnp.max(work, axis=0, keepdims=True)
            idx = jnp.min(jnp.where(work == m, e_iota, n_exp), axis=0, keepdims=True)
            hot = e_iota == idx
            vals.append(m)
            hots.append(hot)
            eid_ref[k:k + 1, rs] = idx
            work = jnp.where(hot, -jnp.inf, work)
        exps = [jnp.exp(v - vals[0]) for v in vals]
        denom = exps[0]
        for e in exps[1:]:
            denom = denom + e
        for k in range(TOP_K):
            wts_ref[k:k + 1, rs] = exps[k] / denom

        multi = hots[0]
        for hot in hots[1:]:
            multi = multi | hot
        multi_f = jnp.where(multi, 1.0, 0.0).astype(F32)
        prefix = _dot(multi_f.astype(BF16), tri_ref[...]) + carry_ref[:, 0:1]
        for k in range(TOP_K):
            rank_ref[k:k + 1, rs] = jnp.sum(jnp.where(hots[k], prefix, 0.0), axis=0, keepdims=True).astype(I32)
        carry_ref[...] = carry_ref[...] + jnp.sum(multi_f, axis=1, keepdims=True)
    cnt_ref[...] = carry_ref[...]


def _mix_ln1(m2, x2, ln_in_g, ln_in_b, w_o_b, ln1_g, ln1_b, rw_hi, rw_lo, rb, tm):
    n, d = x2.shape
    n_exp = rw_hi.shape[0]
    rows = min(MIX_HALF, tm)
    tri = (jnp.arange(rows, dtype=I32)[:, None] < jnp.arange(rows, dtype=I32)[None, :]).astype(BF16)
    row = lambda i: (i, 0)
    fixed = lambda i: (0, 0)
    col = lambda i: (0, i)
    return pl.pallas_call(
        functools.partial(_mix_kernel, rows=rows),
        grid=(n // tm,),
        in_specs=[pl.BlockSpec((tm, d), row), pl.BlockSpec((tm, d), row),
                  pl.BlockSpec((1, d), fixed), pl.BlockSpec((1, d), fixed),
                  pl.BlockSpec((d, d), fixed),
                  pl.BlockSpec((1, d), fixed), pl.BlockSpec((1, d), fixed),
                  pl.BlockSpec((n_exp, d), fixed), pl.BlockSpec((n_exp, d), fixed),
                  pl.BlockSpec((n_exp, 1), fixed),
                  pl.BlockSpec((rows, rows), fixed)],
        out_specs=[pl.BlockSpec((tm, d), row), pl.BlockSpec((tm, d // 2), row),
                   pl.BlockSpec((TOP_K, tm), col), pl.BlockSpec((TOP_K, tm), col), pl.BlockSpec((TOP_K, tm), col),
                   pl.BlockSpec((n_exp, 128), fixed)],
        out_shape=[jax.ShapeDtypeStruct((n, d), F32),
                   jax.ShapeDtypeStruct((n, d // 2), U32),
                   jax.ShapeDtypeStruct((TOP_K, n), I32),
                   jax.ShapeDtypeStruct((TOP_K, n), F32),
                   jax.ShapeDtypeStruct((TOP_K, n), I32),
                   jax.ShapeDtypeStruct((n_exp, 128), F32)],
        scratch_shapes=[pltpu.VMEM((n_exp, 128), F32), pltpu.VMEM((tm, d), F32)],
        compiler_params=_params(1),
        name="mix_ln1",
    )(m2, x2, ln_in_g, ln_in_b, w_o_b, ln1_g, ln1_b, rw_hi, rw_lo, rb, tri)


def _row_copy(src_ref, src_row, dst_ref, dst_row, sem):
    return pltpu.make_async_copy(src_ref.at[pl.ds(src_row, 1)], dst_ref.at[pl.ds(dst_row, 1)], sem)


def _subtile(q):
    return pl.ds(pl.multiple_of(q * SLOT_ROWS, SLOT_ROWS), SLOT_ROWS)


def _dispatch_kernel(pos_ref, gl_ref, gn_ref, used_ref, h_ref, xs_ref, zero_ref, sem, zsem, *, tt):
    base = pl.program_id(0) * (TOP_K * tt)

    @pl.when(pl.program_id(0) == 0)
    def _():
        zero_ref[...] = jnp.zeros(zero_ref.shape, zero_ref.dtype)
        n_exp = gl_ref.shape[0]
        n_sub = xs_ref.shape[0] // SLOT_ROWS

        def zero_copy(q):
            return pltpu.make_async_copy(zero_ref, xs_ref.at[_subtile(q)], zsem)

        def group_tail(e, start):
            @pl.when(gn_ref[e] > 0)
            def _():
                c = zero_copy(gl_ref[e])
                c.start() if start else c.wait()

        def unused(q, start):
            c = zero_copy(q)
            c.start() if start else c.wait()

        for start in (True, False):
            lax.fori_loop(0, n_exp, lambda e, c: (group_tail(e, start), c)[1], 0)
            lax.fori_loop(used_ref[1], n_sub, lambda q, c: (unused(q, start), c)[1], 0)

    def issue(t, c):
        for k in range(TOP_K):
            _row_copy(h_ref, t, xs_ref, pos_ref[base + k * tt + t], sem).start(priority=k % 2)
        return c

    lax.fori_loop(0, tt, issue, 0)
    for k in range(TOP_K):
        pltpu.make_async_copy(h_ref, xs_ref.at[pl.ds(0, tt)], sem).wait()


def _dispatch(h1p, pos_tiles, group_last, group_nsub, used, n_slots, tt):
    n, dw = h1p.shape
    grid_spec = pltpu.PrefetchScalarGridSpec(
        num_scalar_prefetch=4,
        grid=(n // tt,),
        in_specs=[pl.BlockSpec((tt, dw), lambda i, *_: (i, 0))],
        out_specs=pl.BlockSpec(memory_space=pl.ANY),
        scratch_shapes=[pltpu.VMEM((SLOT_ROWS, dw), h1p.dtype),
                        pltpu.SemaphoreType.DMA(()), pltpu.SemaphoreType.DMA(())],
    )
    return pl.pallas_call(
        functools.partial(_dispatch_kernel, tt=tt),
        grid_spec=grid_spec,
        out_shape=jax.ShapeDtypeStruct((n_slots, dw), h1p.dtype),
        compiler_params=_params(1),
        name="dispatch",
    )(pos_tiles, group_last, group_nsub, used, h1p)


def _expert_kernel(ge_ref, gs_ref, gn_ref, nu_ref, xs_ref, wg_ref, wl_ref, wd_ref, bg_ref, bl_ref, bd_ref,
                   ys_ref, xb_ref, acc_ref, stage_ref, wgb_ref, wlb_ref, wdb_ref, sem_in, sem_out, *, nj):
    t = pl.program_id(0)
    n_groups = nu_ref[0]
    n_items = n_groups * nj
    c = jnp.maximum(t - 1, 0)
    s = c // nj
    j = c - s * nj
    cast_slot = t % 2
    use_slot = 1 - cast_slot
    d, tn = wgb_ref.shape[1:]
    half = d // 2
    rows = _subtile

    def cast_slice(i):
        r = pl.ds(pl.multiple_of(i * (d // CAST_SLICES), d // CAST_SLICES), d // CAST_SLICES)
        wgb_ref[cast_slot, r, :] = wg_ref[0, r, :].astype(BF16)
        wlb_ref[cast_slot, r, :] = wl_ref[0, r, :].astype(BF16)
        r = pl.ds(pl.multiple_of(i * (tn // CAST_SLICES), tn // CAST_SLICES), tn // CAST_SLICES)
        wdb_ref[cast_slot, r, :] = wd_ref[0, r, :].astype(BF16)

    def cast_range(lo, hi):
        lax.fori_loop(lo, hi, lambda i, z: (cast_slice(i), z)[1], 0)

    def out_copy(first_sub, q):
        return pltpu.make_async_copy(acc_ref.at[rows(q)], ys_ref.at[rows(first_sub + q)], sem_out.at[q])

    def in_copy(first_sub, q, slot):
        return pltpu.make_async_copy(xs_ref.at[rows(first_sub + q)], stage_ref.at[slot], sem_in.at[slot])

    def prefetch(first_sub, nsub):
        lax.fori_loop(0, jnp.minimum(nsub, stage_ref.shape[0]),
                      lambda q, z: (in_copy(first_sub, q, q).start(), z)[1], 0)

    @pl.when(t == 0)
    def _():
        cast_range(0, CAST_SLICES)

    @pl.when((t >= 1) & (t <= n_items))
    def _():
        first_sub = gs_ref[s]
        nsub = gn_ref[s]

        @pl.when(j == 0)
        def _():
            sp = jnp.maximum(s - 1, 0)
            prev_first = gs_ref[sp]
            prev_nsub = jnp.where(s > 0, gn_ref[sp], 0)
            depth = stage_ref.shape[0]

            @pl.when(s == 0)
            def _():
                prefetch(first_sub, nsub)

            def load(q, z):
                slot = q % depth
                in_copy(first_sub, q, slot).wait()
                lo, hi = _unpack_pairs(stage_ref[slot])
                xb_ref[rows(q), 0:half] = lo
                xb_ref[rows(q), half:] = hi

                @pl.when(q + depth < nsub)
                def _():
                    in_copy(first_sub, q + depth, slot).start()

                @pl.when(q < prev_nsub)
                def _():
                    out_copy(prev_first, q).wait()

                acc_ref[rows(q), :] = jnp.broadcast_to(bd_ref[0], (SLOT_ROWS, d))
                return z

            lax.fori_loop(0, nsub, load, 0)
            lax.fori_loop(nsub, prev_nsub, lambda q, z: (out_copy(prev_first, q).wait(), z)[1], 0)

        def run(q, n):
            r = pl.ds(pl.multiple_of(q * SLOT_ROWS, SLOT_ROWS), n * SLOT_ROWS)
            x = xb_ref[r, :]
            gate = _dot(x, wgb_ref[use_slot]) + bg_ref[0]
            lin = _dot(x, wlb_ref[use_slot]) + bl_ref[0]
            gate = jnp.minimum(gate, SWIGLU_LIMIT)
            lin = jnp.clip(lin, -SWIGLU_LIMIT, SWIGLU_LIMIT)
            act = (lin + 1.0) * gate * jax.nn.sigmoid(SWIGLU_ALPHA * gate)
            acc_ref[r, :] += _dot(act.astype(BF16), wdb_ref[use_slot])

            @pl.when(j == nj - 1)
            def _():
                for u in range(n):
                    out_copy(first_sub, q + u).start()

        a = nsub // 3
        r = nsub - 3 * a
        two_quads = (r == 2) & (a >= 2)
        first_n = jnp.where(nsub == 1, 1, jnp.where(r == 0, 3, jnp.where((r == 1) | two_quads, 4, 2)))
        for n in (1, 2, 3, 4):
            @pl.when(first_n == n)
            def _():
                run(0, n)
                for i in range(CAST_SLICES):
                    cast_slice(i)

        @pl.when(two_quads)
        def _():
            run(4, 4)

        done = first_n + jnp.where(two_quads, 4, 0)
        lax.fori_loop(0, (nsub - done) // 3, lambda p, z: (run(done + 3 * p, 3), z)[1], 0)

        @pl.when((j == nj - 1) & (s + 1 < n_groups))
        def _():
            nxt = jnp.minimum(s + 1, n_groups - 1)
            prefetch(gs_ref[nxt], gn_ref[nxt])

        @pl.when(t == n_items)
        def _():
            lax.fori_loop(0, nsub, lambda q, z: (out_copy(first_sub, q).wait(), z)[1], 0)
            acc_ref[rows(0), :] = jnp.zeros((SLOT_ROWS, d), F32)
            n_sub = ys_ref.shape[0] // SLOT_ROWS

            def tail_copy(q):
                return pltpu.make_async_copy(acc_ref.at[rows(0)], ys_ref.at[rows(q)], sem_out.at[0])

            lax.fori_loop(nu_ref[1], n_sub, lambda q, z: (tail_copy(q).start(), z)[1], 0)
            lax.fori_loop(nu_ref[1], n_sub, lambda q, z: (tail_copy(q).wait(), z)[1], 0)


def _experts(xs, group_expert, group_first, group_nsub, used, w_gate_up, b_gate_up, w_down, b_down, tn):
    n_slots, dw = xs.shape
    d = 2 * dw
    n_exp, _, de2 = w_gate_up.shape
    de = de2 // 2
    nj = de // tn
    max_items = group_expert.shape[0] * nj
    rmax = GROUP_SUBTILES * SLOT_ROWS

    def fetched(t, nu):
        w = jnp.minimum(t, nu[0] * nj - 1)
        return w // nj, w % nj

    def computed(t, nu):
        w = jnp.clip(t - 1, 0, nu[0] * nj - 1)
        return w // nj, w % nj

    def wg_map(t, ge, gs, gn, nu):
        g, j = fetched(t, nu)
        return ge[g], 0, j

    def wl_map(t, ge, gs, gn, nu):
        g, j = fetched(t, nu)
        return ge[g], 0, nj + j

    def wd_map(t, ge, gs, gn, nu):
        g, j = fetched(t, nu)
        return ge[g], j, 0

    def bg_map(t, ge, gs, gn, nu):
        g, j = computed(t, nu)
        return ge[g], 0, j

    def bl_map(t, ge, gs, gn, nu):
        g, j = computed(t, nu)
        return ge[g], 0, nj + j

    def bd_map(t, ge, gs, gn, nu):
        g, _ = computed(t, nu)
        return ge[g], 0, 0

    grid_spec = pltpu.PrefetchScalarGridSpec(
        num_scalar_prefetch=4,
        grid=(used[0] * nj + 1,),
        in_specs=[pl.BlockSpec(memory_space=pl.ANY),
                  pl.BlockSpec((1, d, tn), wg_map), pl.BlockSpec((1, d, tn), wl_map),
                  pl.BlockSpec((1, tn, d), wd_map),
                  pl.BlockSpec((1, 1, tn), bg_map), pl.BlockSpec((1, 1, tn), bl_map),
                  pl.BlockSpec((1, 1, d), bd_map)],
        out_specs=pl.BlockSpec(memory_space=pl.ANY),
        scratch_shapes=[pltpu.VMEM((rmax, d), BF16),
                        pltpu.VMEM((rmax, d), F32),
                        pltpu.VMEM((STAGE_DEPTH, SLOT_ROWS, dw), U32),
                        pltpu.VMEM((2, d, tn), BF16), pltpu.VMEM((2, d, tn), BF16), pltpu.VMEM((2, tn, d), BF16),
                        pltpu.SemaphoreType.DMA((STAGE_DEPTH,)),
                        pltpu.SemaphoreType.DMA((GROUP_SUBTILES,))],
    )
    return pl.pallas_call(
        functools.partial(_expert_kernel, nj=nj),
        grid_spec=grid_spec,
        out_shape=jax.ShapeDtypeStruct((n_slots, d), F32),
        compiler_params=_params(1),
        name="experts",
    )(group_expert, group_first, group_nsub, used, xs, w_gate_up, w_gate_up, w_down,
      b_gate_up.reshape(n_exp, 1, de2), b_gate_up.reshape(n_exp, 1, de2), b_down.reshape(n_exp, 1, d))


def _combine_kernel(pos_ref, h_ref, w_ref, g_ref, b_ref, ys_ref, o_ref, buf_ref, sem, *, tt, n_steps):
    i = pl.program_id(0)
    slot = i % 2
    ch = min(COMBINE_CHUNK, tt)

    def issue_rows(step, dst_slot, t0):
        base = step * (TOP_K * tt)
        for r in range(ch):
            for k in range(TOP_K):
                _row_copy(ys_ref, pos_ref[base + k * tt + t0 + r], buf_ref.at[dst_slot, k], t0 + r,
                          sem.at[dst_slot]).start(priority=k % 2)

    def wait_rows(src_slot):
        for k in range(TOP_K):
            pltpu.make_async_copy(ys_ref.at[pl.ds(0, tt)], buf_ref.at[src_slot, k], sem.at[src_slot]).wait()

    @pl.when(i == 0)
    def _():
        lax.fori_loop(0, tt // ch, lambda c, z: (issue_rows(0, 0, c * ch), z)[1], 0)

    wait_rows(slot)

    @pl.when(i + 1 < n_steps)
    def _():
        lax.fori_loop(0, tt // ch, lambda c, z: (issue_rows(i + 1, 1 - slot, c * ch), z)[1], 0)

    w = w_ref[...]
    y = DEEPNORM_ALPHA * h_ref[...]
    for k in range(TOP_K):
        y = y + w[:, k:k + 1] * buf_ref[slot, k]
    o_ref[...] = _layer_norm(y, g_ref[...], b_ref[...])


def _combine(h1, pos_tiles, wts_t, ys, ln2_g, ln2_b, tt):
    n, d = h1.shape
    n_steps = n // tt
    grid_spec = pltpu.PrefetchScalarGridSpec(
        num_scalar_prefetch=1,
        grid=(n_steps,),
        in_specs=[pl.BlockSpec((tt, d), lambda i, pos: (i, 0)),
                  pl.BlockSpec((tt, TOP_K), lambda i, pos: (i, 0)),
                  pl.BlockSpec((1, d), lambda i, pos: (0, 0)),
                  pl.BlockSpec((1, d), lambda i, pos: (0, 0)),
                  pl.BlockSpec(memory_space=pl.ANY)],
        out_specs=pl.BlockSpec((tt, d), lambda i, pos: (i, 0)),
        scratch_shapes=[pltpu.VMEM((2, TOP_K, tt, d), F32), pltpu.SemaphoreType.DMA((2,))],
    )
    return pl.pallas_call(
        functools.partial(_combine_kernel, tt=tt, n_steps=n_steps),
        grid_spec=grid_spec,
        out_shape=jax.ShapeDtypeStruct((n, d), F32),
        compiler_params=_params(1),
        name="combine",
    )(pos_tiles, h1, wts_t, ln2_g, ln2_b, ys)


def _routing_tables(counts, eid, rank, n_tok):
    n_exp = counts.shape[0]
    e_ids = jnp.arange(n_exp, dtype=I32)
    nsub = (counts + SLOT_ROWS - 1) // SLOT_ROWS
    ngrp = (nsub + GROUP_SUBTILES - 1) // GROUP_SUBTILES
    base = nsub // jnp.maximum(ngrp, 1)
    rem = nsub - base * ngrp
    first_sub = jnp.cumsum(nsub) - nsub
    pos = jnp.sum(jnp.where(eid[..., None] == e_ids, first_sub * SLOT_ROWS, 0), axis=-1) + rank

    n_subtiles = (n_tok * TOP_K) // SLOT_ROWS + n_exp
    max_groups = (n_subtiles + GROUP_SUBTILES - 1) // GROUP_SUBTILES + n_exp
    grp_end = jnp.cumsum(ngrp)
    g_ids = jnp.arange(max_groups, dtype=I32)
    g_exp = jnp.minimum(jnp.sum((grp_end[None, :] <= g_ids[:, None]).astype(I32), axis=1), n_exp - 1)
    pick = lambda tab: jnp.sum(jnp.where(g_exp[:, None] == e_ids[None, :], tab[None, :], 0), axis=1)
    local = g_ids - pick(grp_end - ngrp)
    g_base, g_rem = pick(base), pick(rem)
    g_nsub = g_base + (local < g_rem).astype(I32)
    g_first = pick(first_sub) + local * g_base + jnp.minimum(local, g_rem)
    used = jnp.stack([grp_end[-1], jnp.sum(nsub)]).astype(I32)
    return dict(pos=pos, n_slots=n_subtiles * SLOT_ROWS, used=used,
                group_expert=g_exp.astype(I32), group_first=g_first.astype(I32), group_nsub=g_nsub.astype(I32),
                expert_last=(first_sub + nsub - 1).astype(I32), expert_nsub=nsub.astype(I32))


def _tiles(seq, n_tok):
    return dict(
        ln_rows=min(1024, n_tok),
        conv_rows=min(1024, seq), conv_cols=512,
        pool_rows=min(1024, seq),
        merge_rows=min(1024, n_tok), merge_cols=512,
        mix_rows=min(512, n_tok),
        route_rows=min(512, n_tok), dispatch_rows=min(2048, n_tok),
        expert_cols=256,
    )


def kernel(x, meta_tokens, ln_in_g, ln_in_b, w_in, conv_w, w_a_out, pool_w, pool_scale, w_o, ln1_g, ln1_b,
           router_w, router_b, w_gate_up, b_gate_up, w_down, b_down, ln2_g, ln2_b):
    bsz, seq, d = x.shape
    assert w_in.shape[0] == DEPTH and meta_tokens.shape[0] == N_META == HALO
    d_conv = conv_w.shape[-1]
    n_groups, pool_cin, pool_cout = pool_w.shape[1:]
    d_pool = n_groups * pool_cin
    assert n_groups == len(POOL_WINDOWS) and n_groups * pool_cout == d
    n_exp = router_w.shape[-1]
    n_tok = bsz * seq
    t = _tiles(seq, n_tok)
    row = lambda v: v.reshape(1, -1).astype(F32)

    w_in_b = w_in[0].astype(BF16)
    x2 = x.reshape(n_tok, d)
    lig, lib = row(ln_in_g), row(ln_in_b)

    hx2 = _ln_cast(x2, lig, lib, t["ln_rows"])
    hm = _ln_cast(meta_tokens.astype(F32), lig, lib, N_META)
    hx = hx2.reshape(bsz, seq, d)

    a = _conv_branch(hx, hm, w_in_b, conv_w[0], d_conv, t["conv_rows"], min(t["conv_cols"], d_conv))
    yb = _pool_branch(hx, hm, w_in_b, pool_w[0].astype(BF16), row(pool_scale[0]), 3 * d_conv, t["pool_rows"])
    m2 = _merge(hx2, a.reshape(n_tok, d_conv), yb.reshape(n_tok, d), w_in_b, w_a_out[0].astype(BF16),
                3 * d_conv + d_pool, t["merge_rows"], min(t["merge_cols"], d))

    rw_t = router_w[0].T
    rw_hi = rw_t.astype(BF16)
    rw_lo = (rw_t - rw_hi.astype(F32)).astype(BF16)
    h1, h1p, eid, wts, rank, cnt = _mix_ln1(m2, x2, lig, lib, w_o[0].astype(BF16), row(ln1_g[0]), row(ln1_b[0]),
                                            rw_hi, rw_lo, router_b[0].reshape(n_exp, 1), t["mix_rows"])

    rt = _routing_tables(cnt[:, 0].astype(I32), eid, rank, n_tok)
    def pos_tiles_for(rows):
        return rt["pos"].reshape(TOP_K, n_tok // rows, rows).transpose(1, 0, 2).reshape(-1)

    tt = t["route_rows"]
    pos_tiles = pos_tiles_for(tt)
    xs = _dispatch(h1p, pos_tiles_for(t["dispatch_rows"]), rt["expert_last"], rt["expert_nsub"], rt["used"],
                   rt["n_slots"], t["dispatch_rows"])
    ys = _experts(xs, rt["group_expert"], rt["group_first"], rt["group_nsub"], rt["used"],
                  w_gate_up[0], b_gate_up[0], w_down[0], b_down[0], min(t["expert_cols"], w_down.shape[-2]))
    out = _combine(h1, pos_tiles, wts.T, ys, row(ln2_g[0]), row(ln2_b[0]), tt)
    return out.reshape(bsz, seq, d)
```

```python
import functools

import jax
import jax.numpy as jnp
from jax import lax
from jax.experimental import pallas as pl
from jax.experimental.pallas import tpu as pltpu

F32 = jnp.float32
BF16 = jnp.bfloat16
I32 = jnp.int32
U32 = jnp.uint32

N_META = 16
CONV_K = 3
POOL_WINDOWS = (2, 4, 8, 16)
TOP_K = 4
SWIGLU_LIMIT = 7.0
SWIGLU_ALPHA = 1.702
LN_EPS = 1e-5
DEPTH = 1
DEEPNORM_ALPHA = (2.0 * DEPTH) ** 0.25

HALO = 16
POOL_PAD = 128
POOL_CHUNK = 256
SLOT_ROWS = 256
GROUP_SUBTILES = 9
CAST_SLICES = 8
STAGE_DEPTH = 8
MIX_HALF = 256
COMBINE_CHUNK = 16
V7X_VMEM_LIMIT = 58 * 1024 * 1024

_ARB = "arbitrary"


def _params(n_axes):
    return pltpu.CompilerParams(dimension_semantics=(_ARB,) * n_axes, vmem_limit_bytes=V7X_VMEM_LIMIT)


def _layer_norm(x, g, b):
    mu = jnp.mean(x, axis=-1, keepdims=True)
    xc = x - mu
    var = jnp.mean(xc * xc, axis=-1, keepdims=True)
    return xc * lax.rsqrt(var + LN_EPS) * g + b


def _dot(a, b):
    return jnp.dot(a, b, preferred_element_type=F32)


def _pack_pairs(lo_bf16, hi_bf16):
    lo = lax.bitcast_convert_type(lo_bf16.astype(F32), U32) >> 16
    hi = lax.bitcast_convert_type(hi_bf16.astype(F32), U32) & jnp.uint32(0xFFFF0000)
    return hi | lo


def _unpack_pairs(words):
    lo = lax.bitcast_convert_type(words << 16, F32).astype(BF16)
    hi = lax.bitcast_convert_type(words & jnp.uint32(0xFFFF0000), F32).astype(BF16)
    return lo, hi


def _ln_cast_kernel(x_ref, g_ref, b_ref, o_ref):
    o_ref[...] = _layer_norm(x_ref[...], g_ref[...], b_ref[...]).astype(o_ref.dtype)


def _ln_cast(x2d, g, b, tr):
    n, d = x2d.shape
    return pl.pallas_call(
        _ln_cast_kernel,
        grid=(n // tr,),
        in_specs=[pl.BlockSpec((tr, d), lambda i: (i, 0)),
                  pl.BlockSpec((1, d), lambda i: (0, 0)),
                  pl.BlockSpec((1, d), lambda i: (0, 0))],
        out_specs=pl.BlockSpec((tr, d), lambda i: (i, 0)),
        out_shape=jax.ShapeDtypeStruct((n, d), BF16),
        compiler_params=_params(1),
        name="ln_cast",
    )(x2d, g, b)


def _conv_kernel(h_ref, halo_ref, hm_ref, wb_ref, wc_ref, wu_ref, cw_ref, o_ref, he_ref, cu_ref, *, tm):
    i = pl.program_id(1)
    j = pl.program_id(2)

    @pl.when(j == 0)
    def _():
        he_ref[HALO:, :] = h_ref[0]

    @pl.when((j == 0) & (i == 0))
    def _():
        he_ref[0:HALO, :] = hm_ref[...]

    @pl.when((j == 0) & (i > 0))
    def _():
        he_ref[0:HALO, :] = halo_ref[0]

    he = he_ref[...]
    cu_ref[...] = _dot(he, wc_ref[...]) * _dot(he, wu_ref[...])
    bg = _dot(he_ref[HALO:, :], wb_ref[...])
    cw = cw_ref[...]
    y = cw[CONV_K - 1:CONV_K, :] * cu_ref[HALO:HALO + tm, :]
    for k in range(CONV_K - 1):
        s = HALO - (CONV_K - 1) + k
        y = y + cw[k:k + 1, :] * cu_ref[s:s + tm, :]
    o_ref[0] = (bg * y).astype(o_ref.dtype)


def _conv_branch(hx, hm, w_in_b, conv_w, d_conv, tm, tc):
    bsz, seq, d = hx.shape
    nc = d_conv // tc
    hb = tm // HALO
    return pl.pallas_call(
        functools.partial(_conv_kernel, tm=tm),
        grid=(bsz, seq // tm, nc),
        in_specs=[pl.BlockSpec((1, tm, d), lambda b, i, j: (b, i, 0)),
                  pl.BlockSpec((1, HALO, d), lambda b, i, j: (b, jnp.maximum(i * hb - 1, 0), 0)),
                  pl.BlockSpec((HALO, d), lambda b, i, j: (0, 0)),
                  pl.BlockSpec((d, tc), lambda b, i, j: (0, j)),
                  pl.BlockSpec((d, tc), lambda b, i, j: (0, nc + j)),
                  pl.BlockSpec((d, tc), lambda b, i, j: (0, 2 * nc + j)),
                  pl.BlockSpec((CONV_K, tc), lambda b, i, j: (0, j))],
        out_specs=pl.BlockSpec((1, tm, tc), lambda b, i, j: (b, i, j)),
        out_shape=jax.ShapeDtypeStruct((bsz, seq, d_conv), BF16),
        scratch_shapes=[pltpu.VMEM((tm + HALO, d), BF16), pltpu.VMEM((tm + HALO, tc), F32)],
        compiler_params=_params(3),
        name="conv_branch",
    )(hx, hx, hm, w_in_b, w_in_b, w_in_b, conv_w)


def _pool_kernel(h_ref, halo_ref, hm_ref, wv_ref, band_ref, pw_ref, ps_ref, o_ref, he_ref, vh_ref, vl_ref, *, tm):
    i = pl.program_id(1)
    lead = POOL_PAD - HALO
    he_ref[0:lead, :] = jnp.zeros((lead, he_ref.shape[1]), he_ref.dtype)
    he_ref[POOL_PAD:, :] = h_ref[0]

    @pl.when(i == 0)
    def _():
        he_ref[lead:POOL_PAD, :] = hm_ref[...]

    @pl.when(i > 0)
    def _():
        he_ref[lead:POOL_PAD, :] = halo_ref[0]

    v = _dot(he_ref[...], wv_ref[...])
    v_hi = v.astype(BF16)
    vh_ref[...] = v_hi
    vl_ref[...] = (v - v_hi.astype(F32)).astype(BF16)
    ng, cin, cout = pw_ref.shape
    rows = min(POOL_CHUNK, tm)
    for c in range(tm // rows):
        r0 = c * rows
        for g in range(ng):
            band = band_ref[g]
            win = (slice(r0, r0 + rows + POOL_PAD), slice(g * cin, (g + 1) * cin))
            pooled = _dot(band, vh_ref[win]) + _dot(band, vl_ref[win])
            y = _dot(pooled.astype(BF16), pw_ref[g]) * ps_ref[:, g * cout:(g + 1) * cout]
            o_ref[0, r0:r0 + rows, g * cout:(g + 1) * cout] = y.astype(o_ref.dtype)


def _pool_band(rows):
    t = jnp.arange(rows, dtype=I32)[:, None] + POOL_PAD
    s = jnp.arange(rows + POOL_PAD, dtype=I32)[None, :]
    bands = []
    for w in POOL_WINDOWS:
        inside = ((s <= t) & (s > t - w)).astype(F32) / w
        bands.append(inside - (s == t).astype(F32))
    return jnp.stack(bands).astype(BF16)


def _pool_branch(hx, hm, w_in_b, pool_w_b, pool_scale, col0, tm):
    bsz, seq, d = hx.shape
    ng, cin, cout = pool_w_b.shape
    d_pool = ng * cin
    hb = tm // HALO
    band = _pool_band(min(POOL_CHUNK, tm))
    return pl.pallas_call(
        functools.partial(_pool_kernel, tm=tm),
        grid=(bsz, seq // tm),
        in_specs=[pl.BlockSpec((1, tm, d), lambda b, i: (b, i, 0)),
                  pl.BlockSpec((1, HALO, d), lambda b, i: (b, jnp.maximum(i * hb - 1, 0), 0)),
                  pl.BlockSpec((HALO, d), lambda b, i: (0, 0)),
                  pl.BlockSpec((d, d_pool), lambda b, i: (0, col0 // d_pool)),
                  pl.BlockSpec(band.shape, lambda b, i: (0, 0, 0)),
                  pl.BlockSpec((ng, cin, cout), lambda b, i: (0, 0, 0)),
                  pl.BlockSpec((1, ng * cout), lambda b, i: (0, 0))],
        out_specs=pl.BlockSpec((1, tm, ng * cout), lambda b, i: (b, i, 0)),
        out_shape=jax.ShapeDtypeStruct((bsz, seq, ng * cout), BF16),
        scratch_shapes=[pltpu.VMEM((tm + POOL_PAD, d), BF16),
                        pltpu.VMEM((tm + POOL_PAD, d_pool), BF16),
                        pltpu.VMEM((tm + POOL_PAD, d_pool), BF16)],
        compiler_params=_params(2),
        name="pool_branch",
    )(hx, hx, hm, w_in_b, band, pool_w_b, pool_scale)


def _merge_kernel(h_ref, a_ref, yb_ref, wga_ref, wgb_ref, wao_ref, o_ref):
    h = h_ref[...]
    ga = jax.nn.sigmoid(_dot(h, wga_ref[...]))
    gb = jax.nn.sigmoid(_dot(h, wgb_ref[...]))
    ya = _dot(a_ref[...], wao_ref[...])
    o_ref[...] = (ga * ya + gb * yb_ref[...].astype(F32)).astype(o_ref.dtype)


def _merge(hx2, a2, yb2, w_in_b, w_a_out_b, col0, tm, tj):
    n, d = hx2.shape
    dc = a2.shape[1]
    nj = d // tj
    return pl.pallas_call(
        _merge_kernel,
        grid=(n // tm, nj),
        in_specs=[pl.BlockSpec((tm, d), lambda i, j: (i, 0)),
                  pl.BlockSpec((tm, dc), lambda i, j: (i, 0)),
                  pl.BlockSpec((tm, tj), lambda i, j: (i, j)),
                  pl.BlockSpec((d, tj), lambda i, j: (0, col0 // tj + j)),
                  pl.BlockSpec((d, tj), lambda i, j: (0, col0 // tj + nj + j)),
                  pl.BlockSpec((dc, tj), lambda i, j: (0, j))],
        out_specs=pl.BlockSpec((tm, tj), lambda i, j: (i, j)),
        out_shape=jax.ShapeDtypeStruct((n, d), BF16),
        compiler_params=_params(2),
        name="merge",
    )(hx2, a2, yb2, w_in_b, w_in_b, w_a_out_b)


def _mix_kernel(m_ref, x_ref, lig_ref, lib_ref, wo_ref, g1_ref, b1_ref, rwh_ref, rwl_ref, rb_ref, tri_ref,
                h1_ref, h1p_ref, eid_ref, wts_ref, rank_ref, cnt_ref, carry_ref, z_ref, *, rows):
    step = pl.program_id(0)

    @pl.when(step == 0)
    def _():
        carry_ref[...] = jnp.zeros(carry_ref.shape, F32)

    for r0 in range(0, m_ref.shape[0], rows):
        rs = slice(r0, r0 + rows)
        h0 = _layer_norm(x_ref[rs, :], lig_ref[...], lib_ref[...])
        z_ref[rs, :] = DEEPNORM_ALPHA * h0 + _dot(m_ref[rs, :], wo_ref[...])

    for r0 in range(0, m_ref.shape[0], rows):
        rs = slice(r0, r0 + rows)
        h1 = _layer_norm(z_ref[rs, :], g1_ref[...], b1_ref[...])
        h1_ref[rs, :] = h1
        h_hi = h1.astype(BF16)
        half = h1.shape[1] // 2
        h1p_ref[rs, :] = _pack_pairs(h_hi[:, :half], h_hi[:, half:])

        h_lo = (h1 - h_hi.astype(F32)).astype(BF16)
        nt = (((1,), (1,)), ((), ()))
        rwh = rwh_ref[...]
        logits = (lax.dot_general(rwh, h_hi, nt, preferred_element_type=F32)
                  + lax.dot_general(rwh, h_lo, nt, preferred_element_type=F32)
                  + lax.dot_general(rwl_ref[...], h_hi, nt, preferred_element_type=F32)
                  + rb_ref[...])
        n_exp = logits.shape[0]
        e_iota = lax.broadcasted_iota(I32, logits.shape, 0)
        work = logits
        vals, hots = [], []
        for k in range(TOP_K):
            m = jnp.max(work, axis=0, keepdims=True)
            idx = jnp.min(jnp.where(work == m, e_iota, n_exp), axis=0, keepdims=True)
            hot = e_iota == idx
            vals.append(m)
            hots.append(hot)
            eid_ref[k:k + 1, rs] = idx
            work = jnp.where(hot, -jnp.inf, work)
        exps = [jnp.exp(v - vals[0]) for v in vals]
        denom = exps[0]
        for e in exps[1:]:
            denom = denom + e
        for k in range(TOP_K):
            wts_ref[k:k + 1, rs] = exps[k] / denom

        multi = hots[0]
        for hot in hots[1:]:
            multi = multi | hot
        multi_f = jnp.where(multi, 1.0, 0.0).astype(F32)
        prefix = _dot(multi_f.astype(BF16), tri_ref[...]) + carry_ref[:, 0:1]
        for k in range(TOP_K):
            rank_ref[k:k + 1, rs] = jnp.sum(jnp.where(hots[k], prefix, 0.0), axis=0, keepdims=True).astype(I32)
        carry_ref[...] = carry_ref[...] + jnp.sum(multi_f, axis=1, keepdims=True)
    cnt_ref[...] = carry_ref[...]


def _mix_ln1(m2, x2, ln_in_g, ln_in_b, w_o_b, ln1_g, ln1_b, rw_hi, rw_lo, rb, tm):
    n, d = x2.shape
    n_exp = rw_hi.shape[0]
    rows = min(MIX_HALF, tm)
    tri = (jnp.arange(rows, dtype=I32)[:, None] < jnp.arange(rows, dtype=I32)[None, :]).astype(BF16)
    row = lambda i: (i, 0)
    fixed = lambda i: (0, 0)
    col = lambda i: (0, i)
    return pl.pallas_call(
        functools.partial(_mix_kernel, rows=rows),
        grid=(n // tm,),
        in_specs=[pl.BlockSpec((tm, d), row), pl.BlockSpec((tm, d), row),
                  pl.BlockSpec((1, d), fixed), pl.BlockSpec((1, d), fixed),
                  pl.BlockSpec((d, d), fixed),
                  pl.BlockSpec((1, d), fixed), pl.BlockSpec((1, d), fixed),
                  pl.BlockSpec((n_exp, d), fixed), pl.BlockSpec((n_exp, d), fixed),
                  pl.BlockSpec((n_exp, 1), fixed),
                  pl.BlockSpec((rows, rows), fixed)],
        out_specs=[pl.BlockSpec((tm, d), row), pl.BlockSpec((tm, d // 2), row),
                   pl.BlockSpec((TOP_K, tm), col), pl.BlockSpec((TOP_K, tm), col), pl.BlockSpec((TOP_K, tm), col),
                   pl.BlockSpec((n_exp, 128), fixed)],
        out_shape=[jax.ShapeDtypeStruct((n, d), F32),
                   jax.ShapeDtypeStruct((n, d // 2), U32),
                   jax.ShapeDtypeStruct((TOP_K, n), I32),
                   jax.ShapeDtypeStruct((TOP_K, n), F32),
                   jax.ShapeDtypeStruct((TOP_K, n), I32),
                   jax.ShapeDtypeStruct((n_exp, 128), F32)],
        scratch_shapes=[pltpu.VMEM((n_exp, 128), F32), pltpu.VMEM((tm, d), F32)],
        compiler_params=_params(1),
        name="mix_ln1",
    )(m2, x2, ln_in_g, ln_in_b, w_o_b, ln1_g, ln1_b, rw_hi, rw_lo, rb, tri)


def _row_copy(src_ref, src_row, dst_ref, dst_row, sem):
    return pltpu.make_async_copy(src_ref.at[pl.ds(src_row, 1)], dst_ref.at[pl.ds(dst_row, 1)], sem)


def _subtile(q):
    return pl.ds(pl.multiple_of(q * SLOT_ROWS, SLOT_ROWS), SLOT_ROWS)


def _dispatch_kernel(pos_ref, gl_ref, gn_ref, used_ref, h_ref, xs_ref, zero_ref, sem, zsem, *, tt):
    base = pl.program_id(0) * (TOP_K * tt)

    @pl.when(pl.program_id(0) == 0)
    def _():
        zero_ref[...] = jnp.zeros(zero_ref.shape, zero_ref.dtype)
        n_exp = gl_ref.shape[0]
        n_sub = xs_ref.shape[0] // SLOT_ROWS

        def zero_copy(q):
            return pltpu.make_async_copy(zero_ref, xs_ref.at[_subtile(q)], zsem)

        def group_tail(e, start):
            @pl.when(gn_ref[e] > 0)
            def _():
                c = zero_copy(gl_ref[e])
                c.start() if start else c.wait()

        def unused(q, start):
            c = zero_copy(q)
            c.start() if start else c.wait()

        for start in (True, False):
            lax.fori_loop(0, n_exp, lambda e, c: (group_tail(e, start), c)[1], 0)
            lax.fori_loop(used_ref[1], n_sub, lambda q, c: (unused(q, start), c)[1], 0)

    def issue(t, c):
        for k in range(TOP_K):
            _row_copy(h_ref, t, xs_ref, pos_ref[base + k * tt + t], sem).start(priority=k % 2)
        return c

    lax.fori_loop(0, tt, issue, 0)
    for k in range(TOP_K):
        pltpu.make_async_copy(h_ref, xs_ref.at[pl.ds(0, tt)], sem).wait()


def _dispatch(h1p, pos_tiles, group_last, group_nsub, used, n_slots, tt):
    n, dw = h1p.shape
    grid_spec = pltpu.PrefetchScalarGridSpec(
        num_scalar_prefetch=4,
        grid=(n // tt,),
        in_specs=[pl.BlockSpec((tt, dw), lambda i, *_: (i, 0))],
        out_specs=pl.BlockSpec(memory_space=pl.ANY),
        scratch_shapes=[pltpu.VMEM((SLOT_ROWS, dw), h1p.dtype),
                        pltpu.SemaphoreType.DMA(()), pltpu.SemaphoreType.DMA(())],
    )
    return pl.pallas_call(
        functools.partial(_dispatch_kernel, tt=tt),
        grid_spec=grid_spec,
        out_shape=jax.ShapeDtypeStruct((n_slots, dw), h1p.dtype),
        compiler_params=_params(1),
        name="dispatch",
    )(pos_tiles, group_last, group_nsub, used, h1p)


def _expert_kernel(ge_ref, gs_ref, gn_ref, nu_ref, xs_ref, wg_ref, wl_ref, wd_ref, bg_ref, bl_ref, bd_ref,
                   ys_ref, xb_ref, acc_ref, stage_ref, wgb_ref, wlb_ref, wdb_ref, sem_in, sem_out, *, nj):
    t = pl.program_id(0)
    n_groups = nu_ref[0]
    n_items = n_groups * nj
    c = jnp.maximum(t - 1, 0)
    s = c // nj
    j = c - s * nj
    cast_slot = t % 2
    use_slot = 1 - cast_slot
    d, tn = wgb_ref.shape[1:]
    half = d // 2
    rows = _subtile

    def cast_slice(i):
        r = pl.ds(pl.multiple_of(i * (d // CAST_SLICES), d // CAST_SLICES), d // CAST_SLICES)
        wgb_ref[cast_slot, r, :] = wg_ref[0, r, :].astype(BF16)
        wlb_ref[cast_slot, r, :] = wl_ref[0, r, :].astype(BF16)
        r = pl.ds(pl.multiple_of(i * (tn // CAST_SLICES), tn // CAST_SLICES), tn // CAST_SLICES)
        wdb_ref[cast_slot, r, :] = wd_ref[0, r, :].astype(BF16)

    def cast_range(lo, hi):
        lax.fori_loop(lo, hi, lambda i, z: (cast_slice(i), z)[1], 0)

    def out_copy(first_sub, q):
        return pltpu.make_async_copy(acc_ref.at[rows(q)], ys_ref.at[rows(first_sub + q)], sem_out.at[q])

    def in_copy(first_sub, q, slot):
        return pltpu.make_async_copy(xs_ref.at[rows(first_sub + q)], stage_ref.at[slot], sem_in.at[slot])

    def prefetch(first_sub, nsub):
        lax.fori_loop(0, jnp.minimum(nsub, stage_ref.shape[0]),
                      lambda q, z: (in_copy(first_sub, q, q).start(), z)[1], 0)

    @pl.when(t == 0)
    def _():
        cast_range(0, CAST_SLICES)

    @pl.when((t >= 1) & (t <= n_items))
    def _():
        first_sub = gs_ref[s]
        nsub = gn_ref[s]

        @pl.when(j == 0)
        def _():
            sp = jnp.maximum(s - 1, 0)
            prev_first = gs_ref[sp]
            prev_nsub = jnp.where(s > 0, gn_ref[sp], 0)
            depth = stage_ref.shape[0]

            @pl.when(s == 0)
            def _():
                prefetch(first_sub, nsub)

            def load(q, z):
                slot = q % depth
                in_copy(first_sub, q, slot).wait()
                lo, hi = _unpack_pairs(stage_ref[slot])
                xb_ref[rows(q), 0:half] = lo
                xb_ref[rows(q), half:] = hi

                @pl.when(q + depth < nsub)
                def _():
                    in_copy(first_sub, q + depth, slot).start()

                @pl.when(q < prev_nsub)
                def _():
                    out_copy(prev_first, q).wait()

                acc_ref[rows(q), :] = jnp.broadcast_to(bd_ref[0], (SLOT_ROWS, d))
                return z

            lax.fori_loop(0, nsub, load, 0)
            lax.fori_loop(nsub, prev_nsub, lambda q, z: (out_copy(prev_first, q).wait(), z)[1], 0)

        def run(q, n):
            r = pl.ds(pl.multiple_of(q * SLOT_ROWS, SLOT_ROWS), n * SLOT_ROWS)
            x = xb_ref[r, :]
            gate = _dot(x, wgb_ref[use_slot]) + bg_ref[0]
            lin = _dot(x, wlb_ref[use_slot]) + bl_ref[0]
            gate = jnp.minimum(gate, SWIGLU_LIMIT)
            lin = jnp.clip(lin, -SWIGLU_LIMIT, SWIGLU_LIMIT)
            act = (lin + 1.0) * gate * jax.nn.sigmoid(SWIGLU_ALPHA * gate)
            acc_ref[r, :] += _dot(act.astype(BF16), wdb_ref[use_slot])

            @pl.when(j == nj - 1)
            def _():
                for u in range(n):
                    out_copy(first_sub, q + u).start()

        a = nsub // 3
        r = nsub - 3 * a
        two_quads = (r == 2) & (a >= 2)
        first_n = jnp.where(nsub == 1, 1, jnp.where(r == 0, 3, jnp.where((r == 1) | two_quads, 4, 2)))
        for n in (1, 2, 3, 4):
            @pl.when(first_n == n)
            def _():
                run(0, n)
                for i in range(CAST_SLICES):
                    cast_slice(i)

        @pl.when(two_quads)
        def _():
            run(4, 4)

        done = first_n + jnp.where(two_quads, 4, 0)
        lax.fori_loop(0, (nsub - done) // 3, lambda p, z: (run(done + 3 * p, 3), z)[1], 0)

        @pl.when((j == nj - 1) & (s + 1 < n_groups))
        def _():
            nxt = jnp.minimum(s + 1, n_groups - 1)
            prefetch(gs_ref[nxt], gn_ref[nxt])

        @pl.when(t == n_items)
        def _():
            lax.fori_loop(0, nsub, lambda q, z: (out_copy(first_sub, q).wait(), z)[1], 0)
            acc_ref[rows(0), :] = jnp.zeros((SLOT_ROWS, d), F32)
            n_sub = ys_ref.shape[0] // SLOT_ROWS

            def tail_copy(q):
                return pltpu.make_async_copy(acc_ref.at[rows(0)], ys_ref.at[rows(q)], sem_out.at[0])

            lax.fori_loop(nu_ref[1], n_sub, lambda q, z: (tail_copy(q).start(), z)[1], 0)
            lax.fori_loop(nu_ref[1], n_sub, lambda q, z: (tail_copy(q).wait(), z)[1], 0)


def _experts(xs, group_expert, group_first, group_nsub, used, w_gate_up, b_gate_up, w_down, b_down, tn):
    n_slots, dw = xs.shape
    d = 2 * dw
    n_exp, _, de2 = w_gate_up.shape
    de = de2 // 2
    nj = de // tn
    max_items = group_expert.shape[0] * nj
    rmax = GROUP_SUBTILES * SLOT_ROWS

    def fetched(t, nu):
        w = jnp.minimum(t, nu[0] * nj - 1)
        return w // nj, w % nj

    def computed(t, nu):
        w = jnp.clip(t - 1, 0, nu[0] * nj - 1)
        return w // nj, w % nj

    def wg_map(t, ge, gs, gn, nu):
        g, j = fetched(t, nu)
        return ge[g], 0, j

    def wl_map(t, ge, gs, gn, nu):
        g, j = fetched(t, nu)
        return ge[g], 0, nj + j

    def wd_map(t, ge, gs, gn, nu):
        g, j = fetched(t, nu)
        return ge[g], j, 0

    def bg_map(t, ge, gs, gn, nu):
        g, j = computed(t, nu)
        return ge[g], 0, j

    def bl_map(t, ge, gs, gn, nu):
        g, j = computed(t, nu)
        return ge[g], 0, nj + j

    def bd_map(t, ge, gs, gn, nu):
        g, _ = computed(t, nu)
        return ge[g], 0, 0

    grid_spec = pltpu.PrefetchScalarGridSpec(
        num_scalar_prefetch=4,
        grid=(used[0] * nj + 1,),
        in_specs=[pl.BlockSpec(memory_space=pl.ANY),
                  pl.BlockSpec((1, d, tn), wg_map), pl.BlockSpec((1, d, tn), wl_map),
                  pl.BlockSpec((1, tn, d), wd_map),
                  pl.BlockSpec((1, 1, tn), bg_map), pl.BlockSpec((1, 1, tn), bl_map),
                  pl.BlockSpec((1, 1, d), bd_map)],
        out_specs=pl.BlockSpec(memory_space=pl.ANY),
        scratch_shapes=[pltpu.VMEM((rmax, d), BF16),
                        pltpu.VMEM((rmax, d), F32),
                        pltpu.VMEM((STAGE_DEPTH, SLOT_ROWS, dw), U32),
                        pltpu.VMEM((2, d, tn), BF16), pltpu.VMEM((2, d, tn), BF16), pltpu.VMEM((2, tn, d), BF16),
                        pltpu.SemaphoreType.DMA((STAGE_DEPTH,)),
                        pltpu.SemaphoreType.DMA((GROUP_SUBTILES,))],
    )
    return pl.pallas_call(
        functools.partial(_expert_kernel, nj=nj),
        grid_spec=grid_spec,
        out_shape=jax.ShapeDtypeStruct((n_slots, d), F32),
        compiler_params=_params(1),
        name="experts",
    )(group_expert, group_first, group_nsub, used, xs, w_gate_up, w_gate_up, w_down,
      b_gate_up.reshape(n_exp, 1, de2), b_gate_up.reshape(n_exp, 1, de2), b_down.reshape(n_exp, 1, d))


def _combine_kernel(pos_ref, h_ref, w_ref, g_ref, b_ref, ys_ref, o_ref, buf_ref, sem, *, tt, n_steps):
    i = pl.program_id(0)
    slot = i % 2
    ch = min(COMBINE_CHUNK, tt)

    def issue_rows(step, dst_slot, t0):
        base = step * (TOP_K * tt)
        for r in range(ch):
            for k in range(TOP_K):
                _row_copy(ys_ref, pos_ref[base + k * tt + t0 + r], buf_ref.at[dst_slot, k], t0 + r,
                          sem.at[dst_slot]).start(priority=k % 2)

    def wait_rows(src_slot):
        for k in range(TOP_K):
            pltpu.make_async_copy(ys_ref.at[pl.ds(0, tt)], buf_ref.at[src_slot, k], sem.at[src_slot]).wait()

    @pl.when(i == 0)
    def _():
        lax.fori_loop(0, tt // ch, lambda c, z: (issue_rows(0, 0, c * ch), z)[1], 0)

    wait_rows(slot)

    @pl.when(i + 1 < n_steps)
    def _():
        lax.fori_loop(0, tt // ch, lambda c, z: (issue_rows(i + 1, 1 - slot, c * ch), z)[1], 0)

    w = w_ref[...]
    y = DEEPNORM_ALPHA * h_ref[...]
    for k in range(TOP_K):
        y = y + w[:, k:k + 1] * buf_ref[slot, k]
    o_ref[...] = _layer_norm(y, g_ref[...], b_ref[...])


def _combine(h1, pos_tiles, wts_t, ys, ln2_g, ln2_b, tt):
    n, d = h1.shape
    n_steps = n // tt
    grid_spec = pltpu.PrefetchScalarGridSpec(
        num_scalar_prefetch=1,
        grid=(n_steps,),
        in_specs=[pl.BlockSpec((tt, d), lambda i, pos: (i, 0)),
                  pl.BlockSpec((tt, TOP_K), lambda i, pos: (i, 0)),
                  pl.BlockSpec((1, d), lambda i, pos: (0, 0)),
                  pl.BlockSpec((1, d), lambda i, pos: (0, 0)),
                  pl.BlockSpec(memory_space=pl.ANY)],
        out_specs=pl.BlockSpec((tt, d), lambda i, pos: (i, 0)),
        scratch_shapes=[pltpu.VMEM((2, TOP_K, tt, d), F32), pltpu.SemaphoreType.DMA((2,))],
    )
    return pl.pallas_call(
        functools.partial(_combine_kernel, tt=tt, n_steps=n_steps),
        grid_spec=grid_spec,
        out_shape=jax.ShapeDtypeStruct((n, d), F32),
        compiler_params=_params(1),
        name="combine",
    )(pos_tiles, h1, wts_t, ln2_g, ln2_b, ys)


def _routing_tables(counts, eid, rank, n_tok):
    n_exp = counts.shape[0]
    e_ids = jnp.arange(n_exp, dtype=I32)
    nsub = (counts + SLOT_ROWS - 1) // SLOT_ROWS
    ngrp = (nsub + GROUP_SUBTILES - 1) // GROUP_SUBTILES
    base = nsub // jnp.maximum(ngrp, 1)
    rem = nsub - base * ngrp
    first_sub = jnp.cumsum(nsub) - nsub
    pos = jnp.sum(jnp.where(eid[..., None] == e_ids, first_sub * SLOT_ROWS, 0), axis=-1) + rank

    n_subtiles = (n_tok * TOP_K) // SLOT_ROWS + n_exp
    max_groups = (n_subtiles + GROUP_SUBTILES - 1) // GROUP_SUBTILES + n_exp
    grp_end = jnp.cumsum(ngrp)
    g_ids = jnp.arange(max_groups, dtype=I32)
    g_exp = jnp.minimum(jnp.sum((grp_end[None, :] <= g_ids[:, None]).astype(I32), axis=1), n_exp - 1)
    pick = lambda tab: jnp.sum(jnp.where(g_exp[:, None] == e_ids[None, :], tab[None, :], 0), axis=1)
    local = g_ids - pick(grp_end - ngrp)
    g_base, g_rem = pick(base), pick(rem)
    g_nsub = g_base + (local < g_rem).astype(I32)
    g_first = pick(first_sub) + local * g_base + jnp.minimum(local, g_rem)
    used = jnp.stack([grp_end[-1], jnp.sum(nsub)]).astype(I32)
    return dict(pos=pos, n_slots=n_subtiles * SLOT_ROWS, used=used,
                group_expert=g_exp.astype(I32), group_first=g_first.astype(I32), group_nsub=g_nsub.astype(I32),
                expert_last=(first_sub + nsub - 1).astype(I32), expert_nsub=nsub.astype(I32))


def _tiles(seq, n_tok):
    return dict(
        ln_rows=min(1024, n_tok),
        conv_rows=min(1024, seq), conv_cols=512,
        pool_rows=min(1024, seq),
        merge_rows=min(1024, n_tok), merge_cols=512,
        mix_rows=min(512, n_tok),
        route_rows=min(256, n_tok), dispatch_rows=min(2048, n_tok),
        expert_cols=256,
    )


def kernel(x, meta_tokens, ln_in_g, ln_in_b, w_in, conv_w, w_a_out, pool_w, pool_scale, w_o, ln1_g, ln1_b,
           router_w, router_b, w_gate_up, b_gate_up, w_down, b_down, ln2_g, ln2_b):
    bsz, seq, d = x.shape
    assert w_in.shape[0] == DEPTH and meta_tokens.shape[0] == N_META == HALO
    d_conv = conv_w.shape[-1]
    n_groups, pool_cin, pool_cout = pool_w.shape[1:]
    d_pool = n_groups * pool_cin
    assert n_groups == len(POOL_WINDOWS) and n_groups * pool_cout == d
    n_exp = router_w.shape[-1]
    n_tok = bsz * seq
    t = _tiles(seq, n_tok)
    row = lambda v: v.reshape(1, -1).astype(F32)

    w_in_b = w_in[0].astype(BF16)
    x2 = x.reshape(n_tok, d)
    lig, lib = row(ln_in_g), row(ln_in_b)

    hx2 = _ln_cast(x2, lig, lib, t["ln_rows"])
    hm = _ln_cast(meta_tokens.astype(F32), lig, lib, N_META)
    hx = hx2.reshape(bsz, seq, d)

    a = _conv_branch(hx, hm, w_in_b, conv_w[0], d_conv, t["conv_rows"], min(t["conv_cols"], d_conv))
    yb = _pool_branch(hx, hm, w_in_b, pool_w[0].astype(BF16), row(pool_scale[0]), 3 * d_conv, t["pool_rows"])
    m2 = _merge(hx2, a.reshape(n_tok, d_conv), yb.reshape(n_tok, d), w_in_b, w_a_out[0].astype(BF16),
                3 * d_conv + d_pool, t["merge_rows"], min(t["merge_cols"], d))

    rw_t = router_w[0].T
    rw_hi = rw_t.astype(BF16)
    rw_lo = (rw_t - rw_hi.astype(F32)).astype(BF16)
    h1, h1p, eid, wts, rank, cnt = _mix_ln1(m2, x2, lig, lib, w_o[0].astype(BF16), row(ln1_g[0]), row(ln1_b[0]),
                                            rw_hi, rw_lo, router_b[0].reshape(n_exp, 1), t["mix_rows"])

    rt = _routing_tables(cnt[:, 0].astype(I32), eid, rank, n_tok)
    def pos_tiles_for(rows):
        return rt["pos"].reshape(TOP_K, n_tok // rows, rows).transpose(1, 0, 2).reshape(-1)

    tt = t["route_rows"]
    pos_tiles = pos_tiles_for(tt)
    xs = _dispatch(h1p, pos_tiles_for(t["dispatch_rows"]), rt["expert_last"], rt["expert_nsub"], rt["used"],
                   rt["n_slots"], t["dispatch_rows"])
    ys = _experts(xs, rt["group_expert"], rt["group_first"], rt["group_nsub"], rt["used"],
                  w_gate_up[0], b_gate_up[0], w_down[0], b_down[0], min(t["expert_cols"], w_down.shape[-2]))
    out = _combine(h1, pos_tiles, wts.T, ys, row(ln2_g[0]), row(ln2_b[0]), tt)
    return out.reshape(bsz, seq, d)
```

```python
import functools

import jax
import jax.numpy as jnp
from jax import lax
from jax.experimental import pallas as pl
from jax.experimental.pallas import tpu as pltpu

F32 = jnp.float32
BF16 = jnp.bfloat16
I32 = jnp.int32
U32 = jnp.uint32

N_META = 16
CONV_K = 3
POOL_WINDOWS = (2, 4, 8, 16)
TOP_K = 4
SWIGLU_LIMIT = 7.0
SWIGLU_ALPHA = 1.702
LN_EPS = 1e-5
DEPTH = 1
DEEPNORM_ALPHA = (2.0 * DEPTH) ** 0.25

HALO = 16
POOL_PAD = 128
POOL_CHUNK = 256
SLOT_ROWS = 256
GROUP_SUBTILES = 9
CAST_SLICES = 8
STAGE_DEPTH = 8
MIX_HALF = 256
ROUTER_LANES = 128
COMBINE_CHUNK = 16
V7X_VMEM_LIMIT = 58 * 1024 * 1024

_ARB = "arbitrary"


def _params(n_axes):
    return pltpu.CompilerParams(dimension_semantics=(_ARB,) * n_axes, vmem_limit_bytes=V7X_VMEM_LIMIT)


def _layer_norm(x, g, b):
    mu = jnp.mean(x, axis=-1, keepdims=True)
    xc = x - mu
    var = jnp.mean(xc * xc, axis=-1, keepdims=True)
    return xc * lax.rsqrt(var + LN_EPS) * g + b


def _dot(a, b):
    return jnp.dot(a, b, preferred_element_type=F32)


def _pack_pairs(lo_bf16, hi_bf16):
    lo = lax.bitcast_convert_type(lo_bf16.astype(F32), U32) >> 16
    hi = lax.bitcast_convert_type(hi_bf16.astype(F32), U32) & jnp.uint32(0xFFFF0000)
    return hi | lo


def _unpack_pairs(words):
    lo = lax.bitcast_convert_type(words << 16, F32).astype(BF16)
    hi = lax.bitcast_convert_type(words & jnp.uint32(0xFFFF0000), F32).astype(BF16)
    return lo, hi


def _ln_cast_kernel(x_ref, g_ref, b_ref, o_ref):
    o_ref[...] = _layer_norm(x_ref[...], g_ref[...], b_ref[...]).astype(o_ref.dtype)


def _ln_cast(x2d, g, b, tr):
    n, d = x2d.shape
    return pl.pallas_call(
        _ln_cast_kernel,
        grid=(n // tr,),
        in_specs=[pl.BlockSpec((tr, d), lambda i: (i, 0)),
                  pl.BlockSpec((1, d), lambda i: (0, 0)),
                  pl.BlockSpec((1, d), lambda i: (0, 0))],
        out_specs=pl.BlockSpec((tr, d), lambda i: (i, 0)),
        out_shape=jax.ShapeDtypeStruct((n, d), BF16),
        compiler_params=_params(1),
        name="ln_cast",
    )(x2d, g, b)


def _conv_kernel(h_ref, halo_ref, hm_ref, wb_ref, wc_ref, wu_ref, cw_ref, o_ref, he_ref, cu_ref, *, tm):
    i = pl.program_id(1)
    j = pl.program_id(2)

    @pl.when(j == 0)
    def _():
        he_ref[HALO:, :] = h_ref[0]

    @pl.when((j == 0) & (i == 0))
    def _():
        he_ref[0:HALO, :] = hm_ref[...]

    @pl.when((j == 0) & (i > 0))
    def _():
        he_ref[0:HALO, :] = halo_ref[0]

    he = he_ref[...]
    cu_ref[...] = _dot(he, wc_ref[...]) * _dot(he, wu_ref[...])
    bg = _dot(he_ref[HALO:, :], wb_ref[...])
    cw = cw_ref[...]
    y = cw[CONV_K - 1:CONV_K, :] * cu_ref[HALO:HALO + tm, :]
    for k in range(CONV_K - 1):
        s = HALO - (CONV_K - 1) + k
        y = y + cw[k:k + 1, :] * cu_ref[s:s + tm, :]
    o_ref[0] = (bg * y).astype(o_ref.dtype)


def _conv_branch(hx, hm, w_in_b, conv_w, d_conv, tm, tc):
    bsz, seq, d = hx.shape
    nc = d_conv // tc
    hb = tm // HALO
    return pl.pallas_call(
        functools.partial(_conv_kernel, tm=tm),
        grid=(bsz, seq // tm, nc),
        in_specs=[pl.BlockSpec((1, tm, d), lambda b, i, j: (b, i, 0)),
                  pl.BlockSpec((1, HALO, d), lambda b, i, j: (b, jnp.maximum(i * hb - 1, 0), 0)),
                  pl.BlockSpec((HALO, d), lambda b, i, j: (0, 0)),
                  pl.BlockSpec((d, tc), lambda b, i, j: (0, j)),
                  pl.BlockSpec((d, tc), lambda b, i, j: (0, nc + j)),
                  pl.BlockSpec((d, tc), lambda b, i, j: (0, 2 * nc + j)),
                  pl.BlockSpec((CONV_K, tc), lambda b, i, j: (0, j))],
        out_specs=pl.BlockSpec((1, tm, tc), lambda b, i, j: (b, i, j)),
        out_shape=jax.ShapeDtypeStruct((bsz, seq, d_conv), BF16),
        scratch_shapes=[pltpu.VMEM((tm + HALO, d), BF16), pltpu.VMEM((tm + HALO, tc), F32)],
        compiler_params=_params(3),
        name="conv_branch",
    )(hx, hx, hm, w_in_b, w_in_b, w_in_b, conv_w)


def _pool_kernel(h_ref, halo_ref, hm_ref, wv_ref, band_ref, pw_ref, ps_ref, o_ref, he_ref, vh_ref, vl_ref, *, tm):
    i = pl.program_id(1)
    lead = POOL_PAD - HALO
    he_ref[0:lead, :] = jnp.zeros((lead, he_ref.shape[1]), he_ref.dtype)
    he_ref[POOL_PAD:, :] = h_ref[0]

    @pl.when(i == 0)
    def _():
        he_ref[lead:POOL_PAD, :] = hm_ref[...]

    @pl.when(i > 0)
    def _():
        he_ref[lead:POOL_PAD, :] = halo_ref[0]

    v = _dot(he_ref[...], wv_ref[...])
    v_hi = v.astype(BF16)
    vh_ref[...] = v_hi
    vl_ref[...] = (v - v_hi.astype(F32)).astype(BF16)
    ng, cin, cout = pw_ref.shape
    rows = min(POOL_CHUNK, tm)
    for c in range(tm // rows):
        r0 = c * rows
        for g in range(ng):
            band = band_ref[g]
            win = (slice(r0, r0 + rows + POOL_PAD), slice(g * cin, (g + 1) * cin))
            pooled = _dot(band, vh_ref[win]) + _dot(band, vl_ref[win])
            y = _dot(pooled.astype(BF16), pw_ref[g]) * ps_ref[:, g * cout:(g + 1) * cout]
            o_ref[0, r0:r0 + rows, g * cout:(g + 1) * cout] = y.astype(o_ref.dtype)


def _pool_band(rows):
    t = jnp.arange(rows, dtype=I32)[:, None] + POOL_PAD
    s = jnp.arange(rows + POOL_PAD, dtype=I32)[None, :]
    bands = []
    for w in POOL_WINDOWS:
        inside = ((s <= t) & (s > t - w)).astype(F32) / w
        bands.append(inside - (s == t).astype(F32))
    return jnp.stack(bands).astype(BF16)


def _pool_branch(hx, hm, w_in_b, pool_w_b, pool_scale, col0, tm):
    bsz, seq, d = hx.shape
    ng, cin, cout = pool_w_b.shape
    d_pool = ng * cin
    hb = tm // HALO
    band = _pool_band(min(POOL_CHUNK, tm))
    return pl.pallas_call(
        functools.partial(_pool_kernel, tm=tm),
        grid=(bsz, seq // tm),
        in_specs=[pl.BlockSpec((1, tm, d), lambda b, i: (b, i, 0)),
                  pl.BlockSpec((1, HALO, d), lambda b, i: (b, jnp.maximum(i * hb - 1, 0), 0)),
                  pl.BlockSpec((HALO, d), lambda b, i: (0, 0)),
                  pl.BlockSpec((d, d_pool), lambda b, i: (0, col0 // d_pool)),
                  pl.BlockSpec(band.shape, lambda b, i: (0, 0, 0)),
                  pl.BlockSpec((ng, cin, cout), lambda b, i: (0, 0, 0)),
                  pl.BlockSpec((1, ng * cout), lambda b, i: (0, 0))],
        out_specs=pl.BlockSpec((1, tm, ng * cout), lambda b, i: (b, i, 0)),
        out_shape=jax.ShapeDtypeStruct((bsz, seq, ng * cout), BF16),
        scratch_shapes=[pltpu.VMEM((tm + POOL_PAD, d), BF16),
                        pltpu.VMEM((tm + POOL_PAD, d_pool), BF16),
                        pltpu.VMEM((tm + POOL_PAD, d_pool), BF16)],
        compiler_params=_params(2),
        name="pool_branch",
    )(hx, hx, hm, w_in_b, band, pool_w_b, pool_scale)


def _merge_kernel(h_ref, a_ref, yb_ref, wga_ref, wgb_ref, wao_ref, o_ref):
    h = h_ref[...]
    ga = jax.nn.sigmoid(_dot(h, wga_ref[...]))
    gb = jax.nn.sigmoid(_dot(h, wgb_ref[...]))
    ya = _dot(a_ref[...], wao_ref[...])
    o_ref[...] = (ga * ya + gb * yb_ref[...].astype(F32)).astype(o_ref.dtype)


def _merge(hx2, a2, yb2, w_in_b, w_a_out_b, col0, tm, tj):
    n, d = hx2.shape
    dc = a2.shape[1]
    nj = d // tj
    return pl.pallas_call(
        _merge_kernel,
        grid=(n // tm, nj),
        in_specs=[pl.BlockSpec((tm, d), lambda i, j: (i, 0)),
                  pl.BlockSpec((tm, dc), lambda i, j: (i, 0)),
                  pl.BlockSpec((tm, tj), lambda i, j: (i, j)),
                  pl.BlockSpec((d, tj), lambda i, j: (0, col0 // tj + j)),
                  pl.BlockSpec((d, tj), lambda i, j: (0, col0 // tj + nj + j)),
                  pl.BlockSpec((dc, tj), lambda i, j: (0, j))],
        out_specs=pl.BlockSpec((tm, tj), lambda i, j: (i, j)),
        out_shape=jax.ShapeDtypeStruct((n, d), BF16),
        compiler_params=_params(2),
        name="merge",
    )(hx2, a2, yb2, w_in_b, w_in_b, w_a_out_b)


def _mix_kernel(m_ref, x_ref, lig_ref, lib_ref, wo_ref, g1_ref, b1_ref, rwh_ref, rwl_ref, rb_ref, tri_ref,
                h1_ref, h1p_ref, eid_ref, wts_ref, rank_ref, cnt_ref, carry_ref, z_ref, hh_ref, hl_ref, *, rows):
    step = pl.program_id(0)

    @pl.when(step == 0)
    def _():
        carry_ref[...] = jnp.zeros(carry_ref.shape, F32)

    for r0 in range(0, m_ref.shape[0], rows):
        rs = slice(r0, r0 + rows)
        h0 = _layer_norm(x_ref[rs, :], lig_ref[...], lib_ref[...])
        z_ref[rs, :] = DEEPNORM_ALPHA * h0 + _dot(m_ref[rs, :], wo_ref[...])

    for r0 in range(0, m_ref.shape[0], rows):
        rs = slice(r0, r0 + rows)
        h1 = _layer_norm(z_ref[rs, :], g1_ref[...], b1_ref[...])
        h1_ref[rs, :] = h1
        h_hi = h1.astype(BF16)
        half = h1.shape[1] // 2
        h1p_ref[rs, :] = _pack_pairs(h_hi[:, :half], h_hi[:, half:])
        hh_ref[rs, :] = h_hi
        hl_ref[rs, :] = (h1 - h_hi.astype(F32)).astype(BF16)

    n_exp = rb_ref.shape[0]
    rwh = rwh_ref[...]
    logits_t = _dot(hh_ref[...], rwh) + _dot(hl_ref[...], rwh) + _dot(hh_ref[...], rwl_ref[...])
    logits = logits_t.T[0:n_exp, :] + rb_ref[...]
    e_iota = lax.broadcasted_iota(I32, logits.shape, 0)
    work = logits
    vals, hots = [], []
    for k in range(TOP_K):
        m = jnp.max(work, axis=0, keepdims=True)
        idx = jnp.min(jnp.where(work == m, e_iota, n_exp), axis=0, keepdims=True)
        hot = e_iota == idx
        vals.append(m)
        hots.append(hot)
        eid_ref[k:k + 1, :] = idx
        work = jnp.where(hot, -jnp.inf, work)
    exps = [jnp.exp(v - vals[0]) for v in vals]
    denom = exps[0]
    for e in exps[1:]:
        denom = denom + e
    for k in range(TOP_K):
        wts_ref[k:k + 1, :] = exps[k] / denom

    multi = hots[0]
    for hot in hots[1:]:
        multi = multi | hot
    multi_f = jnp.where(multi, 1.0, 0.0).astype(F32)
    prefix = _dot(multi_f.astype(BF16), tri_ref[...]) + carry_ref[:, 0:1]
    for k in range(TOP_K):
        rank_ref[k:k + 1, :] = jnp.sum(jnp.where(hots[k], prefix, 0.0), axis=0, keepdims=True).astype(I32)
    carry_ref[...] = carry_ref[...] + jnp.sum(multi_f, axis=1, keepdims=True)
    cnt_ref[...] = carry_ref[...]


def _mix_ln1(m2, x2, ln_in_g, ln_in_b, w_o_b, ln1_g, ln1_b, rw_hi, rw_lo, rb, tm):
    n, d = x2.shape
    n_exp = rb.shape[0]
    rows = min(MIX_HALF, tm)
    tri = (jnp.arange(tm, dtype=I32)[:, None] < jnp.arange(tm, dtype=I32)[None, :]).astype(BF16)
    row = lambda i: (i, 0)
    fixed = lambda i: (0, 0)
    col = lambda i: (0, i)
    return pl.pallas_call(
        functools.partial(_mix_kernel, rows=rows),
        grid=(n // tm,),
        in_specs=[pl.BlockSpec((tm, d), row), pl.BlockSpec((tm, d), row),
                  pl.BlockSpec((1, d), fixed), pl.BlockSpec((1, d), fixed),
                  pl.BlockSpec((d, d), fixed),
                  pl.BlockSpec((1, d), fixed), pl.BlockSpec((1, d), fixed),
                  pl.BlockSpec(rw_hi.shape, fixed), pl.BlockSpec(rw_lo.shape, fixed),
                  pl.BlockSpec((n_exp, 1), fixed),
                  pl.BlockSpec((tm, tm), fixed)],
        out_specs=[pl.BlockSpec((tm, d), row), pl.BlockSpec((tm, d // 2), row),
                   pl.BlockSpec((TOP_K, tm), col), pl.BlockSpec((TOP_K, tm), col), pl.BlockSpec((TOP_K, tm), col),
                   pl.BlockSpec((n_exp, 128), fixed)],
        out_shape=[jax.ShapeDtypeStruct((n, d), F32),
                   jax.ShapeDtypeStruct((n, d // 2), U32),
                   jax.ShapeDtypeStruct((TOP_K, n), I32),
                   jax.ShapeDtypeStruct((TOP_K, n), F32),
                   jax.ShapeDtypeStruct((TOP_K, n), I32),
                   jax.ShapeDtypeStruct((n_exp, 128), F32)],
        scratch_shapes=[pltpu.VMEM((n_exp, 128), F32), pltpu.VMEM((tm, d), F32),
                        pltpu.VMEM((tm, d), BF16), pltpu.VMEM((tm, d), BF16)],
        compiler_params=_params(1),
        name="mix_ln1",
    )(m2, x2, ln_in_g, ln_in_b, w_o_b, ln1_g, ln1_b, rw_hi, rw_lo, rb, tri)


def _row_copy(src_ref, src_row, dst_ref, dst_row, sem):
    return pltpu.make_async_copy(src_ref.at[pl.ds(src_row, 1)], dst_ref.at[pl.ds(dst_row, 1)], sem)


def _subtile(q):
    return pl.ds(pl.multiple_of(q * SLOT_ROWS, SLOT_ROWS), SLOT_ROWS)


def _dispatch_kernel(pos_ref, gl_ref, gn_ref, used_ref, h_ref, xs_ref, zero_ref, sem, zsem, *, tt):
    base = pl.program_id(0) * (TOP_K * tt)

    @pl.when(pl.program_id(0) == 0)
    def _():
        zero_ref[...] = jnp.zeros(zero_ref.shape, zero_ref.dtype)
        n_exp = gl_ref.shape[0]
        n_sub = xs_ref.shape[0] // SLOT_ROWS

        def zero_copy(q):
            return pltpu.make_async_copy(zero_ref, xs_ref.at[_subtile(q)], zsem)

        def group_tail(e, start):
            @pl.when(gn_ref[e] > 0)
            def _():
                c = zero_copy(gl_ref[e])
                c.start() if start else c.wait()

        def unused(q, start):
            c = zero_copy(q)
            c.start() if start else c.wait()

        for start in (True, False):
            lax.fori_loop(0, n_exp, lambda e, c: (group_tail(e, start), c)[1], 0)
            lax.fori_loop(used_ref[1], n_sub, lambda q, c: (unused(q, start), c)[1], 0)

    def issue(t, c):
        for k in range(TOP_K):
            _row_copy(h_ref, t, xs_ref, pos_ref[base + k * tt + t], sem).start(priority=k % 2)
        return c

    lax.fori_loop(0, tt, issue, 0)
    for k in range(TOP_K):
        pltpu.make_async_copy(h_ref, xs_ref.at[pl.ds(0, tt)], sem).wait()


def _dispatch(h1p, pos_tiles, group_last, group_nsub, used, n_slots, tt):
    n, dw = h1p.shape
    grid_spec = pltpu.PrefetchScalarGridSpec(
        num_scalar_prefetch=4,
        grid=(n // tt,),
        in_specs=[pl.BlockSpec((tt, dw), lambda i, *_: (i, 0))],
        out_specs=pl.BlockSpec(memory_space=pl.ANY),
        scratch_shapes=[pltpu.VMEM((SLOT_ROWS, dw), h1p.dtype),
                        pltpu.SemaphoreType.DMA(()), pltpu.SemaphoreType.DMA(())],
    )
    return pl.pallas_call(
        functools.partial(_dispatch_kernel, tt=tt),
        grid_spec=grid_spec,
        out_shape=jax.ShapeDtypeStruct((n_slots, dw), h1p.dtype),
        compiler_params=_params(1),
        name="dispatch",
    )(pos_tiles, group_last, group_nsub, used, h1p)


def _expert_kernel(ge_ref, gs_ref, gn_ref, nu_ref, xs_ref, wg_ref, wl_ref, wd_ref, bg_ref, bl_ref, bd_ref,
                   ys_ref, xb_ref, acc_ref, stage_ref, wgb_ref, wlb_ref, wdb_ref, sem_in, sem_out, *, nj):
    t = pl.program_id(0)
    n_groups = nu_ref[0]
    n_items = n_groups * nj
    c = jnp.maximum(t - 1, 0)
    s = c // nj
    j = c - s * nj
    cast_slot = t % 2
    use_slot = 1 - cast_slot
    d, tn = wgb_ref.shape[1:]
    half = d // 2
    rows = _subtile

    def cast_slice(i):
        r = pl.ds(pl.multiple_of(i * (d // CAST_SLICES), d // CAST_SLICES), d // CAST_SLICES)
        wgb_ref[cast_slot, r, :] = wg_ref[0, r, :].astype(BF16)
        wlb_ref[cast_slot, r, :] = wl_ref[0, r, :].astype(BF16)
        r = pl.ds(pl.multiple_of(i * (tn // CAST_SLICES), tn // CAST_SLICES), tn // CAST_SLICES)
        wdb_ref[cast_slot, r, :] = wd_ref[0, r, :].astype(BF16)

    def cast_range(lo, hi):
        lax.fori_loop(lo, hi, lambda i, z: (cast_slice(i), z)[1], 0)

    def out_copy(first_sub, q):
        return pltpu.make_async_copy(acc_ref.at[rows(q)], ys_ref.at[rows(first_sub + q)], sem_out.at[q])

    def in_copy(first_sub, q, slot):
        return pltpu.make_async_copy(xs_ref.at[rows(first_sub + q)], stage_ref.at[slot], sem_in.at[slot])

    def prefetch(first_sub, nsub):
        lax.fori_loop(0, jnp.minimum(nsub, stage_ref.shape[0]),
                      lambda q, z: (in_copy(first_sub, q, q).start(), z)[1], 0)

    @pl.when(t == 0)
    def _():
        cast_range(0, CAST_SLICES)

    @pl.when((t >= 1) & (t <= n_items))
    def _():
        first_sub = gs_ref[s]
        nsub = gn_ref[s]

        @pl.when(j == 0)
        def _():
            sp = jnp.maximum(s - 1, 0)
            prev_first = gs_ref[sp]
            prev_nsub = jnp.where(s > 0, gn_ref[sp], 0)
            depth = stage_ref.shape[0]

            @pl.when(s == 0)
            def _():
                prefetch(first_sub, nsub)

            def load(q, z):
                slot = q % depth
                in_copy(first_sub, q, slot).wait()
                lo, hi = _unpack_pairs(stage_ref[slot])
                xb_ref[rows(q), 0:half] = lo
                xb_ref[rows(q), half:] = hi

                @pl.when(q + depth < nsub)
                def _():
                    in_copy(first_sub, q + depth, slot).start()

                @pl.when(q < prev_nsub)
                def _():
                    out_copy(prev_first, q).wait()

                acc_ref[rows(q), :] = jnp.broadcast_to(bd_ref[0], (SLOT_ROWS, d))
                return z

            lax.fori_loop(0, nsub, load, 0)
            lax.fori_loop(nsub, prev_nsub, lambda q, z: (out_copy(prev_first, q).wait(), z)[1], 0)

        def run(q, n):
            r = pl.ds(pl.multiple_of(q * SLOT_ROWS, SLOT_ROWS), n * SLOT_ROWS)
            x = xb_ref[r, :]
            gate = _dot(x, wgb_ref[use_slot]) + bg_ref[0]
            lin = _dot(x, wlb_ref[use_slot]) + bl_ref[0]
            gate = jnp.minimum(gate, SWIGLU_LIMIT)
            lin = jnp.clip(lin, -SWIGLU_LIMIT, SWIGLU_LIMIT)
            act = (lin + 1.0) * gate * jax.nn.sigmoid(SWIGLU_ALPHA * gate)
            acc_ref[r, :] += _dot(act.astype(BF16), wdb_ref[use_slot])

            @pl.when(j == nj - 1)
            def _():
                for u in range(n):
                    out_copy(first_sub, q + u).start()

        a = nsub // 3
        r = nsub - 3 * a
        two_quads = (r == 2) & (a >= 2)
        first_n = jnp.where(nsub == 1, 1, jnp.where(r == 0, 3, jnp.where((r == 1) | two_quads, 4, 2)))
        for n in (1, 2, 3, 4):
            @pl.when(first_n == n)
            def _():
                run(0, n)
                for i in range(CAST_SLICES):
                    cast_slice(i)

        @pl.when(two_quads)
        def _():
            run(4, 4)

        done = first_n + jnp.where(two_quads, 4, 0)
        lax.fori_loop(0, (nsub - done) // 3, lambda p, z: (run(done + 3 * p, 3), z)[1], 0)

        @pl.when((j == nj - 1) & (s + 1 < n_groups))
        def _():
            nxt = jnp.minimum(s + 1, n_groups - 1)
            prefetch(gs_ref[nxt], gn_ref[nxt])

        @pl.when(t == n_items)
        def _():
            lax.fori_loop(0, nsub, lambda q, z: (out_copy(first_sub, q).wait(), z)[1], 0)
            acc_ref[rows(0), :] = jnp.zeros((SLOT_ROWS, d), F32)
            n_sub = ys_ref.shape[0] // SLOT_ROWS

            def tail_copy(q):
                return pltpu.make_async_copy(acc_ref.at[rows(0)], ys_ref.at[rows(q)], sem_out.at[0])

            lax.fori_loop(nu_ref[1], n_sub, lambda q, z: (tail_copy(q).start(), z)[1], 0)
            lax.fori_loop(nu_ref[1], n_sub, lambda q, z: (tail_copy(q).wait(), z)[1], 0)


def _experts(xs, group_expert, group_first, group_nsub, used, w_gate_up, b_gate_up, w_down, b_down, tn):
    n_slots, dw = xs.shape
    d = 2 * dw
    n_exp, _, de2 = w_gate_up.shape
    de = de2 // 2
    nj = de // tn
    max_items = group_expert.shape[0] * nj
    rmax = GROUP_SUBTILES * SLOT_ROWS

    def fetched(t, nu):
        w = jnp.minimum(t, nu[0] * nj - 1)
        return w // nj, w % nj

    def computed(t, nu):
        w = jnp.clip(t - 1, 0, nu[0] * nj - 1)
        return w // nj, w % nj

    def wg_map(t, ge, gs, gn, nu):
        g, j = fetched(t, nu)
        return ge[g], 0, j

    def wl_map(t, ge, gs, gn, nu):
        g, j = fetched(t, nu)
        return ge[g], 0, nj + j

    def wd_map(t, ge, gs, gn, nu):
        g, j = fetched(t, nu)
        return ge[g], j, 0

    def bg_map(t, ge, gs, gn, nu):
        g, j = computed(t, nu)
        return ge[g], 0, j

    def bl_map(t, ge, gs, gn, nu):
        g, j = computed(t, nu)
        return ge[g], 0, nj + j

    def bd_map(t, ge, gs, gn, nu):
        g, _ = computed(t, nu)
        return ge[g], 0, 0

    grid_spec = pltpu.PrefetchScalarGridSpec(
        num_scalar_prefetch=4,
        grid=(used[0] * nj + 1,),
        in_specs=[pl.BlockSpec(memory_space=pl.ANY),
                  pl.BlockSpec((1, d, tn), wg_map), pl.BlockSpec((1, d, tn), wl_map),
                  pl.BlockSpec((1, tn, d), wd_map),
                  pl.BlockSpec((1, 1, tn), bg_map), pl.BlockSpec((1, 1, tn), bl_map),
                  pl.BlockSpec((1, 1, d), bd_map)],
        out_specs=pl.BlockSpec(memory_space=pl.ANY),
        scratch_shapes=[pltpu.VMEM((rmax, d), BF16),
                        pltpu.VMEM((rmax, d), F32),
                        pltpu.VMEM((STAGE_DEPTH, SLOT_ROWS, dw), U32),
                        pltpu.VMEM((2, d, tn), BF16), pltpu.VMEM((2, d, tn), BF16), pltpu.VMEM((2, tn, d), BF16),
                        pltpu.SemaphoreType.DMA((STAGE_DEPTH,)),
                        pltpu.SemaphoreType.DMA((GROUP_SUBTILES,))],
    )
    return pl.pallas_call(
        functools.partial(_expert_kernel, nj=nj),
        grid_spec=grid_spec,
        out_shape=jax.ShapeDtypeStruct((n_slots, d), F32),
        compiler_params=_params(1),
        name="experts",
    )(group_expert, group_first, group_nsub, used, xs, w_gate_up, w_gate_up, w_down,
      b_gate_up.reshape(n_exp, 1, de2), b_gate_up.reshape(n_exp, 1, de2), b_down.reshape(n_exp, 1, d))


def _combine_kernel(pos_ref, h_ref, w_ref, g_ref, b_ref, ys_ref, o_ref, buf_ref, sem, *, tt, n_steps):
    i = pl.program_id(0)
    slot = i % 2
    ch = min(COMBINE_CHUNK, tt)

    def issue_rows(step, dst_slot, t0):
        base = step * (TOP_K * tt)
        for r in range(ch):
            for k in range(TOP_K):
                _row_copy(ys_ref, pos_ref[base + k * tt + t0 + r], buf_ref.at[dst_slot, k], t0 + r,
                          sem.at[dst_slot]).start(priority=k % 2)

    def wait_rows(src_slot):
        for k in range(TOP_K):
            pltpu.make_async_copy(ys_ref.at[pl.ds(0, tt)], buf_ref.at[src_slot, k], sem.at[src_slot]).wait()

    @pl.when(i == 0)
    def _():
        lax.fori_loop(0, tt // ch, lambda c, z: (issue_rows(0, 0, c * ch), z)[1], 0)

    wait_rows(slot)

    @pl.when(i + 1 < n_steps)
    def _():
        lax.fori_loop(0, tt // ch, lambda c, z: (issue_rows(i + 1, 1 - slot, c * ch), z)[1], 0)

    w = w_ref[...]
    y = DEEPNORM_ALPHA * h_ref[...]
    for k in range(TOP_K):
        y = y + w[:, k:k + 1] * buf_ref[slot, k]
    o_ref[...] = _layer_norm(y, g_ref[...], b_ref[...])


def _combine(h1, pos_tiles, wts_t, ys, ln2_g, ln2_b, tt):
    n, d = h1.shape
    n_steps = n // tt
    grid_spec = pltpu.PrefetchScalarGridSpec(
        num_scalar_prefetch=1,
        grid=(n_steps,),
        in_specs=[pl.BlockSpec((tt, d), lambda i, pos: (i, 0)),
                  pl.BlockSpec((tt, TOP_K), lambda i, pos: (i, 0)),
                  pl.BlockSpec((1, d), lambda i, pos: (0, 0)),
                  pl.BlockSpec((1, d), lambda i, pos: (0, 0)),
                  pl.BlockSpec(memory_space=pl.ANY)],
        out_specs=pl.BlockSpec((tt, d), lambda i, pos: (i, 0)),
        scratch_shapes=[pltpu.VMEM((2, TOP_K, tt, d), F32), pltpu.SemaphoreType.DMA((2,))],
    )
    return pl.pallas_call(
        functools.partial(_combine_kernel, tt=tt, n_steps=n_steps),
        grid_spec=grid_spec,
        out_shape=jax.ShapeDtypeStruct((n, d), F32),
        compiler_params=_params(1),
        name="combine",
    )(pos_tiles, h1, wts_t, ln2_g, ln2_b, ys)


def _routing_tables(counts, eid, rank, n_tok):
    n_exp = counts.shape[0]
    e_ids = jnp.arange(n_exp, dtype=I32)
    nsub = (counts + SLOT_ROWS - 1) // SLOT_ROWS
    ngrp = (nsub + GROUP_SUBTILES - 1) // GROUP_SUBTILES
    base = nsub // jnp.maximum(ngrp, 1)
    rem = nsub - base * ngrp
    first_sub = jnp.cumsum(nsub) - nsub
    pos = jnp.sum(jnp.where(eid[..., None] == e_ids, first_sub * SLOT_ROWS, 0), axis=-1) + rank

    n_subtiles = (n_tok * TOP_K) // SLOT_ROWS + n_exp
    max_groups = (n_subtiles + GROUP_SUBTILES - 1) // GROUP_SUBTILES + n_exp
    grp_end = jnp.cumsum(ngrp)
    g_ids = jnp.arange(max_groups, dtype=I32)
    g_exp = jnp.minimum(jnp.sum((grp_end[None, :] <= g_ids[:, None]).astype(I32), axis=1), n_exp - 1)
    pick = lambda tab: jnp.sum(jnp.where(g_exp[:, None] == e_ids[None, :], tab[None, :], 0), axis=1)
    local = g_ids - pick(grp_end - ngrp)
    g_base, g_rem = pick(base), pick(rem)
    g_nsub = g_base + (local < g_rem).astype(I32)
    g_first = pick(first_sub) + local * g_base + jnp.minimum(local, g_rem)
    used = jnp.stack([grp_end[-1], jnp.sum(nsub)]).astype(I32)
    return dict(pos=pos, n_slots=n_subtiles * SLOT_ROWS, used=used,
                group_expert=g_exp.astype(I32), group_first=g_first.astype(I32), group_nsub=g_nsub.astype(I32),
                expert_last=(first_sub + nsub - 1).astype(I32), expert_nsub=nsub.astype(I32))


def _tiles(seq, n_tok):
    return dict(
        ln_rows=min(1024, n_tok),
        conv_rows=min(1024, seq), conv_cols=512,
        pool_rows=min(1024, seq),
        merge_rows=min(1024, n_tok), merge_cols=512,
        mix_rows=min(512, n_tok),
        route_rows=min(256, n_tok), dispatch_rows=min(2048, n_tok),
        expert_cols=256,
    )


def kernel(x, meta_tokens, ln_in_g, ln_in_b, w_in, conv_w, w_a_out, pool_w, pool_scale, w_o, ln1_g, ln1_b,
           router_w, router_b, w_gate_up, b_gate_up, w_down, b_down, ln2_g, ln2_b):
    bsz, seq, d = x.shape
    assert w_in.shape[0] == DEPTH and meta_tokens.shape[0] == N_META == HALO
    d_conv = conv_w.shape[-1]
    n_groups, pool_cin, pool_cout = pool_w.shape[1:]
    d_pool = n_groups * pool_cin
    assert n_groups == len(POOL_WINDOWS) and n_groups * pool_cout == d
    n_exp = router_w.shape[-1]
    n_tok = bsz * seq
    t = _tiles(seq, n_tok)
    row = lambda v: v.reshape(1, -1).astype(F32)

    w_in_b = w_in[0].astype(BF16)
    x2 = x.reshape(n_tok, d)
    lig, lib = row(ln_in_g), row(ln_in_b)

    hx2 = _ln_cast(x2, lig, lib, t["ln_rows"])
    hm = _ln_cast(meta_tokens.astype(F32), lig, lib, N_META)
    hx = hx2.reshape(bsz, seq, d)

    a = _conv_branch(hx, hm, w_in_b, conv_w[0], d_conv, t["conv_rows"], min(t["conv_cols"], d_conv))
    yb = _pool_branch(hx, hm, w_in_b, pool_w[0].astype(BF16), row(pool_scale[0]), 3 * d_conv, t["pool_rows"])
    m2 = _merge(hx2, a.reshape(n_tok, d_conv), yb.reshape(n_tok, d), w_in_b, w_a_out[0].astype(BF16),
                3 * d_conv + d_pool, t["merge_rows"], min(t["merge_cols"], d))

    rw = jnp.pad(router_w[0], ((0, 0), (0, ROUTER_LANES - n_exp)))
    rw_hi = rw.astype(BF16)
    rw_lo = (rw - rw_hi.astype(F32)).astype(BF16)
    h1, h1p, eid, wts, rank, cnt = _mix_ln1(m2, x2, lig, lib, w_o[0].astype(BF16), row(ln1_g[0]), row(ln1_b[0]),
                                            rw_hi, rw_lo, router_b[0].reshape(n_exp, 1), t["mix_rows"])

    rt = _routing_tables(cnt[:, 0].astype(I32), eid, rank, n_tok)
    def pos_tiles_for(rows):
        return rt["pos"].reshape(TOP_K, n_tok // rows, rows).transpose(1, 0, 2).reshape(-1)

    tt = t["route_rows"]
    pos_tiles = pos_tiles_for(tt)
    xs = _dispatch(h1p, pos_tiles_for(t["dispatch_rows"]), rt["expert_last"], rt["expert_nsub"], rt["used"],
                   rt["n_slots"], t["dispatch_rows"])
    ys = _experts(xs, rt["group_expert"], rt["group_first"], rt["group_nsub"], rt["used"],
                  w_gate_up[0], b_gate_up[0], w_down[0], b_down[0], min(t["expert_cols"], w_down.shape[-2]))
    out = _combine(h1, pos_tiles, wts.T, ys, row(ln2_g[0]), row(ln2_b[0]), tt)
    return out.reshape(bsz, seq, d)
```

```python
import functools

import jax
import jax.numpy as jnp
from jax import lax
from jax.experimental import pallas as pl
from jax.experimental.pallas import tpu as pltpu

F32 = jnp.float32
BF16 = jnp.bfloat16
I32 = jnp.int32
U32 = jnp.uint32

N_META = 16
CONV_K = 3
POOL_WINDOWS = (2, 4, 8, 16)
TOP_K = 4
SWIGLU_LIMIT = 7.0
SWIGLU_ALPHA = 1.702
LN_EPS = 1e-5
DEPTH = 1
DEEPNORM_ALPHA = (2.0 * DEPTH) ** 0.25

HALO = 16
POOL_PAD = 128
POOL_CHUNK = 256
SLOT_ROWS = 256
GROUP_SUBTILES = 9
CAST_SLICES = 8
STAGE_DEPTH = 8
MIX_HALF = 256
ROUTER_LANES = 128
COMBINE_CHUNK = 16
V7X_VMEM_LIMIT = 58 * 1024 * 1024

_ARB = "arbitrary"


def _params(n_axes):
    return pltpu.CompilerParams(dimension_semantics=(_ARB,) * n_axes, vmem_limit_bytes=V7X_VMEM_LIMIT)


def _layer_norm(x, g, b):
    mu = jnp.mean(x, axis=-1, keepdims=True)
    xc = x - mu
    var = jnp.mean(xc * xc, axis=-1, keepdims=True)
    return xc * lax.rsqrt(var + LN_EPS) * g + b


def _dot(a, b):
    return jnp.dot(a, b, preferred_element_type=F32)


def _pack_pairs(lo_bf16, hi_bf16):
    lo = lax.bitcast_convert_type(lo_bf16.astype(F32), U32) >> 16
    hi = lax.bitcast_convert_type(hi_bf16.astype(F32), U32) & jnp.uint32(0xFFFF0000)
    return hi | lo


def _unpack_pairs(words):
    lo = lax.bitcast_convert_type(words << 16, F32).astype(BF16)
    hi = lax.bitcast_convert_type(words & jnp.uint32(0xFFFF0000), F32).astype(BF16)
    return lo, hi


def _ln_cast_kernel(x_ref, g_ref, b_ref, o_ref):
    o_ref[...] = _layer_norm(x_ref[...], g_ref[...], b_ref[...]).astype(o_ref.dtype)


def _ln_cast(x2d, g, b, tr):
    n, d = x2d.shape
    return pl.pallas_call(
        _ln_cast_kernel,
        grid=(n // tr,),
        in_specs=[pl.BlockSpec((tr, d), lambda i: (i, 0)),
                  pl.BlockSpec((1, d), lambda i: (0, 0)),
                  pl.BlockSpec((1, d), lambda i: (0, 0))],
        out_specs=pl.BlockSpec((tr, d), lambda i: (i, 0)),
        out_shape=jax.ShapeDtypeStruct((n, d), BF16),
        compiler_params=_params(1),
        name="ln_cast",
    )(x2d, g, b)


def _conv_kernel(h_ref, halo_ref, hm_ref, wb_ref, wc_ref, wu_ref, cw_ref, o_ref, he_ref, cu_ref, *, tm):
    i = pl.program_id(1)
    j = pl.program_id(2)

    @pl.when(j == 0)
    def _():
        he_ref[HALO:, :] = h_ref[0]

    @pl.when((j == 0) & (i == 0))
    def _():
        he_ref[0:HALO, :] = hm_ref[...]

    @pl.when((j == 0) & (i > 0))
    def _():
        he_ref[0:HALO, :] = halo_ref[0]

    he = he_ref[...]
    cu_ref[...] = _dot(he, wc_ref[...]) * _dot(he, wu_ref[...])
    bg = _dot(he_ref[HALO:, :], wb_ref[...])
    cw = cw_ref[...]
    y = cw[CONV_K - 1:CONV_K, :] * cu_ref[HALO:HALO + tm, :]
    for k in range(CONV_K - 1):
        s = HALO - (CONV_K - 1) + k
        y = y + cw[k:k + 1, :] * cu_ref[s:s + tm, :]
    o_ref[0] = (bg * y).astype(o_ref.dtype)


def _conv_branch(hx, hm, w_in_b, conv_w, d_conv, tm, tc):
    bsz, seq, d = hx.shape
    nc = d_conv // tc
    hb = tm // HALO
    return pl.pallas_call(
        functools.partial(_conv_kernel, tm=tm),
        grid=(bsz, seq // tm, nc),
        in_specs=[pl.BlockSpec((1, tm, d), lambda b, i, j: (b, i, 0)),
                  pl.BlockSpec((1, HALO, d), lambda b, i, j: (b, jnp.maximum(i * hb - 1, 0), 0)),
                  pl.BlockSpec((HALO, d), lambda b, i, j: (0, 0)),
                  pl.BlockSpec((d, tc), lambda b, i, j: (0, j)),
                  pl.BlockSpec((d, tc), lambda b, i, j: (0, nc + j)),
                  pl.BlockSpec((d, tc), lambda b, i, j: (0, 2 * nc + j)),
                  pl.BlockSpec((CONV_K, tc), lambda b, i, j: (0, j))],
        out_specs=pl.BlockSpec((1, tm, tc), lambda b, i, j: (b, i, j)),
        out_shape=jax.ShapeDtypeStruct((bsz, seq, d_conv), BF16),
        scratch_shapes=[pltpu.VMEM((tm + HALO, d), BF16), pltpu.VMEM((tm + HALO, tc), F32)],
        compiler_params=_params(3),
        name="conv_branch",
    )(hx, hx, hm, w_in_b, w_in_b, w_in_b, conv_w)


def _pool_kernel(h_ref, halo_ref, hm_ref, wv_ref, band_ref, pw_ref, ps_ref, o_ref, he_ref, vh_ref, vl_ref, *, tm):
    i = pl.program_id(1)
    lead = POOL_PAD - HALO
    he_ref[0:lead, :] = jnp.zeros((lead, he_ref.shape[1]), he_ref.dtype)
    he_ref[POOL_PAD:, :] = h_ref[0]

    @pl.when(i == 0)
    def _():
        he_ref[lead:POOL_PAD, :] = hm_ref[...]

    @pl.when(i > 0)
    def _():
        he_ref[lead:POOL_PAD, :] = halo_ref[0]

    v = _dot(he_ref[...], wv_ref[...])
    v_hi = v.astype(BF16)
    vh_ref[...] = v_hi
    vl_ref[...] = (v - v_hi.astype(F32)).astype(BF16)
    ng, cin, cout = pw_ref.shape
    rows = min(POOL_CHUNK, tm)
    for c in range(tm // rows):
        r0 = c * rows
        for g in range(ng):
            band = band_ref[g]
            win = (slice(r0, r0 + rows + POOL_PAD), slice(g * cin, (g + 1) * cin))
            pooled = _dot(band, vh_ref[win]) + _dot(band, vl_ref[win])
            y = _dot(pooled.astype(BF16), pw_ref[g]) * ps_ref[:, g * cout:(g + 1) * cout]
            o_ref[0, r0:r0 + rows, g * cout:(g + 1) * cout] = y.astype(o_ref.dtype)


def _pool_band(rows):
    t = jnp.arange(rows, dtype=I32)[:, None] + POOL_PAD
    s = jnp.arange(rows + POOL_PAD, dtype=I32)[None, :]
    bands = []
    for w in POOL_WINDOWS:
        inside = ((s <= t) & (s > t - w)).astype(F32) / w
        bands.append(inside - (s == t).astype(F32))
    return jnp.stack(bands).astype(BF16)


def _pool_branch(hx, hm, w_in_b, pool_w_b, pool_scale, col0, tm):
    bsz, seq, d = hx.shape
    ng, cin, cout = pool_w_b.shape
    d_pool = ng * cin
    hb = tm // HALO
    band = _pool_band(min(POOL_CHUNK, tm))
    return pl.pallas_call(
        functools.partial(_pool_kernel, tm=tm),
        grid=(bsz, seq // tm),
        in_specs=[pl.BlockSpec((1, tm, d), lambda b, i: (b, i, 0)),
                  pl.BlockSpec((1, HALO, d), lambda b, i: (b, jnp.maximum(i * hb - 1, 0), 0)),
                  pl.BlockSpec((HALO, d), lambda b, i: (0, 0)),
                  pl.BlockSpec((d, d_pool), lambda b, i: (0, col0 // d_pool)),
                  pl.BlockSpec(band.shape, lambda b, i: (0, 0, 0)),
                  pl.BlockSpec((ng, cin, cout), lambda b, i: (0, 0, 0)),
                  pl.BlockSpec((1, ng * cout), lambda b, i: (0, 0))],
        out_specs=pl.BlockSpec((1, tm, ng * cout), lambda b, i: (b, i, 0)),
        out_shape=jax.ShapeDtypeStruct((bsz, seq, ng * cout), BF16),
        scratch_shapes=[pltpu.VMEM((tm + POOL_PAD, d), BF16),
                        pltpu.VMEM((tm + POOL_PAD, d_pool), BF16),
                        pltpu.VMEM((tm + POOL_PAD, d_pool), BF16)],
        compiler_params=_params(2),
        name="pool_branch",
    )(hx, hx, hm, w_in_b, band, pool_w_b, pool_scale)


def _merge_kernel(h_ref, a_ref, yb_ref, wga_ref, wgb_ref, wao_ref, o_ref):
    h = h_ref[...]
    ga = jax.nn.sigmoid(_dot(h, wga_ref[...]))
    gb = jax.nn.sigmoid(_dot(h, wgb_ref[...]))
    ya = _dot(a_ref[...], wao_ref[...])
    o_ref[...] = (ga * ya + gb * yb_ref[...].astype(F32)).astype(o_ref.dtype)


def _merge(hx2, a2, yb2, w_in_b, w_a_out_b, col0, tm, tj):
    n, d = hx2.shape
    dc = a2.shape[1]
    nj = d // tj
    return pl.pallas_call(
        _merge_kernel,
        grid=(n // tm, nj),
        in_specs=[pl.BlockSpec((tm, d), lambda i, j: (i, 0)),
                  pl.BlockSpec((tm, dc), lambda i, j: (i, 0)),
                  pl.BlockSpec((tm, tj), lambda i, j: (i, j)),
                  pl.BlockSpec((d, tj), lambda i, j: (0, col0 // tj + j)),
                  pl.BlockSpec((d, tj), lambda i, j: (0, col0 // tj + nj + j)),
                  pl.BlockSpec((dc, tj), lambda i, j: (0, j))],
        out_specs=pl.BlockSpec((tm, tj), lambda i, j: (i, j)),
        out_shape=jax.ShapeDtypeStruct((n, d), BF16),
        compiler_params=_params(2),
        name="merge",
    )(hx2, a2, yb2, w_in_b, w_in_b, w_a_out_b)


def _mix_kernel(m_ref, x_ref, lig_ref, lib_ref, wo_ref, g1_ref, b1_ref, rwh_ref, rwl_ref, rb_ref, tri_ref,
                h1_ref, h1p_ref, eid_ref, wts_ref, rank_ref, cnt_ref, carry_ref, z_ref, hh_ref, hl_ref, *, rows):
    step = pl.program_id(0)

    @pl.when(step == 0)
    def _():
        carry_ref[...] = jnp.zeros(carry_ref.shape, F32)

    for r0 in range(0, m_ref.shape[0], rows):
        rs = slice(r0, r0 + rows)
        h0 = _layer_norm(x_ref[rs, :], lig_ref[...], lib_ref[...])
        z_ref[rs, :] = DEEPNORM_ALPHA * h0 + _dot(m_ref[rs, :], wo_ref[...])

    for r0 in range(0, m_ref.shape[0], rows):
        rs = slice(r0, r0 + rows)
        h1 = _layer_norm(z_ref[rs, :], g1_ref[...], b1_ref[...])
        h1_ref[rs, :] = h1
        h_hi = h1.astype(BF16)
        half = h1.shape[1] // 2
        h1p_ref[rs, :] = _pack_pairs(h_hi[:, :half], h_hi[:, half:])
        hh_ref[rs, :] = h_hi
        hl_ref[rs, :] = (h1 - h_hi.astype(F32)).astype(BF16)

    n_exp = rb_ref.shape[0]
    rwh = rwh_ref[...]
    logits_t = _dot(hh_ref[...], rwh) + _dot(hl_ref[...], rwh) + _dot(hh_ref[...], rwl_ref[...])
    logits = logits_t.T[0:n_exp, :] + rb_ref[...]
    e_iota = lax.broadcasted_iota(I32, logits.shape, 0)
    work = logits
    vals, hots = [], []
    for k in range(TOP_K):
        m = jnp.max(work, axis=0, keepdims=True)
        idx = jnp.min(jnp.where(work == m, e_iota, n_exp), axis=0, keepdims=True)
        hot = e_iota == idx
        vals.append(m)
        hots.append(hot)
        eid_ref[k:k + 1, :] = idx
        work = jnp.where(hot, -jnp.inf, work)
    exps = [jnp.exp(v - vals[0]) for v in vals]
    denom = exps[0]
    for e in exps[1:]:
        denom = denom + e
    for k in range(TOP_K):
        wts_ref[k:k + 1, :] = exps[k] / denom

    multi = hots[0]
    for hot in hots[1:]:
        multi = multi | hot
    multi_f = jnp.where(multi, 1.0, 0.0).astype(F32)
    prefix = _dot(multi_f.astype(BF16), tri_ref[...]) + carry_ref[:, 0:1]
    for k in range(TOP_K):
        rank_ref[k:k + 1, :] = jnp.sum(jnp.where(hots[k], prefix, 0.0), axis=0, keepdims=True).astype(I32)
    carry_ref[...] = carry_ref[...] + jnp.sum(multi_f, axis=1, keepdims=True)
    cnt_ref[...] = carry_ref[...]


def _mix_ln1(m2, x2, ln_in_g, ln_in_b, w_o_b, ln1_g, ln1_b, rw_hi, rw_lo, rb, tm):
    n, d = x2.shape
    n_exp = rb.shape[0]
    rows = min(MIX_HALF, tm)
    tri = (jnp.arange(tm, dtype=I32)[:, None] < jnp.arange(tm, dtype=I32)[None, :]).astype(BF16)
    row = lambda i: (i, 0)
    fixed = lambda i: (0, 0)
    col = lambda i: (0, i)
    return pl.pallas_call(
        functools.partial(_mix_kernel, rows=rows),
        grid=(n // tm,),
        in_specs=[pl.BlockSpec((tm, d), row), pl.BlockSpec((tm, d), row),
                  pl.BlockSpec((1, d), fixed), pl.BlockSpec((1, d), fixed),
                  pl.BlockSpec((d, d), fixed),
                  pl.BlockSpec((1, d), fixed), pl.BlockSpec((1, d), fixed),
                  pl.BlockSpec(rw_hi.shape, fixed), pl.BlockSpec(rw_lo.shape, fixed),
                  pl.BlockSpec((n_exp, 1), fixed),
                  pl.BlockSpec((tm, tm), fixed)],
        out_specs=[pl.BlockSpec((tm, d), row), pl.BlockSpec((tm, d // 2), row),
                   pl.BlockSpec((TOP_K, tm), col), pl.BlockSpec((TOP_K, tm), col), pl.BlockSpec((TOP_K, tm), col),
                   pl.BlockSpec((n_exp, 128), fixed)],
        out_shape=[jax.ShapeDtypeStruct((n, d), F32),
                   jax.ShapeDtypeStruct((n, d // 2), U32),
                   jax.ShapeDtypeStruct((TOP_K, n), I32),
                   jax.ShapeDtypeStruct((TOP_K, n), F32),
                   jax.ShapeDtypeStruct((TOP_K, n), I32),
                   jax.ShapeDtypeStruct((n_exp, 128), F32)],
        scratch_shapes=[pltpu.VMEM((n_exp, 128), F32), pltpu.VMEM((tm, d), F32),
                        pltpu.VMEM((tm, d), BF16), pltpu.VMEM((tm, d), BF16)],
        compiler_params=_params(1),
        name="mix_ln1",
    )(m2, x2, ln_in_g, ln_in_b, w_o_b, ln1_g, ln1_b, rw_hi, rw_lo, rb, tri)


def _row_copy(src_ref, src_row, dst_ref, dst_row, sem):
    return pltpu.make_async_copy(src_ref.at[pl.ds(src_row, 1)], dst_ref.at[pl.ds(dst_row, 1)], sem)


def _subtile(q):
    return pl.ds(pl.multiple_of(q * SLOT_ROWS, SLOT_ROWS), SLOT_ROWS)


def _dispatch_kernel(pos_ref, gl_ref, gn_ref, used_ref, h_ref, xs_ref, zero_ref, sem, zsem, *, tt):
    base = pl.program_id(0) * (TOP_K * tt)

    @pl.when(pl.program_id(0) == 0)
    def _():
        zero_ref[...] = jnp.zeros(zero_ref.shape, zero_ref.dtype)
        n_exp = gl_ref.shape[0]
        n_sub = xs_ref.shape[0] // SLOT_ROWS

        def zero_copy(q):
            return pltpu.make_async_copy(zero_ref, xs_ref.at[_subtile(q)], zsem)

        def group_tail(e, start):
            @pl.when(gn_ref[e] > 0)
            def _():
                c = zero_copy(gl_ref[e])
                c.start() if start else c.wait()

        def unused(q, start):
            c = zero_copy(q)
            c.start() if start else c.wait()

        for start in (True, False):
            lax.fori_loop(0, n_exp, lambda e, c: (group_tail(e, start), c)[1], 0)
            lax.fori_loop(used_ref[1], n_sub, lambda q, c: (unused(q, start), c)[1], 0)

    def issue(t, c):
        for k in range(TOP_K):
            _row_copy(h_ref, t, xs_ref, pos_ref[base + k * tt + t], sem).start(priority=k % 2)
        return c

    lax.fori_loop(0, tt, issue, 0)
    for k in range(TOP_K):
        pltpu.make_async_copy(h_ref, xs_ref.at[pl.ds(0, tt)], sem).wait()


def _dispatch(h1p, pos_tiles, group_last, group_nsub, used, n_slots, tt):
    n, dw = h1p.shape
    grid_spec = pltpu.PrefetchScalarGridSpec(
        num_scalar_prefetch=4,
        grid=(n // tt,),
        in_specs=[pl.BlockSpec((tt, dw), lambda i, *_: (i, 0))],
        out_specs=pl.BlockSpec(memory_space=pl.ANY),
        scratch_shapes=[pltpu.VMEM((SLOT_ROWS, dw), h1p.dtype),
                        pltpu.SemaphoreType.DMA(()), pltpu.SemaphoreType.DMA(())],
    )
    return pl.pallas_call(
        functools.partial(_dispatch_kernel, tt=tt),
        grid_spec=grid_spec,
        out_shape=jax.ShapeDtypeStruct((n_slots, dw), h1p.dtype),
        compiler_params=_params(1),
        name="dispatch",
    )(pos_tiles, group_last, group_nsub, used, h1p)


def _expert_kernel(ge_ref, gs_ref, gn_ref, nu_ref, xs_ref, wg_ref, wl_ref, wd_ref, bg_ref, bl_ref, bd_ref,
                   ys_ref, xb_ref, acc_ref, stage_ref, wglb_ref, wdb_ref, sem_in, sem_out, *, nj):
    t = pl.program_id(0)
    n_groups = nu_ref[0]
    n_items = n_groups * nj
    c = jnp.maximum(t - 1, 0)
    s = c // nj
    j = c - s * nj
    cast_slot = t % 2
    use_slot = 1 - cast_slot
    tn, d = wdb_ref.shape[1:]
    half = d // 2
    rows = _subtile

    def cast_slice(i):
        r = pl.ds(pl.multiple_of(i * (d // CAST_SLICES), d // CAST_SLICES), d // CAST_SLICES)
        wglb_ref[cast_slot, r, 0:tn] = wg_ref[0, r, :].astype(BF16)
        wglb_ref[cast_slot, r, tn:] = wl_ref[0, r, :].astype(BF16)
        r = pl.ds(pl.multiple_of(i * (tn // CAST_SLICES), tn // CAST_SLICES), tn // CAST_SLICES)
        wdb_ref[cast_slot, r, :] = wd_ref[0, r, :].astype(BF16)

    def cast_range(lo, hi):
        lax.fori_loop(lo, hi, lambda i, z: (cast_slice(i), z)[1], 0)

    def out_copy(first_sub, q):
        return pltpu.make_async_copy(acc_ref.at[rows(q)], ys_ref.at[rows(first_sub + q)], sem_out.at[q])

    def in_copy(first_sub, q, slot):
        return pltpu.make_async_copy(xs_ref.at[rows(first_sub + q)], stage_ref.at[slot], sem_in.at[slot])

    def prefetch(first_sub, nsub):
        lax.fori_loop(0, jnp.minimum(nsub, stage_ref.shape[0]),
                      lambda q, z: (in_copy(first_sub, q, q).start(), z)[1], 0)

    @pl.when(t == 0)
    def _():
        cast_range(0, CAST_SLICES)

    @pl.when((t >= 1) & (t <= n_items))
    def _():
        first_sub = gs_ref[s]
        nsub = gn_ref[s]

        @pl.when(j == 0)
        def _():
            sp = jnp.maximum(s - 1, 0)
            prev_first = gs_ref[sp]
            prev_nsub = jnp.where(s > 0, gn_ref[sp], 0)
            depth = stage_ref.shape[0]

            @pl.when(s == 0)
            def _():
                prefetch(first_sub, nsub)

            def load(q, z):
                slot = q % depth
                in_copy(first_sub, q, slot).wait()
                lo, hi = _unpack_pairs(stage_ref[slot])
                xb_ref[rows(q), 0:half] = lo
                xb_ref[rows(q), half:] = hi

                @pl.when(q + depth < nsub)
                def _():
                    in_copy(first_sub, q + depth, slot).start()

                @pl.when(q < prev_nsub)
                def _():
                    out_copy(prev_first, q).wait()

                acc_ref[rows(q), :] = jnp.broadcast_to(bd_ref[0], (SLOT_ROWS, d))
                return z

            lax.fori_loop(0, nsub, load, 0)
            lax.fori_loop(nsub, prev_nsub, lambda q, z: (out_copy(prev_first, q).wait(), z)[1], 0)

        def run(q, n):
            r = pl.ds(pl.multiple_of(q * SLOT_ROWS, SLOT_ROWS), n * SLOT_ROWS)
            x = xb_ref[r, :]
            gl = _dot(x, wglb_ref[use_slot])
            gate = gl[:, 0:tn] + bg_ref[0]
            lin = gl[:, tn:] + bl_ref[0]
            gate = jnp.minimum(gate, SWIGLU_LIMIT)
            lin = jnp.clip(lin, -SWIGLU_LIMIT, SWIGLU_LIMIT)
            act = (lin + 1.0) * gate * jax.nn.sigmoid(SWIGLU_ALPHA * gate)
            acc_ref[r, :] += _dot(act.astype(BF16), wdb_ref[use_slot])

            @pl.when(j == nj - 1)
            def _():
                for u in range(n):
                    out_copy(first_sub, q + u).start()

        a = nsub // 3
        r = nsub - 3 * a
        two_quads = (r == 2) & (a >= 2)
        first_n = jnp.where(nsub == 1, 1, jnp.where(r == 0, 3, jnp.where((r == 1) | two_quads, 4, 2)))
        for n in (1, 2, 3, 4):
            @pl.when(first_n == n)
            def _():
                run(0, n)
                for i in range(CAST_SLICES):
                    cast_slice(i)

        @pl.when(two_quads)
        def _():
            run(4, 4)

        done = first_n + jnp.where(two_quads, 4, 0)
        lax.fori_loop(0, (nsub - done) // 3, lambda p, z: (run(done + 3 * p, 3), z)[1], 0)

        @pl.when((j == nj - 1) & (s + 1 < n_groups))
        def _():
            nxt = jnp.minimum(s + 1, n_groups - 1)
            prefetch(gs_ref[nxt], gn_ref[nxt])

        @pl.when(t == n_items)
        def _():
            lax.fori_loop(0, nsub, lambda q, z: (out_copy(first_sub, q).wait(), z)[1], 0)
            acc_ref[rows(0), :] = jnp.zeros((SLOT_ROWS, d), F32)
            n_sub = ys_ref.shape[0] // SLOT_ROWS

            def tail_copy(q):
                return pltpu.make_async_copy(acc_ref.at[rows(0)], ys_ref.at[rows(q)], sem_out.at[0])

            lax.fori_loop(nu_ref[1], n_sub, lambda q, z: (tail_copy(q).start(), z)[1], 0)
            lax.fori_loop(nu_ref[1], n_sub, lambda q, z: (tail_copy(q).wait(), z)[1], 0)


def _experts(xs, group_expert, group_first, group_nsub, used, w_gate_up, b_gate_up, w_down, b_down, tn):
    n_slots, dw = xs.shape
    d = 2 * dw
    n_exp, _, de2 = w_gate_up.shape
    de = de2 // 2
    nj = de // tn
    max_items = group_expert.shape[0] * nj
    rmax = GROUP_SUBTILES * SLOT_ROWS

    def fetched(t, nu):
        w = jnp.minimum(t, nu[0] * nj - 1)
        return w // nj, w % nj

    def computed(t, nu):
        w = jnp.clip(t - 1, 0, nu[0] * nj - 1)
        return w // nj, w % nj

    def wg_map(t, ge, gs, gn, nu):
        g, j = fetched(t, nu)
        return ge[g], 0, j

    def wl_map(t, ge, gs, gn, nu):
        g, j = fetched(t, nu)
        return ge[g], 0, nj + j

    def wd_map(t, ge, gs, gn, nu):
        g, j = fetched(t, nu)
        return ge[g], j, 0

    def bg_map(t, ge, gs, gn, nu):
        g, j = computed(t, nu)
        return ge[g], 0, j

    def bl_map(t, ge, gs, gn, nu):
        g, j = computed(t, nu)
        return ge[g], 0, nj + j

    def bd_map(t, ge, gs, gn, nu):
        g, _ = computed(t, nu)
        return ge[g], 0, 0

    grid_spec = pltpu.PrefetchScalarGridSpec(
        num_scalar_prefetch=4,
        grid=(used[0] * nj + 1,),
        in_specs=[pl.BlockSpec(memory_space=pl.ANY),
                  pl.BlockSpec((1, d, tn), wg_map), pl.BlockSpec((1, d, tn), wl_map),
                  pl.BlockSpec((1, tn, d), wd_map),
                  pl.BlockSpec((1, 1, tn), bg_map), pl.BlockSpec((1, 1, tn), bl_map),
                  pl.BlockSpec((1, 1, d), bd_map)],
        out_specs=pl.BlockSpec(memory_space=pl.ANY),
        scratch_shapes=[pltpu.VMEM((rmax, d), BF16),
                        pltpu.VMEM((rmax, d), F32),
                        pltpu.VMEM((STAGE_DEPTH, SLOT_ROWS, dw), U32),
                        pltpu.VMEM((2, d, 2 * tn), BF16), pltpu.VMEM((2, tn, d), BF16),
                        pltpu.SemaphoreType.DMA((STAGE_DEPTH,)),
                        pltpu.SemaphoreType.DMA((GROUP_SUBTILES,))],
    )
    return pl.pallas_call(
        functools.partial(_expert_kernel, nj=nj),
        grid_spec=grid_spec,
        out_shape=jax.ShapeDtypeStruct((n_slots, d), F32),
        compiler_params=_params(1),
        name="experts",
    )(group_expert, group_first, group_nsub, used, xs, w_gate_up, w_gate_up, w_down,
      b_gate_up.reshape(n_exp, 1, de2), b_gate_up.reshape(n_exp, 1, de2), b_down.reshape(n_exp, 1, d))


def _combine_kernel(pos_ref, h_ref, w_ref, g_ref, b_ref, ys_ref, o_ref, buf_ref, sem, *, tt, n_steps):
    i = pl.program_id(0)
    slot = i % 2
    ch = min(COMBINE_CHUNK, tt)

    def issue_rows(step, dst_slot, t0):
        base = step * (TOP_K * tt)
        for r in range(ch):
            for k in range(TOP_K):
                _row_copy(ys_ref, pos_ref[base + k * tt + t0 + r], buf_ref.at[dst_slot, k], t0 + r,
                          sem.at[dst_slot]).start(priority=k % 2)

    def wait_rows(src_slot):
        for k in range(TOP_K):
            pltpu.make_async_copy(ys_ref.at[pl.ds(0, tt)], buf_ref.at[src_slot, k], sem.at[src_slot]).wait()

    @pl.when(i == 0)
    def _():
        lax.fori_loop(0, tt // ch, lambda c, z: (issue_rows(0, 0, c * ch), z)[1], 0)

    wait_rows(slot)

    @pl.when(i + 1 < n_steps)
    def _():
        lax.fori_loop(0, tt // ch, lambda c, z: (issue_rows(i + 1, 1 - slot, c * ch), z)[1], 0)

    w = w_ref[...]
    y = DEEPNORM_ALPHA * h_ref[...]
    for k in range(TOP_K):
        y = y + w[:, k:k + 1] * buf_ref[slot, k]
    o_ref[...] = _layer_norm(y, g_ref[...], b_ref[...])


def _combine(h1, pos_tiles, wts_t, ys, ln2_g, ln2_b, tt):
    n, d = h1.shape
    n_steps = n // tt
    grid_spec = pltpu.PrefetchScalarGridSpec(
        num_scalar_prefetch=1,
        grid=(n_steps,),
        in_specs=[pl.BlockSpec((tt, d), lambda i, pos: (i, 0)),
                  pl.BlockSpec((tt, TOP_K), lambda i, pos: (i, 0)),
                  pl.BlockSpec((1, d), lambda i, pos: (0, 0)),
                  pl.BlockSpec((1, d), lambda i, pos: (0, 0)),
                  pl.BlockSpec(memory_space=pl.ANY)],
        out_specs=pl.BlockSpec((tt, d), lambda i, pos: (i, 0)),
        scratch_shapes=[pltpu.VMEM((2, TOP_K, tt, d), F32), pltpu.SemaphoreType.DMA((2,))],
    )
    return pl.pallas_call(
        functools.partial(_combine_kernel, tt=tt, n_steps=n_steps),
        grid_spec=grid_spec,
        out_shape=jax.ShapeDtypeStruct((n, d), F32),
        compiler_params=_params(1),
        name="combine",
    )(pos_tiles, h1, wts_t, ln2_g, ln2_b, ys)


def _routing_tables(counts, eid, rank, n_tok):
    n_exp = counts.shape[0]
    e_ids = jnp.arange(n_exp, dtype=I32)
    nsub = (counts + SLOT_ROWS - 1) // SLOT_ROWS
    ngrp = (nsub + GROUP_SUBTILES - 1) // GROUP_SUBTILES
    base = nsub // jnp.maximum(ngrp, 1)
    rem = nsub - base * ngrp
    first_sub = jnp.cumsum(nsub) - nsub
    pos = jnp.sum(jnp.where(eid[..., None] == e_ids, first_sub * SLOT_ROWS, 0), axis=-1) + rank

    n_subtiles = (n_tok * TOP_K) // SLOT_ROWS + n_exp
    max_groups = (n_subtiles + GROUP_SUBTILES - 1) // GROUP_SUBTILES + n_exp
    grp_end = jnp.cumsum(ngrp)
    g_ids = jnp.arange(max_groups, dtype=I32)
    g_exp = jnp.minimum(jnp.sum((grp_end[None, :] <= g_ids[:, None]).astype(I32), axis=1), n_exp - 1)
    pick = lambda tab: jnp.sum(jnp.where(g_exp[:, None] == e_ids[None, :], tab[None, :], 0), axis=1)
    local = g_ids - pick(grp_end - ngrp)
    g_base, g_rem = pick(base), pick(rem)
    g_nsub = g_base + (local < g_rem).astype(I32)
    g_first = pick(first_sub) + local * g_base + jnp.minimum(local, g_rem)
    used = jnp.stack([grp_end[-1], jnp.sum(nsub)]).astype(I32)
    return dict(pos=pos, n_slots=n_subtiles * SLOT_ROWS, used=used,
                group_expert=g_exp.astype(I32), group_first=g_first.astype(I32), group_nsub=g_nsub.astype(I32),
                expert_last=(first_sub + nsub - 1).astype(I32), expert_nsub=nsub.astype(I32))


def _tiles(seq, n_tok):
    return dict(
        ln_rows=min(1024, n_tok),
        conv_rows=min(1024, seq), conv_cols=512,
        pool_rows=min(1024, seq),
        merge_rows=min(1024, n_tok), merge_cols=512,
        mix_rows=min(512, n_tok),
        route_rows=min(256, n_tok), dispatch_rows=min(2048, n_tok),
        expert_cols=256,
    )


def kernel(x, meta_tokens, ln_in_g, ln_in_b, w_in, conv_w, w_a_out, pool_w, pool_scale, w_o, ln1_g, ln1_b,
           router_w, router_b, w_gate_up, b_gate_up, w_down, b_down, ln2_g, ln2_b):
    bsz, seq, d = x.shape
    assert w_in.shape[0] == DEPTH and meta_tokens.shape[0] == N_META == HALO
    d_conv = conv_w.shape[-1]
    n_groups, pool_cin, pool_cout = pool_w.shape[1:]
    d_pool = n_groups * pool_cin
    assert n_groups == len(POOL_WINDOWS) and n_groups * pool_cout == d
    n_exp = router_w.shape[-1]
    n_tok = bsz * seq
    t = _tiles(seq, n_tok)
    row = lambda v: v.reshape(1, -1).astype(F32)

    w_in_b = w_in[0].astype(BF16)
    x2 = x.reshape(n_tok, d)
    lig, lib = row(ln_in_g), row(ln_in_b)

    hx2 = _ln_cast(x2, lig, lib, t["ln_rows"])
    hm = _ln_cast(meta_tokens.astype(F32), lig, lib, N_META)
    hx = hx2.reshape(bsz, seq, d)

    a = _conv_branch(hx, hm, w_in_b, conv_w[0], d_conv, t["conv_rows"], min(t["conv_cols"], d_conv))
    yb = _pool_branch(hx, hm, w_in_b, pool_w[0].astype(BF16), row(pool_scale[0]), 3 * d_conv, t["pool_rows"])
    m2 = _merge(hx2, a.reshape(n_tok, d_conv), yb.reshape(n_tok, d), w_in_b, w_a_out[0].astype(BF16),
                3 * d_conv + d_pool, t["merge_rows"], min(t["merge_cols"], d))

    rw = jnp.pad(router_w[0], ((0, 0), (0, ROUTER_LANES - n_exp)))
    rw_hi = rw.astype(BF16)
    rw_lo = (rw - rw_hi.astype(F32)).astype(BF16)
    h1, h1p, eid, wts, rank, cnt = _mix_ln1(m2, x2, lig, lib, w_o[0].astype(BF16), row(ln1_g[0]), row(ln1_b[0]),
                                            rw_hi, rw_lo, router_b[0].reshape(n_exp, 1), t["mix_rows"])

    rt = _routing_tables(cnt[:, 0].astype(I32), eid, rank, n_tok)
    def pos_tiles_for(rows):
        return rt["pos"].reshape(TOP_K, n_tok // rows, rows).transpose(1, 0, 2).reshape(-1)

    tt = t["route_rows"]
    pos_tiles = pos_tiles_for(tt)
    xs = _dispatch(h1p, pos_tiles_for(t["dispatch_rows"]), rt["expert_last"], rt["expert_nsub"], rt["used"],
                   rt["n_slots"], t["dispatch_rows"])
    ys = _experts(xs, rt["group_expert"], rt["group_first"], rt["group_nsub"], rt["used"],
                  w_gate_up[0], b_gate_up[0], w_down[0], b_down[0], min(t["expert_cols"], w_down.shape[-2]))
    out = _combine(h1, pos_tiles, wts.T, ys, row(ln2_g[0]), row(ln2_b[0]), tt)
    return out.reshape(bsz, seq, d)
```
